```python
import math
import jax, jax.numpy as jnp
from jax import lax
import numpy as np


D_MODEL = 1024
BATCH = 2
SEQ = 8192
DEPTH = 2
DEC_BATCH = 128
DEC_SEQ = 8
PAST_LEN = 8192
PAGE_SIZE = 128

N_A_LAYERS = DEPTH // 2
N_B_LAYERS = DEPTH - N_A_LAYERS
GLA_HEADS = 4
GLA_DK = D_MODEL // 2 // GLA_HEADS
GLA_DV = D_MODEL // GLA_HEADS
GLA_RANK = 16
GLA_TAU = 16.0
GLA_CHUNK = 64
IN_A_COLS = 2 * GLA_HEADS * GLA_DK + 2 * GLA_HEADS * GLA_DV + GLA_RANK
SWA_HEADS = 16
SWA_KV_HEADS = 4
SWA_GROUP = SWA_HEADS // SWA_KV_HEADS
SWA_HEAD_DIM = 64
WINDOW = 128
REL_BUCKETS = 32
REL_MAX_DIST = 128
D_FF = 4 * D_MODEL
PLE_DIM = 256
EPS = 1e-6

kernel_name = 'yoco_gla_swa_sink_decoder_step'


def rmsnorm(x, gain):
    xf = x.astype(jnp.float32)
    xf = xf * lax.rsqrt(jnp.mean(xf * xf, axis=-1, keepdims=True) + EPS)
    return (xf * gain.astype(jnp.float32)).astype(x.dtype)


def rel_bucket(d):
    max_exact = REL_BUCKETS // 2
    df = jnp.maximum(d, 1).astype(jnp.float32)
    large = max_exact + (jnp.log(df / max_exact) / math.log(REL_MAX_DIST / max_exact)
                         * (REL_BUCKETS - max_exact)).astype(jnp.int32)
    large = jnp.minimum(large, REL_BUCKETS - 1)
    return jnp.where(d < max_exact, d, large)


def gla_recurrence(q, k, v, lg, s0):
    b, s = q.shape[:2]
    c = GLA_CHUNK if s % GLA_CHUNK == 0 else s
    nc = s // c

    def to_chunks(t):
        return t.reshape(b, nc, c, GLA_HEADS, t.shape[-1]).transpose(1, 0, 3, 2, 4)

    causal = jnp.tril(jnp.ones((c, c), bool))[:, :, None]

    def step(state, inp):
        qc, kc, vc, lc = [t.astype(jnp.float32) for t in inp]
        bc = jnp.cumsum(lc, axis=-2)
        diff = bc[:, :, :, None, :] - bc[:, :, None, :, :]
        decay = jnp.where(causal, jnp.exp(jnp.where(causal, diff, 0.0)), 0.0)
        attn = jnp.einsum('bhid,bhjd,bhijd->bhij', qc, kc, decay)
        o = jnp.einsum('bhij,bhje->bhie', attn, vc) + jnp.einsum('bhid,bhde->bhie', qc * jnp.exp(bc), state)
        last = bc[:, :, -1:, :]
        new_state = (jnp.exp(last[:, :, 0, :])[..., None] * state
                     + jnp.einsum('bhjd,bhje->bhde', kc * jnp.exp(last - bc), vc))
        return new_state, o

    s_fin, o = lax.scan(step, s0.astype(jnp.float32),
                        (to_chunks(q), to_chunks(k), to_chunks(v), to_chunks(lg)))
    o = o.transpose(1, 0, 3, 2, 4).reshape(b, s, GLA_HEADS, GLA_DV)
    return o, s_fin


def gla_mixer(hn, s0, w_in, w_a2, b_a2, o_gain, w_out):
    b, s, _ = hn.shape
    nq = GLA_HEADS * GLA_DK
    nv = GLA_HEADS * GLA_DV
    proj = hn @ w_in
    q, k, v, g, a = jnp.split(proj, [nq, 2 * nq, 2 * nq + nv, 2 * nq + 2 * nv], axis=-1)
    lg = jax.nn.log_sigmoid((a @ w_a2 + b_a2).astype(jnp.float32)) / GLA_TAU
    o, s_fin = gla_recurrence(q.reshape(b, s, GLA_HEADS, GLA_DK) * GLA_DK ** -0.5,
                              k.reshape(b, s, GLA_HEADS, GLA_DK),
                              v.reshape(b, s, GLA_HEADS, GLA_DV),
                              lg.reshape(b, s, GLA_HEADS, GLA_DK), s0)
    o = rmsnorm(o.astype(hn.dtype), o_gain).reshape(b, s, nv)
    return (o * jax.nn.silu(g)) @ w_out, s_fin.astype(hn.dtype)


def swa_core(q, k, v, qpos, kpos, rel_bias, sinks):
    n, nq = qpos.shape
    nk = kpos.shape[1]
    d = qpos[:, :, None] - kpos[:, None, :]
    valid = (d >= 0) & (d < WINDOW) & (kpos[:, None, :] >= 0)
    bias = rel_bias.astype(jnp.float32).T[:, rel_bucket(jnp.maximum(d, 0))]
    bias = bias.reshape(SWA_KV_HEADS, SWA_GROUP, n, nq, nk).transpose(2, 0, 1, 3, 4)
    logits = jnp.einsum('bnqhgd,bnshd->bnhgqs', q, k).astype(jnp.float32) * SWA_HEAD_DIM ** -0.5 + bias
    logits = jnp.where(valid[None, :, None, None], logits, -jnp.inf)
    sink = sinks.astype(jnp.float32).reshape(SWA_KV_HEADS, SWA_GROUP)[:, :, None, None]
    m = jnp.maximum(jnp.max(logits, axis=-1, keepdims=True), sink)
    e = jnp.exp(logits - m)
    p = e / (jnp.sum(e, axis=-1, keepdims=True) + jnp.exp(sink - m))
    return jnp.einsum('bnhgqs,bnshd->bnqhgd', p.astype(v.dtype), v)


def swa_mixer(hn, k_new, v_new, past_k, past_v, w_q, w_o, sinks, rel_bias):
    b, s, _ = hn.shape
    q = (hn @ w_q).reshape(b, s, SWA_KV_HEADS, SWA_GROUP, SWA_HEAD_DIM)
    if past_k is None:
        nb = s // WINDOW
        q = q.reshape(b, nb, WINDOW, SWA_KV_HEADS, SWA_GROUP, SWA_HEAD_DIM)

        def band(t):
            tb = jnp.concatenate([jnp.zeros_like(t[:, :WINDOW]), t], axis=1)
            tb = tb.reshape(b, nb + 1, WINDOW, SWA_KV_HEADS, SWA_HEAD_DIM)
            return jnp.concatenate([tb[:, :-1], tb[:, 1:]], axis=2)

        kb, vb = band(k_new), band(v_new)
        qpos = jnp.arange(s, dtype=jnp.int32).reshape(nb, WINDOW)
        kpos = ((jnp.arange(nb, dtype=jnp.int32)[:, None] - 1) * WINDOW
                + jnp.arange(2 * WINDOW, dtype=jnp.int32)[None, :])
    else:
        n_past = past_k.shape[1]
        kb = jnp.concatenate([past_k, k_new], axis=1)[:, None]
        vb = jnp.concatenate([past_v, v_new], axis=1)[:, None]
        q = q[:, None]
        qpos = (PAST_LEN + jnp.arange(s, dtype=jnp.int32))[None]
        kpos = (PAST_LEN - n_past + jnp.arange(n_past + s, dtype=jnp.int32))[None]
    o = swa_core(q, kb, vb, qpos, kpos, rel_bias, sinks)
    return o.reshape(b, s, SWA_HEADS * SWA_HEAD_DIM) @ w_o


def sq_relu_mlp(hn, w_up, w_down):
    return jnp.square(jax.nn.relu(hn @ w_up)) @ w_down


def trunk(x, p, gla_state, past_k, past_v, w):
    h = x
    b, s, _ = x.shape
    gla_states = []
    k_sh = v_sh = None
    for i in range(DEPTH):
        hn = rmsnorm(h, w['norm_mix'][i])
        if i < N_A_LAYERS:
            y, s_fin = gla_mixer(hn, gla_state[i], w['w_in_a'][i], w['w_a2'][i], w['b_a2'][i],
                                 w['gla_o_gain'][i], w['w_out_a'][i])
            gla_states.append(s_fin)
        else:
            j = i - N_A_LAYERS
            y = swa_mixer(hn, k_sh, v_sh, past_k, past_v, w['w_q_b'][j], w['w_o_b'][j],
                          w['sinks'][j], w['rel_bias'])
        h = h + y
        h = h + sq_relu_mlp(rmsnorm(h, w['norm_mlp'][i]), w['w_up'][i], w['w_down'][i])
        gate = jax.nn.sigmoid(rmsnorm(h, w['norm_ple'][i]) @ w['w_ple_gate'][i])
        h = h + gate * (p[i] @ w['w_ple'][i])
        if i == N_A_LAYERS - 1:
            kv = rmsnorm(h, w['norm_kv']) @ w['w_kv']
            k_sh, v_sh = jnp.split(kv, 2, axis=-1)
            k_sh = k_sh.reshape(b, s, SWA_KV_HEADS, SWA_HEAD_DIM)
            v_sh = v_sh.reshape(b, s, SWA_KV_HEADS, SWA_HEAD_DIM)
    y = rmsnorm(h, w['norm_final'])
    if past_k is None:
        keep = min(WINDOW, s)
        new_k, new_v = k_sh[:, s - keep:], v_sh[:, s - keep:]
    else:
        n_past = past_k.shape[1]
        new_k = jnp.concatenate([past_k, k_sh], axis=1)[:, -n_past:]
        new_v = jnp.concatenate([past_v, v_sh], axis=1)[:, -n_past:]
    return y, jnp.stack(gla_states), new_k, new_v


def setup_inputs(seed: int = 0) -> dict:
    key = jax.random.key(seed)
    ks = iter(jax.random.split(key, 32))

    def nrm(shape, scale):
        return jax.random.normal(next(ks), shape, jnp.float32) * scale

    def gain(shape):
        return 1.0 + nrm(shape, 0.02)

    n_win = min(WINDOW, PAST_LEN)
    return {
        'x_prompt': nrm((BATCH, SEQ, D_MODEL), 1.0),
        'x_sample': nrm((DEC_BATCH, DEC_SEQ, D_MODEL), 1.0),
        'state_gla': nrm((N_A_LAYERS, DEC_BATCH, GLA_HEADS, GLA_DK, GLA_DV), 0.5),
        'cache_win_k': nrm((DEC_BATCH, n_win, SWA_KV_HEADS, SWA_HEAD_DIM), 1.0),
        'cache_win_v': nrm((DEC_BATCH, n_win, SWA_KV_HEADS, SWA_HEAD_DIM), 1.0),
        'p_prompt': nrm((DEPTH, BATCH, SEQ, PLE_DIM), 1.0),
        'p_sample': nrm((DEPTH, DEC_BATCH, DEC_SEQ, PLE_DIM), 1.0),
        'norm_mix': gain((DEPTH, D_MODEL)),
        'norm_mlp': gain((DEPTH, D_MODEL)),
        'norm_ple': gain((DEPTH, D_MODEL)),
        'norm_kv': gain((D_MODEL,)),
        'norm_final': gain((D_MODEL,)),
        'w_in_a': nrm((N_A_LAYERS, D_MODEL, IN_A_COLS), D_MODEL ** -0.5),
        'w_a2': nrm((N_A_LAYERS, GLA_RANK, GLA_HEADS * GLA_DK), GLA_RANK ** -0.5),
        'b_a2': nrm((N_A_LAYERS, GLA_HEADS * GLA_DK), 0.1),
        'gla_o_gain': gain((N_A_LAYERS, GLA_DV)),
        'w_out_a': nrm((N_A_LAYERS, GLA_HEADS * GLA_DV, D_MODEL), (GLA_HEADS * GLA_DV) ** -0.5),
        'w_kv': nrm((D_MODEL, 2 * SWA_KV_HEADS * SWA_HEAD_DIM), D_MODEL ** -0.5),
        'w_q_b': nrm((N_B_LAYERS, D_MODEL, SWA_HEADS * SWA_HEAD_DIM), D_MODEL ** -0.5),
        'w_o_b': nrm((N_B_LAYERS, SWA_HEADS * SWA_HEAD_DIM, D_MODEL), (SWA_HEADS * SWA_HEAD_DIM) ** -0.5),
        'sinks': nrm((N_B_LAYERS, SWA_HEADS), 0.5),
        'rel_bias': nrm((REL_BUCKETS, SWA_HEADS), 0.5),
        'w_up': nrm((DEPTH, D_MODEL, D_FF), D_MODEL ** -0.5),
        'w_down': nrm((DEPTH, D_FF, D_MODEL), D_FF ** -0.5),
        'w_ple': nrm((DEPTH, PLE_DIM, D_MODEL), PLE_DIM ** -0.5),
        'w_ple_gate': nrm((DEPTH, D_MODEL, D_MODEL), D_MODEL ** -0.5),
    }


def reference(x_prompt, x_sample, state_gla, cache_win_k, cache_win_v, p_prompt, p_sample,
              norm_mix, norm_mlp, norm_ple, norm_kv, norm_final,
              w_in_a, w_a2, b_a2, gla_o_gain, w_out_a,
              w_kv, w_q_b, w_o_b, sinks, rel_bias,
              w_up, w_down, w_ple, w_ple_gate):
    w = dict(norm_mix=norm_mix, norm_mlp=norm_mlp, norm_ple=norm_ple, norm_kv=norm_kv,
             norm_final=norm_final, w_in_a=w_in_a, w_a2=w_a2, b_a2=b_a2, gla_o_gain=gla_o_gain,
             w_out_a=w_out_a, w_kv=w_kv, w_q_b=w_q_b, w_o_b=w_o_b, sinks=sinks, rel_bias=rel_bias,
             w_up=w_up, w_down=w_down, w_ple=w_ple, w_ple_gate=w_ple_gate)
    b = x_prompt.shape[0]
    gla0_prompt = jnp.zeros((N_A_LAYERS, b, GLA_HEADS, GLA_DK, GLA_DV), x_prompt.dtype)
    y_prompt, state_gla_prompt, cache_win_k_prompt, cache_win_v_prompt = trunk(
        x_prompt, p_prompt, gla0_prompt, None, None, w)
    y_sample, state_gla_sample, cache_win_k_sample, cache_win_v_sample = trunk(
        x_sample, p_sample, state_gla, cache_win_k, cache_win_v, w)
    return (y_prompt, y_sample, state_gla_prompt, state_gla_sample,
            cache_win_k_prompt, cache_win_v_prompt, cache_win_k_sample, cache_win_v_sample)
```

```python
import functools
import math

import numpy as np
import jax
import jax.numpy as jnp
from jax import lax
from jax.experimental import pallas as pl
from jax.experimental.pallas import tpu as pltpu

F32 = jnp.float32
BF16 = jnp.bfloat16

D_MODEL = 1024
D_FF = 4 * D_MODEL
PLE_DIM = 256
GLA_HEADS = 4
GLA_DK = 128
GLA_DV = 256
GLA_RANK = 16
GLA_TAU = 16.0
SWA_HEADS = 16
SWA_KV_HEADS = 4
SWA_GROUP = 4
SWA_HEAD_DIM = 64
WINDOW = 128
REL_BUCKETS = 32
REL_MAX_DIST = 128
EPS = 1e-6

NQ = GLA_HEADS * GLA_DK
NV = GLA_HEADS * GLA_DV
KV_COLS = SWA_KV_HEADS * SWA_HEAD_DIM
LANES = 128
RANK_PAD = LANES
NEG = -1e30
EXP_CLAMP = 80.0
VMEM_LIMIT = 56 * 1024 * 1024


def _rel_bucket_starts():
    max_exact = REL_BUCKETS // 2
    starts = {}
    for d in range(WINDOW):
        if d < max_exact:
            b = d
        else:
            b = max_exact + int(np.float32(
                np.log(np.float32(d) / np.float32(max_exact)) / np.float32(math.log(REL_MAX_DIST / max_exact))
                * np.float32(REL_BUCKETS - max_exact)))
            b = min(b, REL_BUCKETS - 1)
        starts.setdefault(b, d)
    return sorted((d, b) for b, d in starts.items())


BUCKET_STARTS = _rel_bucket_starts()


def _dot(a, b):
    return jnp.dot(a, b, preferred_element_type=F32)


def _dot_nt(a, b):
    return lax.dot_general(a, b, (((1,), (1,)), ((), ())), preferred_element_type=F32)


def _dot_tn(a, b):
    return lax.dot_general(a, b, (((0,), (0,)), ((), ())), preferred_element_type=F32)


def _rms(x, gain):
    return x * lax.rsqrt(jnp.mean(x * x, axis=-1, keepdims=True) + EPS) * gain


def _sigmoid(x):
    return 1.0 / (1.0 + jnp.exp(-x))


def _const_spec(shape):
    return pl.BlockSpec(shape, lambda *_: (0,) * len(shape))


def _params(n_axes):
    return pltpu.CompilerParams(dimension_semantics=("arbitrary",) * n_axes,
                                vmem_limit_bytes=VMEM_LIMIT)


def _gla_proj_kernel(x_ref, gain_ref, wq_ref, wk_ref, wv_ref, wg_ref, wa_ref, wa2_ref, ba2_ref,
                     q_ref, k_ref, v_ref, g_ref, lg_ref):
    hn = _rms(x_ref[...], gain_ref[...]).astype(BF16)
    q_ref[...] = (_dot(hn, wq_ref[...]) * GLA_DK ** -0.5).astype(q_ref.dtype)
    k_ref[...] = _dot(hn, wk_ref[...]).astype(k_ref.dtype)
    v_ref[...] = _dot(hn, wv_ref[...]).astype(v_ref.dtype)
    g_ref[...] = _dot(hn, wg_ref[...]).astype(g_ref.dtype)
    a = _dot(hn, wa_ref[...]).astype(BF16)
    z = _dot(a, wa2_ref[...]) + ba2_ref[...]
    log_sig = jnp.minimum(z, 0.0) - jnp.log1p(jnp.exp(-jnp.abs(z)))
    lg_ref[...] = log_sig * (1.0 / GLA_TAU)


def _gla_proj(x, gain, wq, wk, wv, wg, wa, wa2, ba2, *, tm, act_dtype):
    t = x.shape[0]
    row = lambda cols: pl.BlockSpec((tm, cols), lambda i: (i, 0))
    return pl.pallas_call(
        _gla_proj_kernel,
        grid=(t // tm,),
        in_specs=[row(D_MODEL), _const_spec((1, D_MODEL)),
                  _const_spec((D_MODEL, NQ)), _const_spec((D_MODEL, NQ)),
                  _const_spec((D_MODEL, NV)), _const_spec((D_MODEL, NV)),
                  _const_spec((D_MODEL, RANK_PAD)), _const_spec((RANK_PAD, NQ)), _const_spec((1, NQ))],
        out_specs=[row(NQ), row(NQ), row(NV), row(NV), row(NQ)],
        out_shape=[jax.ShapeDtypeStruct((t, NQ), act_dtype), jax.ShapeDtypeStruct((t, NQ), act_dtype),
                   jax.ShapeDtypeStruct((t, NV), act_dtype), jax.ShapeDtypeStruct((t, NV), act_dtype),
                   jax.ShapeDtypeStruct((t, NQ), F32)],
        compiler_params=_params(1),
        name="gla_proj",
    )(x, gain, wq, wk, wv, wg, wa, wa2, ba2)


def _cumsum_rows(x):
    c = x.shape[0]
    row = lax.broadcasted_iota(jnp.int32, x.shape, 0)
    shift = 1
    while shift < c:
        x = x + jnp.where(row >= shift, pltpu.roll(x, shift, axis=0), 0.0)
        shift *= 2
    return x


def _row_to_col(row):
    return jnp.transpose(jnp.broadcast_to(row, (LANES, LANES)))[:, :1]


def _gla_chunk(q, k, v, lg, state):
    c = q.shape[0]
    bc = _cumsum_rows(lg)
    last = bc[c - 1:c, :]
    mid = bc[c // 2 - 1:c // 2, :]
    q_in = (q * jnp.exp(jnp.clip(bc - mid, -EXP_CLAMP, EXP_CLAMP))).astype(BF16)
    k_in = (k * jnp.exp(jnp.clip(mid - bc, -EXP_CLAMP, EXP_CLAMP))).astype(BF16)
    attn = _dot_nt(q_in, k_in)
    causal = lax.broadcasted_iota(jnp.int32, (c, c), 0) >= lax.broadcasted_iota(jnp.int32, (c, c), 1)
    attn = jnp.where(causal, attn, 0.0).astype(BF16)
    q_st = (q * jnp.exp(bc)).astype(BF16)
    o = _dot(attn, v) + _dot(q_st, state.astype(BF16))
    k_st = (k * jnp.exp(last - bc)).astype(BF16)
    new_state = _row_to_col(jnp.exp(last)) * state + _dot_tn(k_st, v)
    return o, new_state


def _gla_gate(o, o_gain, g):
    return _rms(o, o_gain) * (g * _sigmoid(g))


def _gla_prompt_kernel(q_ref, k_ref, v_ref, g_ref, lg_ref, ogain_ref, mix_ref, sfin_ref, state_ref):
    ci = pl.program_id(2)

    @pl.when(ci == 0)
    def _():
        state_ref[...] = jnp.zeros_like(state_ref)

    o, new_state = _gla_chunk(q_ref[...].astype(F32), k_ref[...].astype(F32), v_ref[...],
                              lg_ref[...], state_ref[...])
    state_ref[...] = new_state
    mix_ref[...] = _gla_gate(o, ogain_ref[...], g_ref[...].astype(F32)).astype(mix_ref.dtype)

    @pl.when(ci == pl.num_programs(2) - 1)
    def _():
        sfin_ref[0, 0] = new_state


def _gla_prompt(q, k, v, g, lg, o_gain, *, batch, seq, chunk):
    nc = seq // chunk
    blk = lambda cols: pl.BlockSpec((chunk, cols), lambda b, h, c: (b * nc + c, h))
    return pl.pallas_call(
        _gla_prompt_kernel,
        grid=(batch, GLA_HEADS, nc),
        in_specs=[blk(GLA_DK), blk(GLA_DK), blk(GLA_DV), blk(GLA_DV), blk(GLA_DK),
                  _const_spec((1, GLA_DV))],
        out_specs=[blk(GLA_DV),
                   pl.BlockSpec((1, 1, GLA_DK, GLA_DV), lambda b, h, c: (b, h, 0, 0))],
        out_shape=[jax.ShapeDtypeStruct((batch * seq, NV), BF16),
                   jax.ShapeDtypeStruct((batch, GLA_HEADS, GLA_DK, GLA_DV), F32)],
        scratch_shapes=[pltpu.VMEM((GLA_DK, GLA_DV), F32)],
        compiler_params=_params(3),
        name="gla_prompt",
    )(q, k, v, g, lg, o_gain)


def _gla_sample_kernel(q_ref, k_ref, v_ref, g_ref, lg_ref, ogain_ref, s0_ref, mix_ref, sfin_ref, *, tb, s):
    def body(b, carry):
        r0 = pl.multiple_of(b * s, s)
        for h in range(GLA_HEADS):
            ks = slice(h * GLA_DK, (h + 1) * GLA_DK)
            vs = slice(h * GLA_DV, (h + 1) * GLA_DV)
            o, new_state = _gla_chunk(q_ref[pl.ds(r0, s), ks], k_ref[pl.ds(r0, s), ks],
                                      v_ref[pl.ds(r0, s), vs].astype(BF16), lg_ref[pl.ds(r0, s), ks],
                                      s0_ref[b, h])
            sfin_ref[b, h] = new_state
            mix_ref[pl.ds(r0, s), vs] = _gla_gate(o, ogain_ref[...], g_ref[pl.ds(r0, s), vs])
        return carry

    lax.fori_loop(0, tb, body, 0)


def _gla_sample(q, k, v, g, lg, o_gain, s0, *, batch, s, tb):
    rows = lambda cols: pl.BlockSpec((tb * s, cols), lambda i: (i, 0))
    st = pl.BlockSpec((tb, GLA_HEADS, GLA_DK, GLA_DV), lambda i: (i, 0, 0, 0))
    return pl.pallas_call(
        functools.partial(_gla_sample_kernel, tb=tb, s=s),
        grid=(batch // tb,),
        in_specs=[rows(NQ), rows(NQ), rows(NV), rows(NV), rows(NQ), _const_spec((1, GLA_DV)), st],
        out_specs=[rows(NV), st],
        out_shape=[jax.ShapeDtypeStruct((batch * s, NV), F32),
                   jax.ShapeDtypeStruct((batch, GLA_HEADS, GLA_DK, GLA_DV), F32)],
        compiler_params=_params(1),
        name="gla_sample",
    )(q, k, v, g, lg, o_gain, s0)


def _tail_body(h_ref, mix_ref, p_ref, wproj_ref, gmlp_ref, wup_ref, wdown_ref, gple_ref, wgate_ref,
               wple_ref):
    h = h_ref[...] + _dot(mix_ref[...].astype(BF16), wproj_ref[...])
    hn = _rms(h, gmlp_ref[...]).astype(BF16)
    ff_chunk = D_MODEL
    for j in range(D_FF // ff_chunk):
        cols = slice(j * ff_chunk, (j + 1) * ff_chunk)
        u = jnp.square(jnp.maximum(_dot(hn, wup_ref[:, cols]), 0.0)).astype(BF16)
        h = h + _dot(u, wdown_ref[cols, :])
    gate = _sigmoid(_dot(_rms(h, gple_ref[...]).astype(BF16), wgate_ref[...]))
    return h + gate * _dot(p_ref[...].astype(BF16), wple_ref[...])


def _tail0_kernel(h_ref, mix_ref, p_ref, wproj_ref, gmlp_ref, wup_ref, wdown_ref, gple_ref, wgate_ref,
                  wple_ref, gkv_ref, wk_ref, wv_ref, gq_ref, wq_ref,
                  hout_ref, kout_ref, vout_ref, qout_ref):
    h = _tail_body(h_ref, mix_ref, p_ref, wproj_ref, gmlp_ref, wup_ref, wdown_ref, gple_ref,
                   wgate_ref, wple_ref)
    hout_ref[...] = h
    hkv = _rms(h, gkv_ref[...]).astype(BF16)
    kout_ref[...] = _dot(hkv, wk_ref[...])
    vout_ref[...] = _dot(hkv, wv_ref[...])
    hq = _rms(h, gq_ref[...]).astype(BF16)
    qout_ref[...] = (_dot(hq, wq_ref[...]) * SWA_HEAD_DIM ** -0.5).astype(qout_ref.dtype)


def _tail1_kernel(h_ref, mix_ref, p_ref, wproj_ref, gmlp_ref, wup_ref, wdown_ref, gple_ref, wgate_ref,
                  wple_ref, gfin_ref, y_ref):
    h = _tail_body(h_ref, mix_ref, p_ref, wproj_ref, gmlp_ref, wup_ref, wdown_ref, gple_ref,
                   wgate_ref, wple_ref)
    y_ref[...] = _rms(h, gfin_ref[...])


def _resident(shape):
    return pl.BlockSpec(shape, lambda *_: (0,) * len(shape), pipeline_mode=pl.Buffered(1))


def _tail_common_specs(tm):
    row = lambda cols: pl.BlockSpec((tm, cols), lambda i: (i, 0))
    return [row(D_MODEL), row(D_MODEL), row(PLE_DIM),
            _resident((D_MODEL, D_MODEL)), _resident((1, D_MODEL)),
            _resident((D_MODEL, D_FF)), _resident((D_FF, D_MODEL)),
            _resident((1, D_MODEL)), _resident((D_MODEL, D_MODEL)), _resident((PLE_DIM, D_MODEL))]


def _tail0(h, mix, p, wproj, gmlp, wup, wdown, gple, wgate, wple, gkv, wk, wv, gq, wq, *, tm, q_dtype):
    t = h.shape[0]
    row = lambda cols: pl.BlockSpec((tm, cols), lambda i: (i, 0))
    return pl.pallas_call(
        _tail0_kernel,
        grid=(t // tm,),
        in_specs=_tail_common_specs(tm) + [
            _resident((1, D_MODEL)), _resident((D_MODEL, KV_COLS)), _resident((D_MODEL, KV_COLS)),
            _resident((1, D_MODEL)), _resident((D_MODEL, D_MODEL))],
        out_specs=[row(D_MODEL), row(KV_COLS), row(KV_COLS), row(D_MODEL)],
        out_shape=[jax.ShapeDtypeStruct((t, D_MODEL), F32), jax.ShapeDtypeStruct((t, KV_COLS), F32),
                   jax.ShapeDtypeStruct((t, KV_COLS), F32), jax.ShapeDtypeStruct((t, D_MODEL), q_dtype)],
        compiler_params=_params(1),
        name="tail0",
    )(h, mix, p, wproj, gmlp, wup, wdown, gple, wgate, wple, gkv, wk, wv, gq, wq)


def _tail1(h, mix, p, wproj, gmlp, wup, wdown, gple, wgate, wple, gfin, *, tm):
    t = h.shape[0]
    return pl.pallas_call(
        _tail1_kernel,
        grid=(t // tm,),
        in_specs=_tail_common_specs(tm) + [_resident((1, D_MODEL))],
        out_specs=pl.BlockSpec((tm, D_MODEL), lambda i: (i, 0)),
        out_shape=jax.ShapeDtypeStruct((t, D_MODEL), F32),
        compiler_params=_params(1),
        name="tail1",
    )(h, mix, p, wproj, gmlp, wup, wdown, gple, wgate, wple, gfin)


def _bias_table(rb_ref, head, d, valid):
    val = jnp.full(d.shape, rb_ref[BUCKET_STARTS[0][1], head], F32)
    for start, bucket in BUCKET_STARTS[1:]:
        val = jnp.where(d >= start, rb_ref[bucket, head], val)
    return jnp.where(valid, val, NEG)


def _softmax_sink(logits, sink):
    m = jnp.maximum(jnp.max(logits, axis=-1, keepdims=True), sink)
    e = jnp.exp(logits - m)
    return e / (jnp.sum(e, axis=-1, keepdims=True) + jnp.exp(sink - m))


def _lane_block(shape):
    return lax.broadcasted_iota(jnp.int32, shape, 1) // SWA_HEAD_DIM


def _swa_prompt_kernel(rb_ref, sink_ref, q_ref, kp_ref, kc_ref, vp_ref, vc_ref, o_ref, bias_ref):
    n = pl.program_id(1)
    nk = 2 * WINDOW

    @pl.when((pl.program_id(0) == 0) & (n == 0))
    def _():
        d = (WINDOW + lax.broadcasted_iota(jnp.int32, (WINDOW, nk), 0)
             - lax.broadcasted_iota(jnp.int32, (WINDOW, nk), 1))
        valid = (d >= 0) & (d < WINDOW)
        for head in range(SWA_HEADS):
            bias_ref[head] = _bias_table(rb_ref, head, d, valid)

    kblk = jnp.concatenate([kp_ref[...], kc_ref[...]], axis=0).astype(BF16)
    vblk = jnp.concatenate([vp_ref[...], vc_ref[...]], axis=0).astype(BF16)
    key_col = lax.broadcasted_iota(jnp.int32, (1, nk), 1)
    pad = jnp.where((key_col < WINDOW) & (n == 0), NEG, 0.0)
    lane_blk = _lane_block((WINDOW, KV_COLS))
    for g in range(SWA_GROUP):
        slab = q_ref[:, g * KV_COLS:(g + 1) * KV_COLS].astype(F32)
        q_stack = jnp.concatenate(
            [jnp.where(lane_blk == c, slab, 0.0) for c in range(SWA_KV_HEADS)], axis=0).astype(BF16)
        logits = _dot_nt(q_stack, kblk)
        probs = []
        for c in range(SWA_KV_HEADS):
            head = c * SWA_GROUP + g
            lg = logits[c * WINDOW:(c + 1) * WINDOW] + bias_ref[head] + pad
            probs.append(_softmax_sink(lg, sink_ref[0, head]).astype(BF16))
        wide = _dot(jnp.concatenate(probs, axis=0), vblk)
        out = jnp.zeros((WINDOW, KV_COLS), F32)
        for c in range(SWA_KV_HEADS):
            out = jnp.where(lane_blk == c, wide[c * WINDOW:(c + 1) * WINDOW], out)
        o_ref[:, g * KV_COLS:(g + 1) * KV_COLS] = out.astype(o_ref.dtype)


def _swa_prompt(rel_bias, sinks, q, k, v, *, batch, seq):
    nb = seq // WINDOW
    cur = lambda cols: pl.BlockSpec((WINDOW, cols), lambda b, n: (b * nb + n, 0))
    prev = lambda cols: pl.BlockSpec((WINDOW, cols), lambda b, n: (b * nb + jnp.maximum(n - 1, 0), 0))
    smem = pl.BlockSpec(memory_space=pltpu.SMEM)
    return pl.pallas_call(
        _swa_prompt_kernel,
        grid=(batch, nb),
        in_specs=[smem, smem, cur(D_MODEL), prev(KV_COLS), cur(KV_COLS), prev(KV_COLS), cur(KV_COLS)],
        out_specs=cur(D_MODEL),
        out_shape=jax.ShapeDtypeStruct((batch * seq, D_MODEL), BF16),
        scratch_shapes=[pltpu.VMEM((SWA_HEADS, WINDOW, 2 * WINDOW), F32)],
        compiler_params=_params(2),
        name="swa_prompt",
    )(rel_bias, sinks, q, k, k, v, v)


def _swa_sample_kernel(rb_ref, sink_ref, q_ref, kn_ref, vn_ref, ck_ref, cv_ref,
                       o_ref, ok_ref, ov_ref, bias_ref, sinkcol_ref, *, tb, s, n_past):
    nk = 2 * WINDOW
    rows = SWA_HEADS * s

    @pl.when(pl.program_id(0) == 0)
    def _():
        d = (n_past + lax.broadcasted_iota(jnp.int32, (s, nk), 0)
             - lax.broadcasted_iota(jnp.int32, (s, nk), 1))
        valid = (d >= 0) & (d < WINDOW)
        for g in range(SWA_GROUP):
            for c in range(SWA_KV_HEADS):
                head = c * SWA_GROUP + g
                r0 = (g * SWA_KV_HEADS + c) * s
                bias_ref[r0:r0 + s, :] = _bias_table(rb_ref, head, d, valid)
                sinkcol_ref[r0:r0 + s, :] = jnp.full((s, LANES), sink_ref[0, head], F32)

    lane_blk = _lane_block((s, KV_COLS))
    zero_rows = jnp.zeros((nk - n_past - s, KV_COLS), F32)

    def body(b, carry):
        r0 = pl.multiple_of(b * s, s)
        k_new = kn_ref[pl.ds(r0, s), :]
        v_new = vn_ref[pl.ds(r0, s), :]
        ok_ref[b, 0:n_past - s, :] = ck_ref[b, s:n_past, :]
        ok_ref[b, n_past - s:n_past, :] = k_new
        ov_ref[b, 0:n_past - s, :] = cv_ref[b, s:n_past, :]
        ov_ref[b, n_past - s:n_past, :] = v_new
        kblk = jnp.concatenate([ck_ref[b], k_new, zero_rows], axis=0).astype(BF16)
        vblk = jnp.concatenate([cv_ref[b], v_new, zero_rows], axis=0).astype(BF16)
        pieces = []
        for g in range(SWA_GROUP):
            slab = q_ref[pl.ds(r0, s), g * KV_COLS:(g + 1) * KV_COLS]
            pieces += [jnp.where(lane_blk == c, slab, 0.0) for c in range(SWA_KV_HEADS)]
        q_stack = jnp.concatenate(pieces, axis=0).astype(BF16)
        logits = _dot_nt(q_stack, kblk) + bias_ref[...]
        probs = _softmax_sink(logits, sinkcol_ref[:, :1]).astype(BF16)
        wide = _dot(probs, vblk)
        for g in range(SWA_GROUP):
            out = jnp.zeros((s, KV_COLS), F32)
            for c in range(SWA_KV_HEADS):
                w0 = (g * SWA_KV_HEADS + c) * s
                out = jnp.where(lane_blk == c, wide[w0:w0 + s], out)
            o_ref[pl.ds(r0, s), g * KV_COLS:(g + 1) * KV_COLS] = out
        return carry

    lax.fori_loop(0, tb, body, 0)
    del rows


def _swa_sample(rel_bias, sinks, q, k_new, v_new, cache_k, cache_v, *, batch, s, tb):
    n_past = cache_k.shape[1]
    rows = lambda cols: pl.BlockSpec((tb * s, cols), lambda i: (i, 0))
    cache = pl.BlockSpec((tb, n_past, KV_COLS), lambda i: (i, 0, 0))
    smem = pl.BlockSpec(memory_space=pltpu.SMEM)
    return pl.pallas_call(
        functools.partial(_swa_sample_kernel, tb=tb, s=s, n_past=n_past),
        grid=(batch // tb,),
        in_specs=[smem, smem, rows(D_MODEL), rows(KV_COLS), rows(KV_COLS), cache, cache],
        out_specs=[rows(D_MODEL), cache, cache],
        out_shape=[jax.ShapeDtypeStruct((batch * s, D_MODEL), F32),
                   jax.ShapeDtypeStruct(cache_k.shape, F32), jax.ShapeDtypeStruct(cache_v.shape, F32)],
        scratch_shapes=[pltpu.VMEM((SWA_HEADS * s, 2 * WINDOW), F32),
                        pltpu.VMEM((SWA_HEADS * s, LANES), F32)],
        compiler_params=_params(1),
        name="swa_sample",
    )(rel_bias, sinks, q, k_new, v_new, cache_k, cache_v)


def _prep_weights(norm_mix, norm_mlp, norm_ple, norm_kv, norm_final, w_in_a, w_a2, b_a2, gla_o_gain,
                  w_out_a, w_kv, w_q_b, w_o_b, w_up, w_down, w_ple, w_ple_gate):
    w_in = w_in_a[0]
    wq, wk, wv, wg, wa = jnp.split(w_in, [NQ, 2 * NQ, 2 * NQ + NV, 2 * NQ + 2 * NV], axis=-1)
    wa = jnp.pad(wa, ((0, 0), (0, RANK_PAD - GLA_RANK)))
    wa2 = jnp.pad(w_a2[0], ((0, RANK_PAD - GLA_RANK), (0, 0)))
    wq_b = w_q_b[0].reshape(D_MODEL, SWA_KV_HEADS, SWA_GROUP, SWA_HEAD_DIM).transpose(0, 2, 1, 3)
    wo_b = w_o_b[0].reshape(SWA_KV_HEADS, SWA_GROUP, SWA_HEAD_DIM, D_MODEL).transpose(1, 0, 2, 3)
    bf = lambda w: w.astype(BF16)
    row = lambda g: g.reshape(1, -1)
    return dict(
        gla=(row(norm_mix[0]), bf(wq), bf(wk), bf(wv), bf(wg), bf(wa), bf(wa2), row(b_a2[0])),
        o_gain=row(gla_o_gain[0]),
        tail=[(bf(w_out_a[0]), row(norm_mlp[0]), bf(w_up[0]), bf(w_down[0]), row(norm_ple[0]),
               bf(w_ple_gate[0]), bf(w_ple[0])),
              (bf(wo_b.reshape(D_MODEL, D_MODEL)), row(norm_mlp[1]), bf(w_up[1]), bf(w_down[1]),
               row(norm_ple[1]), bf(w_ple_gate[1]), bf(w_ple[1]))],
        shared=(row(norm_kv), bf(w_kv[:, :KV_COLS]), bf(w_kv[:, KV_COLS:]), row(norm_mix[1]),
                bf(wq_b.reshape(D_MODEL, D_MODEL))),
        final=row(norm_final),
    )


def kernel(x_prompt, x_sample, state_gla, cache_win_k, cache_win_v, p_prompt, p_sample, norm_mix, norm_mlp, norm_ple, norm_kv, norm_final, w_in_a, w_a2, b_a2, gla_o_gain, w_out_a, w_kv, w_q_b, w_o_b, sinks, rel_bias, w_up, w_down, w_ple, w_ple_gate):
    w = _prep_weights(norm_mix, norm_mlp, norm_ple, norm_kv, norm_final, w_in_a, w_a2, b_a2, gla_o_gain,
                      w_out_a, w_kv, w_q_b, w_o_b, w_up, w_down, w_ple, w_ple_gate)
    sink_row = sinks[0].reshape(1, SWA_HEADS)

    bp, sp, _ = x_prompt.shape
    tp = bp * sp
    xp = x_prompt.reshape(tp, D_MODEL)
    pp = p_prompt.reshape(2, tp, PLE_DIM)
    q, k, v, g, lg = _gla_proj(xp, *w["gla"], tm=512, act_dtype=BF16)
    mix, state_prompt = _gla_prompt(q, k, v, g, lg, w["o_gain"], batch=bp, seq=sp, chunk=256)
    h, k_sh, v_sh, q_b = _tail0(xp, mix, pp[0], *w["tail"][0], *w["shared"], tm=512, q_dtype=BF16)
    attn = _swa_prompt(rel_bias, sink_row, q_b, k_sh, v_sh, batch=bp, seq=sp)
    y_prompt = _tail1(h, attn, pp[1], *w["tail"][1], w["final"], tm=512).reshape(bp, sp, D_MODEL)
    keep = min(WINDOW, sp)
    k4 = k_sh.reshape(bp, sp, SWA_KV_HEADS, SWA_HEAD_DIM)
    v4 = v_sh.reshape(bp, sp, SWA_KV_HEADS, SWA_HEAD_DIM)
    cache_k_prompt, cache_v_prompt = k4[:, sp - keep:], v4[:, sp - keep:]

    bs, ss, _ = x_sample.shape
    ts = bs * ss
    xs = x_sample.reshape(ts, D_MODEL)
    ps = p_sample.reshape(2, ts, PLE_DIM)
    n_past = cache_win_k.shape[1]
    q, k, v, g, lg = _gla_proj(xs, *w["gla"], tm=512, act_dtype=F32)
    mix, state_sample = _gla_sample(q, k, v, g, lg, w["o_gain"], state_gla[0], batch=bs, s=ss, tb=4)
    h, k_sh, v_sh, q_b = _tail0(xs, mix, ps[0], *w["tail"][0], *w["shared"], tm=512, q_dtype=F32)
    attn, ck, cv = _swa_sample(rel_bias, sink_row, q_b, k_sh, v_sh,
                               cache_win_k.reshape(bs, n_past, KV_COLS),
                               cache_win_v.reshape(bs, n_past, KV_COLS), batch=bs, s=ss, tb=8)
    y_sample = _tail1(h, attn, ps[1], *w["tail"][1], w["final"], tm=512).reshape(bs, ss, D_MODEL)
    cache_shape = (bs, n_past, SWA_KV_HEADS, SWA_HEAD_DIM)

    return (y_prompt, y_sample, state_prompt[None], state_sample[None],
            cache_k_prompt, cache_v_prompt, ck.reshape(cache_shape), cv.reshape(cache_shape))
```

```python
import functools
import math

import numpy as np
import jax
import jax.numpy as jnp
from jax import lax
from jax.experimental import pallas as pl
from jax.experimental.pallas import tpu as pltpu

F32 = jnp.float32
BF16 = jnp.bfloat16

D_MODEL = 1024
D_FF = 4 * D_MODEL
PLE_DIM = 256
GLA_HEADS = 4
GLA_DK = 128
GLA_DV = 256
GLA_RANK = 16
GLA_TAU = 16.0
SWA_HEADS = 16
SWA_KV_HEADS = 4
SWA_GROUP = 4
SWA_HEAD_DIM = 64
WINDOW = 128
REL_BUCKETS = 32
REL_MAX_DIST = 128
EPS = 1e-6

NQ = GLA_HEADS * GLA_DK
NV = GLA_HEADS * GLA_DV
KV_COLS = SWA_KV_HEADS * SWA_HEAD_DIM
LANES = 128
RANK_PAD = LANES
NEG = -1e30
EXP_CLAMP = 80.0
LOG2E = math.log2(math.e)
VMEM_LIMIT = 56 * 1024 * 1024

TOKEN_TILE = 512
GLA_PROMPT_CHUNK = 256
GLA_SAMPLE_BATCH = 4
SWA_PROMPT_BLOCKS = 4
SWA_SAMPLE_BATCH = 8


def _rel_bucket_starts():
    max_exact = REL_BUCKETS // 2
    starts = {}
    for d in range(WINDOW):
        if d < max_exact:
            b = d
        else:
            b = max_exact + int(np.float32(
                np.log(np.float32(d) / np.float32(max_exact)) / np.float32(math.log(REL_MAX_DIST / max_exact))
                * np.float32(REL_BUCKETS - max_exact)))
            b = min(b, REL_BUCKETS - 1)
        starts.setdefault(b, d)
    return sorted((d, b) for b, d in starts.items())


BUCKET_STARTS = _rel_bucket_starts()


def _dot(a, b):
    return jnp.dot(a, b, preferred_element_type=F32)


def _dot_nt(a, b):
    return lax.dot_general(a, b, (((1,), (1,)), ((), ())), preferred_element_type=F32)


def _dot_tn(a, b):
    return lax.dot_general(a, b, (((0,), (0,)), ((), ())), preferred_element_type=F32)


def _rms(x, gain):
    return x * lax.rsqrt(jnp.mean(x * x, axis=-1, keepdims=True) + EPS) * gain


def _sigmoid(x):
    return 1.0 / (1.0 + jnp.exp(-x))


def _const_spec(shape):
    return pl.BlockSpec(shape, lambda *_: (0,) * len(shape))


def _params(n_axes):
    return pltpu.CompilerParams(dimension_semantics=("arbitrary",) * n_axes,
                                vmem_limit_bytes=VMEM_LIMIT)


def _gla_proj_kernel(x_ref, gain_ref, wq_ref, wk_ref, wv_ref, wg_ref, wa_ref, wa2_ref, ba2_ref,
                     q_ref, k_ref, v_ref, g_ref, lg_ref):
    hn = _rms(x_ref[...], gain_ref[...]).astype(BF16)
    q_ref[...] = (_dot(hn, wq_ref[...]) * GLA_DK ** -0.5).astype(q_ref.dtype)
    k_ref[...] = _dot(hn, wk_ref[...]).astype(k_ref.dtype)
    v_ref[...] = _dot(hn, wv_ref[...]).astype(v_ref.dtype)
    g_ref[...] = _dot(hn, wg_ref[...]).astype(g_ref.dtype)
    a = _dot(hn, wa_ref[...]).astype(BF16)
    z = _dot(a, wa2_ref[...]) + ba2_ref[...]
    log_sig = jnp.minimum(z, 0.0) - jnp.log1p(jnp.exp(-jnp.abs(z)))
    lg_ref[...] = log_sig * (1.0 / GLA_TAU)


def _gla_proj(x, gain, wq, wk, wv, wg, wa, wa2, ba2, *, tm, act_dtype):
    t = x.shape[0]
    row = lambda cols: pl.BlockSpec((tm, cols), lambda i: (i, 0))
    return pl.pallas_call(
        _gla_proj_kernel,
        grid=(t // tm,),
        in_specs=[row(D_MODEL), _const_spec((1, D_MODEL)),
                  _const_spec((D_MODEL, NQ)), _const_spec((D_MODEL, NQ)),
                  _const_spec((D_MODEL, NV)), _const_spec((D_MODEL, NV)),
                  _const_spec((D_MODEL, RANK_PAD)), _const_spec((RANK_PAD, NQ)), _const_spec((1, NQ))],
        out_specs=[row(NQ), row(NQ), row(NV), row(NV), row(NQ)],
        out_shape=[jax.ShapeDtypeStruct((t, NQ), act_dtype), jax.ShapeDtypeStruct((t, NQ), act_dtype),
                   jax.ShapeDtypeStruct((t, NV), act_dtype), jax.ShapeDtypeStruct((t, NV), act_dtype),
                   jax.ShapeDtypeStruct((t, NQ), F32)],
        compiler_params=_params(1),
        name="gla_proj",
    )(x, gain, wq, wk, wv, wg, wa, wa2, ba2)


def _cumsum_rows(x):
    c = x.shape[0]
    row = lax.broadcasted_iota(jnp.int32, x.shape, 0)
    shift = 1
    while shift < c:
        x = x + jnp.where(row >= shift, pltpu.roll(x, shift, axis=0), 0.0)
        shift *= 2
    return x


def _row_to_col(row):
    return jnp.transpose(jnp.broadcast_to(row, (LANES, LANES)))[:, :1]


def _gla_chunk(q, k, v, lg, state):
    c = q.shape[0]
    bc = _cumsum_rows(lg)
    last = bc[c - 1:c, :]
    mid = bc[c // 2 - 1:c // 2, :]
    q_in = (q * jnp.exp(jnp.clip(bc - mid, -EXP_CLAMP, EXP_CLAMP))).astype(BF16)
    k_in = (k * jnp.exp(jnp.clip(mid - bc, -EXP_CLAMP, EXP_CLAMP))).astype(BF16)
    attn = _dot_nt(q_in, k_in)
    causal = lax.broadcasted_iota(jnp.int32, (c, c), 0) >= lax.broadcasted_iota(jnp.int32, (c, c), 1)
    attn = jnp.where(causal, attn, 0.0).astype(BF16)
    q_st = (q * jnp.exp(bc)).astype(BF16)
    o = _dot(attn, v) + _dot(q_st, state.astype(BF16))
    k_st = (k * jnp.exp(last - bc)).astype(BF16)
    new_state = _row_to_col(jnp.exp(last)) * state + _dot_tn(k_st, v)
    return o, new_state


def _gla_gate(o, o_gain, g):
    return _rms(o, o_gain) * (g * _sigmoid(g))


def _gla_prompt_kernel(q_ref, k_ref, v_ref, g_ref, lg_ref, ogain_ref, mix_ref, sfin_ref, state_ref):
    ci = pl.program_id(1)

    @pl.when(ci == 0)
    def _():
        state_ref[...] = jnp.zeros_like(state_ref)

    for h in range(GLA_HEADS):
        ks = slice(h * GLA_DK, (h + 1) * GLA_DK)
        vs = slice(h * GLA_DV, (h + 1) * GLA_DV)
        o, new_state = _gla_chunk(q_ref[:, ks].astype(F32), k_ref[:, ks].astype(F32), v_ref[:, vs],
                                  lg_ref[:, ks], state_ref[h])
        state_ref[h] = new_state
        mix_ref[:, vs] = _gla_gate(o, ogain_ref[...], g_ref[:, vs].astype(F32)).astype(mix_ref.dtype)

    @pl.when(ci == pl.num_programs(1) - 1)
    def _():
        sfin_ref[0] = state_ref[...]


def _gla_prompt(q, k, v, g, lg, o_gain, *, batch, seq, chunk):
    nc = seq // chunk
    blk = lambda cols: pl.BlockSpec((chunk, cols), lambda b, c: (b * nc + c, 0))
    return pl.pallas_call(
        _gla_prompt_kernel,
        grid=(batch, nc),
        in_specs=[blk(NQ), blk(NQ), blk(NV), blk(NV), blk(NQ), _const_spec((1, GLA_DV))],
        out_specs=[blk(NV),
                   pl.BlockSpec((1, GLA_HEADS, GLA_DK, GLA_DV), lambda b, c: (b, 0, 0, 0))],
        out_shape=[jax.ShapeDtypeStruct((batch * seq, NV), BF16),
                   jax.ShapeDtypeStruct((batch, GLA_HEADS, GLA_DK, GLA_DV), F32)],
        scratch_shapes=[pltpu.VMEM((GLA_HEADS, GLA_DK, GLA_DV), F32)],
        compiler_params=_params(2),
        name="gla_prompt",
    )(q, k, v, g, lg, o_gain)


def _gla_sample_kernel(q_ref, k_ref, v_ref, g_ref, lg_ref, ogain_ref, s0_ref, mix_ref, sfin_ref, *, tb, s):
    def body(b, carry):
        r0 = pl.multiple_of(b * s, s)
        for h in range(GLA_HEADS):
            ks = slice(h * GLA_DK, (h + 1) * GLA_DK)
            vs = slice(h * GLA_DV, (h + 1) * GLA_DV)
            o, new_state = _gla_chunk(q_ref[pl.ds(r0, s), ks], k_ref[pl.ds(r0, s), ks],
                                      v_ref[pl.ds(r0, s), vs].astype(BF16), lg_ref[pl.ds(r0, s), ks],
                                      s0_ref[b, h])
            sfin_ref[b, h] = new_state
            mix_ref[pl.ds(r0, s), vs] = _gla_gate(o, ogain_ref[...], g_ref[pl.ds(r0, s), vs])
        return carry

    lax.fori_loop(0, tb, body, 0)


def _gla_sample(q, k, v, g, lg, o_gain, s0, *, batch, s, tb):
    rows = lambda cols: pl.BlockSpec((tb * s, cols), lambda i: (i, 0))
    st = pl.BlockSpec((tb, GLA_HEADS, GLA_DK, GLA_DV), lambda i: (i, 0, 0, 0))
    return pl.pallas_call(
        functools.partial(_gla_sample_kernel, tb=tb, s=s),
        grid=(batch // tb,),
        in_specs=[rows(NQ), rows(NQ), rows(NV), rows(NV), rows(NQ), _const_spec((1, GLA_DV)), st],
        out_specs=[rows(NV), st],
        out_shape=[jax.ShapeDtypeStruct((batch * s, NV), F32),
                   jax.ShapeDtypeStruct((batch, GLA_HEADS, GLA_DK, GLA_DV), F32)],
        compiler_params=_params(1),
        name="gla_sample",
    )(q, k, v, g, lg, o_gain, s0)


def _tail_body(h_ref, mix_ref, p_ref, wproj_ref, gmlp_ref, wup_ref, wdown_ref, gple_ref, wgate_ref,
               wple_ref):
    h = h_ref[...] + _dot(mix_ref[...].astype(BF16), wproj_ref[...])
    hn = _rms(h, gmlp_ref[...]).astype(BF16)
    ff_chunk = D_MODEL
    for j in range(D_FF // ff_chunk):
        cols = slice(j * ff_chunk, (j + 1) * ff_chunk)
        u = jnp.square(jnp.maximum(_dot(hn, wup_ref[:, cols]), 0.0)).astype(BF16)
        h = h + _dot(u, wdown_ref[cols, :])
    gate = _sigmoid(_dot(_rms(h, gple_ref[...]).astype(BF16), wgate_ref[...]))
    return h + gate * _dot(p_ref[...].astype(BF16), wple_ref[...])


def _tail0_kernel(h_ref, mix_ref, p_ref, wproj_ref, gmlp_ref, wup_ref, wdown_ref, gple_ref, wgate_ref,
                  wple_ref, gkv_ref, wk_ref, wv_ref, gq_ref, wq_ref,
                  hout_ref, kout_ref, vout_ref, qout_ref):
    h = _tail_body(h_ref, mix_ref, p_ref, wproj_ref, gmlp_ref, wup_ref, wdown_ref, gple_ref,
                   wgate_ref, wple_ref)
    hout_ref[...] = h
    hkv = _rms(h, gkv_ref[...]).astype(BF16)
    kout_ref[...] = _dot(hkv, wk_ref[...])
    vout_ref[...] = _dot(hkv, wv_ref[...])
    hq = _rms(h, gq_ref[...]).astype(BF16)
    qout_ref[...] = (_dot(hq, wq_ref[...]) * (SWA_HEAD_DIM ** -0.5 * LOG2E)).astype(qout_ref.dtype)


def _tail1_kernel(h_ref, mix_ref, p_ref, wproj_ref, gmlp_ref, wup_ref, wdown_ref, gple_ref, wgate_ref,
                  wple_ref, gfin_ref, y_ref):
    h = _tail_body(h_ref, mix_ref, p_ref, wproj_ref, gmlp_ref, wup_ref, wdown_ref, gple_ref,
                   wgate_ref, wple_ref)
    y_ref[...] = _rms(h, gfin_ref[...])


def _resident(shape):
    return pl.BlockSpec(shape, lambda *_: (0,) * len(shape), pipeline_mode=pl.Buffered(1))


def _layer_resident(shape, layer):
    return pl.BlockSpec((None,) + shape, lambda *_: (layer,) + (0,) * len(shape),
                        pipeline_mode=pl.Buffered(1))


def _tail_common_specs(tm, layer):
    row = lambda cols: pl.BlockSpec((tm, cols), lambda i: (i, 0))
    return [row(D_MODEL), row(D_MODEL), pl.BlockSpec((None, tm, PLE_DIM), lambda i: (layer, i, 0)),
            _resident((D_MODEL, D_MODEL)), _layer_resident((1, D_MODEL), layer),
            _layer_resident((D_MODEL, D_FF), layer), _layer_resident((D_FF, D_MODEL), layer),
            _layer_resident((1, D_MODEL), layer), _layer_resident((D_MODEL, D_MODEL), layer),
            _layer_resident((PLE_DIM, D_MODEL), layer)]


def _tail0(h, mix, p, wproj, gmlp, wup, wdown, gple, wgate, wple, gkv, wk, wv, gq, wq, *, tm, q_dtype):
    t = h.shape[0]
    row = lambda cols: pl.BlockSpec((tm, cols), lambda i: (i, 0))
    return pl.pallas_call(
        _tail0_kernel,
        grid=(t // tm,),
        in_specs=_tail_common_specs(tm, 0) + [
            _resident((1, D_MODEL)), _resident((D_MODEL, KV_COLS)), _resident((D_MODEL, KV_COLS)),
            _resident((1, D_MODEL)), _resident((D_MODEL, D_MODEL))],
        out_specs=[row(D_MODEL), row(KV_COLS), row(KV_COLS), row(D_MODEL)],
        out_shape=[jax.ShapeDtypeStruct((t, D_MODEL), F32), jax.ShapeDtypeStruct((t, KV_COLS), F32),
                   jax.ShapeDtypeStruct((t, KV_COLS), F32), jax.ShapeDtypeStruct((t, D_MODEL), q_dtype)],
        compiler_params=_params(1),
        name="tail0",
    )(h, mix, p, wproj, gmlp, wup, wdown, gple, wgate, wple, gkv, wk, wv, gq, wq)


def _tail1(h, mix, p, wproj, gmlp, wup, wdown, gple, wgate, wple, gfin, *, tm):
    t = h.shape[0]
    return pl.pallas_call(
        _tail1_kernel,
        grid=(t // tm,),
        in_specs=_tail_common_specs(tm, 1) + [_resident((1, D_MODEL))],
        out_specs=pl.BlockSpec((tm, D_MODEL), lambda i: (i, 0)),
        out_shape=jax.ShapeDtypeStruct((t, D_MODEL), F32),
        compiler_params=_params(1),
        name="tail1",
    )(h, mix, p, wproj, gmlp, wup, wdown, gple, wgate, wple, gfin)


def _bias_table(rb_ref, head, d, valid):
    val = jnp.full(d.shape, rb_ref[BUCKET_STARTS[0][1], head], F32)
    for start, bucket in BUCKET_STARTS[1:]:
        val = jnp.where(d >= start, rb_ref[bucket, head], val)
    return jnp.where(valid, val * LOG2E, NEG)


def _softmax_sink(logits, sink):
    m = jnp.maximum(jnp.max(logits, axis=-1, keepdims=True), sink)
    e = jnp.exp2(logits - m)
    return e / (jnp.sum(e, axis=-1, keepdims=True) + jnp.exp2(sink - m))


def _lane_block(shape):
    return lax.broadcasted_iota(jnp.int32, shape, 1) // SWA_HEAD_DIM


def _swa_prompt_kernel(rb_ref, sink_ref, q_ref, kp_ref, kc_ref, vp_ref, vc_ref, o_ref,
                       bias_ref, kbuf_ref, vbuf_ref, *, qb):
    n = pl.program_id(1)
    w = WINDOW

    @pl.when((pl.program_id(0) == 0) & (n == 0))
    def _():
        i = lax.broadcasted_iota(jnp.int32, (w, w), 0)
        j = lax.broadcasted_iota(jnp.int32, (w, w), 1)
        d = jnp.where(j <= i, i - j, w + i - j)
        for head in range(SWA_HEADS):
            bias_ref[head] = _bias_table(rb_ref, head, d, d >= 0)
            bias_ref[SWA_HEADS + head] = _bias_table(rb_ref, head, d, j <= i)

    kbuf_ref[0:w] = kp_ref[...].astype(BF16)
    kbuf_ref[w:] = kc_ref[...].astype(BF16)
    vbuf_ref[0:w] = vp_ref[...].astype(BF16)
    vbuf_ref[w:] = vc_ref[...].astype(BF16)

    rows = SWA_KV_HEADS * w
    from_prev = (lax.broadcasted_iota(jnp.int32, (rows, w), 1)
                 > (lax.broadcasted_iota(jnp.int32, (rows, w), 0) & (w - 1)))
    lane_blk = _lane_block((w, KV_COLS))

    def block(jb, carry):
        r0 = pl.multiple_of(jb * w, w)
        kblk = kbuf_ref[pl.ds(r0, 2 * w), :]
        vblk = vbuf_ref[pl.ds(r0, 2 * w), :]
        table = jnp.where((n == 0) & (jb == 0), SWA_HEADS, 0)
        for g in range(SWA_GROUP):
            slab = q_ref[pl.ds(r0, w), g * KV_COLS:(g + 1) * KV_COLS].astype(F32)
            q_stack = jnp.concatenate(
                [jnp.where(lane_blk == c, slab, 0.0) for c in range(SWA_KV_HEADS)], axis=0).astype(BF16)
            logits = _dot_nt(q_stack, kblk)
            folded = jnp.where(from_prev, logits[:, :w], logits[:, w:])
            es, scales = [], []
            for c in range(SWA_KV_HEADS):
                head = c * SWA_GROUP + g
                sink = sink_ref[0, head] * LOG2E
                lg = folded[c * w:(c + 1) * w] + bias_ref[table + head]
                m = jnp.maximum(jnp.max(lg, axis=-1, keepdims=True), sink)
                e = jnp.exp2(lg - m)
                scales.append(1.0 / (jnp.sum(e, axis=-1, keepdims=True) + jnp.exp2(sink - m)))
                es.append(e)
            e_all = jnp.concatenate(es, axis=0)
            probs = jnp.concatenate([jnp.where(from_prev, e_all, 0.0).astype(BF16),
                                     jnp.where(from_prev, 0.0, e_all).astype(BF16)], axis=1)
            wide = _dot(probs, vblk)
            out = jnp.zeros((w, KV_COLS), F32)
            for c in range(SWA_KV_HEADS):
                out = jnp.where(lane_blk == c, wide[c * w:(c + 1) * w] * scales[c], out)
            o_ref[pl.ds(r0, w), g * KV_COLS:(g + 1) * KV_COLS] = out.astype(o_ref.dtype)
        return carry

    lax.fori_loop(0, qb, block, 0)


def _swa_prompt(rel_bias, sinks, q, k, v, *, batch, seq, qb):
    nb = seq // (qb * WINDOW)
    cur = lambda cols: pl.BlockSpec((qb * WINDOW, cols), lambda b, n: (b * nb + n, 0))
    prev = lambda cols: pl.BlockSpec(
        (WINDOW, cols), lambda b, n: (jnp.maximum((b * nb + n) * qb - 1, b * nb * qb), 0))
    smem = pl.BlockSpec(memory_space=pltpu.SMEM)
    return pl.pallas_call(
        functools.partial(_swa_prompt_kernel, qb=qb),
        grid=(batch, nb),
        in_specs=[smem, smem, cur(D_MODEL), prev(KV_COLS), cur(KV_COLS), prev(KV_COLS), cur(KV_COLS)],
        out_specs=cur(D_MODEL),
        out_shape=jax.ShapeDtypeStruct((batch * seq, D_MODEL), BF16),
        scratch_shapes=[pltpu.VMEM((2 * SWA_HEADS, WINDOW, WINDOW), F32),
                        pltpu.VMEM(((qb + 1) * WINDOW, KV_COLS), BF16),
                        pltpu.VMEM(((qb + 1) * WINDOW, KV_COLS), BF16)],
        compiler_params=_params(2),
        name="swa_prompt",
    )(rel_bias, sinks, q, k, k, v, v)


def _swa_sample_kernel(rb_ref, sink_ref, q_ref, kn_ref, vn_ref, ck_ref, cv_ref,
                       o_ref, ok_ref, ov_ref, bias_ref, sinkcol_ref, *, tb, s, n_past):
    nk = 2 * WINDOW
    rows = SWA_HEADS * s

    @pl.when(pl.program_id(0) == 0)
    def _():
        d = (n_past + lax.broadcasted_iota(jnp.int32, (s, nk), 0)
             - lax.broadcasted_iota(jnp.int32, (s, nk), 1))
        valid = (d >= 0) & (d < WINDOW)
        for g in range(SWA_GROUP):
            for c in range(SWA_KV_HEADS):
                head = c * SWA_GROUP + g
                r0 = (g * SWA_KV_HEADS + c) * s
                bias_ref[r0:r0 + s, :] = _bias_table(rb_ref, head, d, valid)
                sinkcol_ref[r0:r0 + s, :] = jnp.full((s, LANES), sink_ref[0, head] * LOG2E, F32)

    lane_blk = _lane_block((s, KV_COLS))
    zero_rows = jnp.zeros((nk - n_past - s, KV_COLS), F32)

    def body(b, carry):
        r0 = pl.multiple_of(b * s, s)
        k_new = kn_ref[pl.ds(r0, s), :]
        v_new = vn_ref[pl.ds(r0, s), :]
        ok_ref[b, 0:n_past - s, :] = ck_ref[b, s:n_past, :]
        ok_ref[b, n_past - s:n_past, :] = k_new
        ov_ref[b, 0:n_past - s, :] = cv_ref[b, s:n_past, :]
        ov_ref[b, n_past - s:n_past, :] = v_new
        kblk = jnp.concatenate([ck_ref[b], k_new, zero_rows], axis=0).astype(BF16)
        vblk = jnp.concatenate([cv_ref[b], v_new, zero_rows], axis=0).astype(BF16)
        pieces = []
        for g in range(SWA_GROUP):
            slab = q_ref[pl.ds(r0, s), g * KV_COLS:(g + 1) * KV_COLS]
            pieces += [jnp.where(lane_blk == c, slab, 0.0) for c in range(SWA_KV_HEADS)]
        q_stack = jnp.concatenate(pieces, axis=0).astype(BF16)
        logits = _dot_nt(q_stack, kblk) + bias_ref[...]
        probs = _softmax_sink(logits, sinkcol_ref[:, :1]).astype(BF16)
        wide = _dot(probs, vblk)
        for g in range(SWA_GROUP):
            out = jnp.zeros((s, KV_COLS), F32)
            for c in range(SWA_KV_HEADS):
                w0 = (g * SWA_KV_HEADS + c) * s
                out = jnp.where(lane_blk == c, wide[w0:w0 + s], out)
            o_ref[pl.ds(r0, s), g * KV_COLS:(g + 1) * KV_COLS] = out
        return carry

    lax.fori_loop(0, tb, body, 0)
    del rows


def _swa_sample(rel_bias, sinks, q, k_new, v_new, cache_k, cache_v, *, batch, s, tb):
    n_past = cache_k.shape[1]
    rows = lambda cols: pl.BlockSpec((tb * s, cols), lambda i: (i, 0))
    cache = pl.BlockSpec((tb, n_past, KV_COLS), lambda i: (i, 0, 0))
    smem = pl.BlockSpec(memory_space=pltpu.SMEM)
    return pl.pallas_call(
        functools.partial(_swa_sample_kernel, tb=tb, s=s, n_past=n_past),
        grid=(batch // tb,),
        in_specs=[smem, smem, rows(D_MODEL), rows(KV_COLS), rows(KV_COLS), cache, cache],
        out_specs=[rows(D_MODEL), cache, cache],
        out_shape=[jax.ShapeDtypeStruct((batch * s, D_MODEL), F32),
                   jax.ShapeDtypeStruct(cache_k.shape, F32), jax.ShapeDtypeStruct(cache_v.shape, F32)],
        scratch_shapes=[pltpu.VMEM((SWA_HEADS * s, 2 * WINDOW), F32),
                        pltpu.VMEM((SWA_HEADS * s, LANES), F32)],
        compiler_params=_params(1),
        name="swa_sample",
    )(rel_bias, sinks, q, k_new, v_new, cache_k, cache_v)


def _prep_weights(norm_mix, norm_mlp, norm_ple, norm_kv, norm_final, w_in_a, w_a2, b_a2, gla_o_gain,
                  w_out_a, w_kv, w_q_b, w_o_b, w_up, w_down, w_ple, w_ple_gate):
    w_in = w_in_a[0]
    wq, wk, wv, wg, wa = jnp.split(w_in, [NQ, 2 * NQ, 2 * NQ + NV, 2 * NQ + 2 * NV], axis=-1)
    wa = jnp.pad(wa, ((0, 0), (0, RANK_PAD - GLA_RANK)))
    wa2 = jnp.pad(w_a2[0], ((0, RANK_PAD - GLA_RANK), (0, 0)))
    wq_b = w_q_b[0].reshape(D_MODEL, SWA_KV_HEADS, SWA_GROUP, SWA_HEAD_DIM).transpose(0, 2, 1, 3)
    wo_b = w_o_b[0].reshape(SWA_KV_HEADS, SWA_GROUP, SWA_HEAD_DIM, D_MODEL).transpose(1, 0, 2, 3)
    bf = lambda w: w.astype(BF16)
    row = lambda g: g.reshape(1, -1)
    layers = lambda g: g.reshape(g.shape[0], 1, -1)
    stacked = (layers(norm_mlp), bf(w_up), bf(w_down), layers(norm_ple), bf(w_ple_gate), bf(w_ple))
    return dict(
        gla=(row(norm_mix[0]), bf(wq), bf(wk), bf(wv), bf(wg), bf(wa), bf(wa2), row(b_a2[0])),
        o_gain=row(gla_o_gain[0]),
        tail=[(bf(w_out_a[0]),) + stacked, (bf(wo_b.reshape(D_MODEL, D_MODEL)),) + stacked],
        shared=(row(norm_kv), bf(w_kv[:, :KV_COLS]), bf(w_kv[:, KV_COLS:]), row(norm_mix[1]),
                bf(wq_b.reshape(D_MODEL, D_MODEL))),
        final=row(norm_final),
    )


def kernel(x_prompt, x_sample, state_gla, cache_win_k, cache_win_v, p_prompt, p_sample, norm_mix, norm_mlp, norm_ple, norm_kv, norm_final, w_in_a, w_a2, b_a2, gla_o_gain, w_out_a, w_kv, w_q_b, w_o_b, sinks, rel_bias, w_up, w_down, w_ple, w_ple_gate):
    w = _prep_weights(norm_mix, norm_mlp, norm_ple, norm_kv, norm_final, w_in_a, w_a2, b_a2, gla_o_gain,
                      w_out_a, w_kv, w_q_b, w_o_b, w_up, w_down, w_ple, w_ple_gate)
    sink_row = sinks[0].reshape(1, SWA_HEADS)

    bp, sp, _ = x_prompt.shape
    tp = bp * sp
    xp = x_prompt.reshape(tp, D_MODEL)
    pp = p_prompt.reshape(2, tp, PLE_DIM)
    q, k, v, g, lg = _gla_proj(xp, *w["gla"], tm=TOKEN_TILE, act_dtype=BF16)
    mix, state_prompt = _gla_prompt(q, k, v, g, lg, w["o_gain"], batch=bp, seq=sp, chunk=GLA_PROMPT_CHUNK)
    h, k_sh, v_sh, q_b = _tail0(xp, mix, pp, *w["tail"][0], *w["shared"], tm=TOKEN_TILE, q_dtype=BF16)
    attn = _swa_prompt(rel_bias, sink_row, q_b, k_sh, v_sh, batch=bp, seq=sp, qb=SWA_PROMPT_BLOCKS)
    y_prompt = _tail1(h, attn, pp, *w["tail"][1], w["final"], tm=TOKEN_TILE).reshape(bp, sp, D_MODEL)
    keep = min(WINDOW, sp)
    cache_shape = (bp, keep, SWA_KV_HEADS, SWA_HEAD_DIM)
    cache_k_prompt = k_sh.reshape(bp, sp, KV_COLS)[:, sp - keep:].reshape(cache_shape)
    cache_v_prompt = v_sh.reshape(bp, sp, KV_COLS)[:, sp - keep:].reshape(cache_shape)

    bs, ss, _ = x_sample.shape
    ts = bs * ss
    xs = x_sample.reshape(ts, D_MODEL)
    ps = p_sample.reshape(2, ts, PLE_DIM)
    n_past = cache_win_k.shape[1]
    q, k, v, g, lg = _gla_proj(xs, *w["gla"], tm=TOKEN_TILE, act_dtype=F32)
    mix, state_sample = _gla_sample(q, k, v, g, lg, w["o_gain"], state_gla[0], batch=bs, s=ss,
                                    tb=GLA_SAMPLE_BATCH)
    h, k_sh, v_sh, q_b = _tail0(xs, mix, ps, *w["tail"][0], *w["shared"], tm=TOKEN_TILE, q_dtype=F32)
    attn, ck, cv = _swa_sample(rel_bias, sink_row, q_b, k_sh, v_sh,
                               cache_win_k.reshape(bs, n_past, KV_COLS),
                               cache_win_v.reshape(bs, n_past, KV_COLS), batch=bs, s=ss,
                               tb=SWA_SAMPLE_BATCH)
    y_sample = _tail1(h, attn, ps, *w["tail"][1], w["final"], tm=TOKEN_TILE).reshape(bs, ss, D_MODEL)
    cache_shape = (bs, n_past, SWA_KV_HEADS, SWA_HEAD_DIM)

    return (y_prompt, y_sample, state_prompt[None], state_sample[None],
            cache_k_prompt, cache_v_prompt, ck.reshape(cache_shape), cv.reshape(cache_shape))
```

```python
import functools
import math

import numpy as np
import jax
import jax.numpy as jnp
from jax import lax
from jax.experimental import pallas as pl
from jax.experimental.pallas import tpu as pltpu

F32 = jnp.float32
BF16 = jnp.bfloat16

D_MODEL = 1024
D_FF = 4 * D_MODEL
PLE_DIM = 256
GLA_HEADS = 4
GLA_DK = 128
GLA_DV = 256
GLA_RANK = 16
GLA_TAU = 16.0
SWA_HEADS = 16
SWA_KV_HEADS = 4
SWA_GROUP = 4
SWA_HEAD_DIM = 64
WINDOW = 128
REL_BUCKETS = 32
REL_MAX_DIST = 128
EPS = 1e-6

NQ = GLA_HEADS * GLA_DK
NV = GLA_HEADS * GLA_DV
KV_COLS = SWA_KV_HEADS * SWA_HEAD_DIM
LANES = 128
RANK_PAD = LANES
NEG = -1e30
EXP_CLAMP = 80.0
LOG2E = math.log2(math.e)
VMEM_LIMIT = 62 * 1024 * 1024

TOKEN_TILE = 512
GLA_PROMPT_CHUNK = 256
GLA_SAMPLE_BATCH = 4
SWA_PROMPT_BLOCKS = 4
SWA_SAMPLE_BATCH = 8


def _rel_bucket_starts():
    max_exact = REL_BUCKETS // 2
    starts = {}
    for d in range(WINDOW):
        if d < max_exact:
            b = d
        else:
            b = max_exact + int(np.float32(
                np.log(np.float32(d) / np.float32(max_exact)) / np.float32(math.log(REL_MAX_DIST / max_exact))
                * np.float32(REL_BUCKETS - max_exact)))
            b = min(b, REL_BUCKETS - 1)
        starts.setdefault(b, d)
    return sorted((d, b) for b, d in starts.items())


BUCKET_STARTS = _rel_bucket_starts()


def _dot(a, b):
    return jnp.dot(a, b, preferred_element_type=F32)


def _dot_nt(a, b):
    return lax.dot_general(a, b, (((1,), (1,)), ((), ())), preferred_element_type=F32)


def _dot_tn(a, b):
    return lax.dot_general(a, b, (((0,), (0,)), ((), ())), preferred_element_type=F32)


def _rms(x, gain):
    return x * lax.rsqrt(jnp.mean(x * x, axis=-1, keepdims=True) + EPS) * gain


def _sigmoid(x):
    return 1.0 / (1.0 + jnp.exp(-x))


def _const_spec(shape):
    return pl.BlockSpec(shape, lambda *_: (0,) * len(shape))


def _params(n_axes, flags=None):
    return pltpu.CompilerParams(dimension_semantics=("arbitrary",) * n_axes,
                                vmem_limit_bytes=VMEM_LIMIT, flags=flags)


def _gla_proj_kernel(x_ref, gain_ref, wq_ref, wk_ref, wv_ref, wg_ref, wa_ref, wa2_ref, ba2_ref,
                     q_ref, k_ref, v_ref, g_ref, lg_ref):
    hn = _rms(x_ref[...], gain_ref[...]).astype(BF16)
    q_ref[...] = (_dot(hn, wq_ref[...]) * GLA_DK ** -0.5).astype(q_ref.dtype)
    k_ref[...] = _dot(hn, wk_ref[...]).astype(k_ref.dtype)
    v_ref[...] = _dot(hn, wv_ref[...]).astype(v_ref.dtype)
    g_ref[...] = _dot(hn, wg_ref[...]).astype(g_ref.dtype)
    a = _dot(hn, wa_ref[...]).astype(BF16)
    z = _dot(a, wa2_ref[...]) + ba2_ref[...]
    log_sig = jnp.minimum(z, 0.0) - jnp.log1p(jnp.exp(-jnp.abs(z)))
    lg_ref[...] = log_sig * (1.0 / GLA_TAU)


def _gla_proj(x, gain, wq, wk, wv, wg, wa, wa2, ba2, *, tm, act_dtype):
    t = x.shape[0]
    row = lambda cols: pl.BlockSpec((tm, cols), lambda i: (i, 0))
    return pl.pallas_call(
        _gla_proj_kernel,
        grid=(t // tm,),
        in_specs=[row(D_MODEL), _const_spec((1, D_MODEL)),
                  _const_spec((D_MODEL, NQ)), _const_spec((D_MODEL, NQ)),
                  _const_spec((D_MODEL, NV)), _const_spec((D_MODEL, NV)),
                  _const_spec((D_MODEL, RANK_PAD)), _const_spec((RANK_PAD, NQ)), _const_spec((1, NQ))],
        out_specs=[row(NQ), row(NQ), row(NV), row(NV), row(NQ)],
        out_shape=[jax.ShapeDtypeStruct((t, NQ), act_dtype), jax.ShapeDtypeStruct((t, NQ), act_dtype),
                   jax.ShapeDtypeStruct((t, NV), act_dtype), jax.ShapeDtypeStruct((t, NV), act_dtype),
                   jax.ShapeDtypeStruct((t, NQ), F32)],
        compiler_params=_params(1),
        name="gla_proj",
    )(x, gain, wq, wk, wv, wg, wa, wa2, ba2)


def _cumsum_rows(x):
    c = x.shape[0]
    row = lax.broadcasted_iota(jnp.int32, x.shape, 0)
    shift = 1
    while shift < c:
        x = x + jnp.where(row >= shift, pltpu.roll(x, shift, axis=0), 0.0)
        shift *= 2
    return x


def _row_to_col(row):
    return jnp.transpose(jnp.broadcast_to(row, (LANES, LANES)))[:, :1]


def _gla_chunk(q, k, v, lg, state):
    c = q.shape[0]
    bc = _cumsum_rows(lg)
    last = bc[c - 1:c, :]
    mid = bc[c // 2 - 1:c // 2, :]
    q_in = (q * jnp.exp(jnp.clip(bc - mid, -EXP_CLAMP, EXP_CLAMP))).astype(BF16)
    k_in = (k * jnp.exp(jnp.clip(mid - bc, -EXP_CLAMP, EXP_CLAMP))).astype(BF16)
    attn = _dot_nt(q_in, k_in)
    causal = lax.broadcasted_iota(jnp.int32, (c, c), 0) >= lax.broadcasted_iota(jnp.int32, (c, c), 1)
    attn = jnp.where(causal, attn, 0.0).astype(BF16)
    q_st = (q * jnp.exp(bc)).astype(BF16)
    o = _dot(attn, v) + _dot(q_st, state.astype(BF16))
    k_st = (k * jnp.exp(last - bc)).astype(BF16)
    new_state = _row_to_col(jnp.exp(last)) * state + _dot_tn(k_st, v)
    return o, new_state


def _gla_gate(o, o_gain, g):
    return _rms(o, o_gain) * (g * _sigmoid(g))


HEAD_COLS = 2 * GLA_DK + 2 * GLA_DV


def _gla_prompt_kernel(x_ref, gain_ref, wh_ref, wa_ref, wa2_ref, ba2_ref, ogain_ref,
                       mix_ref, sfin_ref, proj_a, proj_b, lg_a, lg_b, state_ref, *, tiles_per_seq):
    i = pl.program_id(0)
    lag = jnp.maximum(i - 1, 0)

    @pl.when(i == 0)
    def _():
        proj_b[...] = jnp.zeros_like(proj_b)
        lg_b[...] = jnp.zeros_like(lg_b)

    @pl.when(lag % tiles_per_seq == 0)
    def _():
        state_ref[...] = jnp.zeros_like(state_ref)

    def body(proj_w, lg_w, proj_r, lg_r):
        hn = _rms(x_ref[...], gain_ref[...]).astype(BF16)
        z = _dot(_dot(hn, wa_ref[...]).astype(BF16), wa2_ref[...]) + ba2_ref[...]
        lg_w[...] = (jnp.minimum(z, 0.0) - jnp.log1p(jnp.exp(-jnp.abs(z)))) * (1.0 / GLA_TAU)
        for h in range(GLA_HEADS):
            c0 = h * HEAD_COLS
            c1, c2 = c0 + 2 * GLA_DK, c0 + 2 * GLA_DK + GLA_DV
            qk = _dot(hn, wh_ref[:, c0:c1])
            proj_w[:, c0:c0 + GLA_DK] = (qk[:, :GLA_DK] * GLA_DK ** -0.5).astype(BF16)
            proj_w[:, c0 + GLA_DK:c1] = qk[:, GLA_DK:].astype(BF16)
            proj_w[:, c1:c2] = _dot(hn, wh_ref[:, c1:c2]).astype(BF16)
            proj_w[:, c2:c0 + HEAD_COLS] = _dot(hn, wh_ref[:, c2:c0 + HEAD_COLS]).astype(BF16)

            q = proj_r[:, c0:c0 + GLA_DK].astype(F32)
            k = proj_r[:, c0 + GLA_DK:c1].astype(F32)
            g = proj_r[:, c2:c0 + HEAD_COLS].astype(F32)
            o, new_state = _gla_chunk(q, k, proj_r[:, c1:c2], lg_r[:, h * GLA_DK:(h + 1) * GLA_DK],
                                      state_ref[h])
            state_ref[h] = new_state
            mix_ref[:, h * GLA_DV:(h + 1) * GLA_DV] = _gla_gate(o, ogain_ref[...], g).astype(mix_ref.dtype)

    @pl.when(i % 2 == 0)
    def _():
        body(proj_a, lg_a, proj_b, lg_b)

    @pl.when(i % 2 == 1)
    def _():
        body(proj_b, lg_b, proj_a, lg_a)

    @pl.when((lag % tiles_per_seq == tiles_per_seq - 1) & (i > 0))
    def _():
        sfin_ref[0] = state_ref[...]


def _gla_prompt(x, gain, wh, wa, wa2, ba2, o_gain, *, batch, seq, chunk):
    tps = seq // chunk
    n = batch * tps
    lag_i = lambda i: jnp.maximum(i - 1, 0)
    return pl.pallas_call(
        functools.partial(_gla_prompt_kernel, tiles_per_seq=tps),
        grid=(n + 1,),
        in_specs=[pl.BlockSpec((chunk, D_MODEL), lambda i: (jnp.minimum(i, n - 1), 0)),
                  _const_spec((1, D_MODEL)), _resident((D_MODEL, GLA_HEADS * HEAD_COLS)),
                  _const_spec((D_MODEL, RANK_PAD)), _const_spec((RANK_PAD, NQ)), _const_spec((1, NQ)),
                  _const_spec((1, GLA_DV))],
        out_specs=[pl.BlockSpec((chunk, NV), lambda i: (lag_i(i), 0)),
                   pl.BlockSpec((1, GLA_HEADS, GLA_DK, GLA_DV), lambda i: (lag_i(i) // tps, 0, 0, 0))],
        out_shape=[jax.ShapeDtypeStruct((batch * seq, NV), BF16),
                   jax.ShapeDtypeStruct((batch, GLA_HEADS, GLA_DK, GLA_DV), F32)],
        scratch_shapes=[pltpu.VMEM((chunk, GLA_HEADS * HEAD_COLS), BF16),
                        pltpu.VMEM((chunk, GLA_HEADS * HEAD_COLS), BF16),
                        pltpu.VMEM((chunk, NQ), F32), pltpu.VMEM((chunk, NQ), F32),
                        pltpu.VMEM((GLA_HEADS, GLA_DK, GLA_DV), F32)],
        compiler_params=_params(1),
        name="gla_prompt",
    )(x, gain, wh, wa, wa2, ba2, o_gain)


def _gla_sample_kernel(q_ref, k_ref, v_ref, g_ref, lg_ref, ogain_ref, s0_ref, mix_ref, sfin_ref, *, tb, s):
    def body(b, carry):
        r0 = pl.multiple_of(b * s, s)
        for h in range(GLA_HEADS):
            ks = slice(h * GLA_DK, (h + 1) * GLA_DK)
            vs = slice(h * GLA_DV, (h + 1) * GLA_DV)
            o, new_state = _gla_chunk(q_ref[pl.ds(r0, s), ks], k_ref[pl.ds(r0, s), ks],
                                      v_ref[pl.ds(r0, s), vs].astype(BF16), lg_ref[pl.ds(r0, s), ks],
                                      s0_ref[b, h])
            sfin_ref[b, h] = new_state
            mix_ref[pl.ds(r0, s), vs] = _gla_gate(o, ogain_ref[...], g_ref[pl.ds(r0, s), vs])
        return carry

    lax.fori_loop(0, tb, body, 0)


def _gla_sample(q, k, v, g, lg, o_gain, s0, *, batch, s, tb):
    rows = lambda cols: pl.BlockSpec((tb * s, cols), lambda i: (i, 0))
    st = pl.BlockSpec((tb, GLA_HEADS, GLA_DK, GLA_DV), lambda i: (i, 0, 0, 0))
    return pl.pallas_call(
        functools.partial(_gla_sample_kernel, tb=tb, s=s),
        grid=(batch // tb,),
        in_specs=[rows(NQ), rows(NQ), rows(NV), rows(NV), rows(NQ), _const_spec((1, GLA_DV)), st],
        out_specs=[rows(NV), st],
        out_shape=[jax.ShapeDtypeStruct((batch * s, NV), F32),
                   jax.ShapeDtypeStruct((batch, GLA_HEADS, GLA_DK, GLA_DV), F32)],
        compiler_params=_params(1),
        name="gla_sample",
    )(q, k, v, g, lg, o_gain, s0)


def _tail_body(h_ref, mix, p_ref, wproj_ref, gmlp_ref, wup_ref, wdown_ref, gple_ref, wgate_ref,
               wple_ref, side_work=()):
    h = h_ref[...] + _dot(mix.astype(BF16), wproj_ref[...])
    hn = _rms(h, gmlp_ref[...]).astype(BF16)
    ff_chunk = D_MODEL
    for j in range(D_FF // ff_chunk):
        if j < len(side_work):
            side_work[j]()
        cols = slice(j * ff_chunk, (j + 1) * ff_chunk)
        u = jnp.square(jnp.maximum(_dot(hn, wup_ref[:, cols]), 0.0)).astype(BF16)
        h = h + _dot(u, wdown_ref[cols, :])
    gate = _sigmoid(_dot(_rms(h, gple_ref[...]).astype(BF16), wgate_ref[...]))
    return h + gate * _dot(p_ref[...].astype(BF16), wple_ref[...])


def _tail0_kernel(h_ref, mix_ref, p_ref, wproj_ref, gmlp_ref, wup_ref, wdown_ref, gple_ref, wgate_ref,
                  wple_ref, gkv_ref, wk_ref, wv_ref, gq_ref, wq_ref,
                  hout_ref, kout_ref, vout_ref, qout_ref):
    h = _tail_body(h_ref, mix_ref[...], p_ref, wproj_ref, gmlp_ref, wup_ref, wdown_ref, gple_ref,
                   wgate_ref, wple_ref)
    hout_ref[...] = h
    hkv = _rms(h, gkv_ref[...]).astype(BF16)
    kout_ref[...] = _dot(hkv, wk_ref[...])
    vout_ref[...] = _dot(hkv, wv_ref[...])
    hq = _rms(h, gq_ref[...]).astype(BF16)
    qout_ref[...] = (_dot(hq, wq_ref[...]) * (SWA_HEAD_DIM ** -0.5 * LOG2E)).astype(qout_ref.dtype)


def _tail1_kernel(h_ref, mix_ref, p_ref, wproj_ref, gmlp_ref, wup_ref, wdown_ref, gple_ref, wgate_ref,
                  wple_ref, gfin_ref, y_ref):
    h = _tail_body(h_ref, mix_ref[...], p_ref, wproj_ref, gmlp_ref, wup_ref, wdown_ref, gple_ref,
                   wgate_ref, wple_ref)
    y_ref[...] = _rms(h, gfin_ref[...])


def _resident(shape):
    return pl.BlockSpec(shape, lambda *_: (0,) * len(shape), pipeline_mode=pl.Buffered(1))


def _layer_resident(shape, layer):
    return pl.BlockSpec((None,) + shape, lambda *_: (layer,) + (0,) * len(shape),
                        pipeline_mode=pl.Buffered(1))


def _tail_common_specs(tm, layer):
    row = lambda cols: pl.BlockSpec((tm, cols), lambda i: (i, 0))
    return [row(D_MODEL), row(D_MODEL), pl.BlockSpec((None, tm, PLE_DIM), lambda i: (layer, i, 0)),
            _resident((D_MODEL, D_MODEL)), _layer_resident((1, D_MODEL), layer),
            _layer_resident((D_MODEL, D_FF), layer), _layer_resident((D_FF, D_MODEL), layer),
            _layer_resident((1, D_MODEL), layer), _layer_resident((D_MODEL, D_MODEL), layer),
            _layer_resident((PLE_DIM, D_MODEL), layer)]


def _tail0(h, mix, p, wproj, gmlp, wup, wdown, gple, wgate, wple, gkv, wk, wv, gq, wq, *, tm, q_dtype):
    t = h.shape[0]
    row = lambda cols: pl.BlockSpec((tm, cols), lambda i: (i, 0))
    return pl.pallas_call(
        _tail0_kernel,
        grid=(t // tm,),
        in_specs=_tail_common_specs(tm, 0) + [
            _resident((1, D_MODEL)), _resident((D_MODEL, KV_COLS)), _resident((D_MODEL, KV_COLS)),
            _resident((1, D_MODEL)), _resident((D_MODEL, D_MODEL))],
        out_specs=[row(D_MODEL), row(KV_COLS), row(KV_COLS), row(D_MODEL)],
        out_shape=[jax.ShapeDtypeStruct((t, D_MODEL), F32), jax.ShapeDtypeStruct((t, KV_COLS), F32),
                   jax.ShapeDtypeStruct((t, KV_COLS), F32), jax.ShapeDtypeStruct((t, D_MODEL), q_dtype)],
        compiler_params=_params(1),
        name="tail0",
    )(h, mix, p, wproj, gmlp, wup, wdown, gple, wgate, wple, gkv, wk, wv, gq, wq)


def _tail1(h, mix, p, wproj, gmlp, wup, wdown, gple, wgate, wple, gfin, *, tm):
    t = h.shape[0]
    return pl.pallas_call(
        _tail1_kernel,
        grid=(t // tm,),
        in_specs=_tail_common_specs(tm, 1) + [_resident((1, D_MODEL))],
        out_specs=pl.BlockSpec((tm, D_MODEL), lambda i: (i, 0)),
        out_shape=jax.ShapeDtypeStruct((t, D_MODEL), F32),
        compiler_params=_params(1),
        name="tail1",
    )(h, mix, p, wproj, gmlp, wup, wdown, gple, wgate, wple, gfin)


def _bias_table(rb_ref, head, d, valid):
    val = jnp.full(d.shape, rb_ref[BUCKET_STARTS[0][1], head], F32)
    for start, bucket in BUCKET_STARTS[1:]:
        val = jnp.where(d >= start, rb_ref[bucket, head], val)
    return jnp.where(valid, val * LOG2E, NEG)


def _softmax_sink(logits, sink):
    m = jnp.maximum(jnp.max(logits, axis=-1, keepdims=True), sink)
    e = jnp.exp2(logits - m)
    return e / (jnp.sum(e, axis=-1, keepdims=True) + jnp.exp2(sink - m))


def _lane_block(shape):
    return lax.broadcasted_iota(jnp.int32, shape, 1) // SWA_HEAD_DIM


def _swa_query_block(q_ref, r0, kblk, vblk, table, sink_ref, bias_ref):
    w = WINDOW
    rows = SWA_KV_HEADS * w
    from_prev = (lax.broadcasted_iota(jnp.int32, (rows, w), 1)
                 > (lax.broadcasted_iota(jnp.int32, (rows, w), 0) & (w - 1)))
    lane_blk = _lane_block((w, KV_COLS))
    slabs = []
    for g in range(SWA_GROUP):
        slab = q_ref[r0:r0 + w, g * KV_COLS:(g + 1) * KV_COLS].astype(F32)
        q_stack = jnp.concatenate(
            [jnp.where(lane_blk == c, slab, 0.0) for c in range(SWA_KV_HEADS)], axis=0).astype(BF16)
        logits = _dot_nt(q_stack, kblk)
        folded = jnp.where(from_prev, logits[:, :w], logits[:, w:])
        es, scales = [], []
        for c in range(SWA_KV_HEADS):
            head = c * SWA_GROUP + g
            sink = sink_ref[0, head] * LOG2E
            lg = folded[c * w:(c + 1) * w] + bias_ref[table + head]
            m = jnp.maximum(jnp.max(lg, axis=-1, keepdims=True), sink)
            e = jnp.exp2(lg - m)
            scales.append(1.0 / (jnp.sum(e, axis=-1, keepdims=True) + jnp.exp2(sink - m)))
            es.append(e)
        e_all = jnp.concatenate(es, axis=0)
        probs = jnp.concatenate([jnp.where(from_prev, e_all, 0.0).astype(BF16),
                                 jnp.where(from_prev, 0.0, e_all).astype(BF16)], axis=1)
        wide = _dot(probs, vblk)
        out = jnp.zeros((w, KV_COLS), F32)
        for c in range(SWA_KV_HEADS):
            out = jnp.where(lane_blk == c, wide[c * w:(c + 1) * w] * scales[c], out)
        slabs.append(out.astype(BF16))
    return jnp.concatenate(slabs, axis=1)


def _layer1_prompt_kernel(rb_ref, sink_ref, q_ref, kp_ref, kc_ref, vp_ref, vc_ref,
                          h_ref, p_ref, wproj_ref, gmlp_ref, wup_ref, wdown_ref, gple_ref, wgate_ref,
                          wple_ref, gfin_ref, y_ref, bias_ref, kbuf_ref, vbuf_ref, attn_ref,
                          *, qb, tiles_per_seq):
    i = pl.program_id(0)
    last = pl.num_programs(0) - 2
    w = WINDOW

    @pl.when(i == 0)
    def _():
        r = lax.broadcasted_iota(jnp.int32, (w, w), 0)
        c = lax.broadcasted_iota(jnp.int32, (w, w), 1)
        d = jnp.where(c <= r, r - c, w + r - c)
        for head in range(SWA_HEADS):
            bias_ref[head] = _bias_table(rb_ref, head, d, d >= 0)
            bias_ref[SWA_HEADS + head] = _bias_table(rb_ref, head, d, c <= r)
        attn_ref[1] = jnp.zeros(attn_ref.shape[1:], attn_ref.dtype)

    kbuf_ref[0:w] = kp_ref[...].astype(BF16)
    kbuf_ref[w:] = kc_ref[...].astype(BF16)
    vbuf_ref[0:w] = vp_ref[...].astype(BF16)
    vbuf_ref[w:] = vc_ref[...].astype(BF16)
    tile = jnp.minimum(i, last)
    first_table = jnp.where(tile % tiles_per_seq == 0, SWA_HEADS, 0)
    slot = i % 2

    def attend(jb):
        r0 = jb * w
        attn_ref[slot, r0:r0 + w, :] = _swa_query_block(
            q_ref, r0, kbuf_ref[r0:r0 + 2 * w, :], vbuf_ref[r0:r0 + 2 * w, :],
            first_table if jb == 0 else 0, sink_ref, bias_ref)

    h = _tail_body(h_ref, attn_ref[(i + 1) % 2], p_ref, wproj_ref, gmlp_ref, wup_ref, wdown_ref,
                   gple_ref, wgate_ref, wple_ref,
                   side_work=[functools.partial(attend, jb) for jb in range(qb)])
    y_ref[...] = _rms(h, gfin_ref[...])


def _layer1_prompt(rel_bias, sinks, q, k, v, h, p, wproj, gmlp, wup, wdown, gple, wgate, wple, gfin,
                   *, seq, qb):
    t = h.shape[0]
    tm = qb * WINDOW
    n = t // tm
    cur_i = lambda i: jnp.minimum(i, n - 1)
    lag_i = lambda i: jnp.maximum(i - 1, 0)
    cur = lambda cols: pl.BlockSpec((tm, cols), lambda i: (cur_i(i), 0))
    prev = lambda cols: pl.BlockSpec((WINDOW, cols), lambda i: (jnp.maximum(cur_i(i) * qb - 1, 0), 0))
    lag = lambda cols: pl.BlockSpec((tm, cols), lambda i: (lag_i(i), 0))
    smem = pl.BlockSpec(memory_space=pltpu.SMEM)
    tail_specs = _tail_common_specs(tm, 1)
    return pl.pallas_call(
        functools.partial(_layer1_prompt_kernel, qb=qb, tiles_per_seq=seq // tm),
        grid=(n + 1,),
        in_specs=[smem, smem, cur(D_MODEL), prev(KV_COLS), cur(KV_COLS), prev(KV_COLS), cur(KV_COLS),
                  lag(D_MODEL), pl.BlockSpec((None, tm, PLE_DIM), lambda i: (1, lag_i(i), 0))]
                 + tail_specs[3:] + [_resident((1, D_MODEL))],
        out_specs=lag(D_MODEL),
        out_shape=jax.ShapeDtypeStruct((t, D_MODEL), F32),
        scratch_shapes=[pltpu.VMEM((2 * SWA_HEADS, WINDOW, WINDOW), F32),
                        pltpu.VMEM(((qb + 1) * WINDOW, KV_COLS), BF16),
                        pltpu.VMEM(((qb + 1) * WINDOW, KV_COLS), BF16),
                        pltpu.VMEM((2, tm, D_MODEL), BF16)],
        compiler_params=_params(1),
        name="layer1_prompt",
    )(rel_bias, sinks, q, k, k, v, v, h, p, wproj, gmlp, wup, wdown, gple, wgate, wple, gfin)


def _swa_sample_kernel(rb_ref, sink_ref, q_ref, kn_ref, vn_ref, ck_ref, cv_ref,
                       o_ref, ok_ref, ov_ref, bias_ref, sinkcol_ref, *, tb, s, n_past):
    nk = 2 * WINDOW
    rows = SWA_HEADS * s

    @pl.when(pl.program_id(0) == 0)
    def _():
        d = (n_past + lax.broadcasted_iota(jnp.int32, (s, nk), 0)
             - lax.broadcasted_iota(jnp.int32, (s, nk), 1))
        valid = (d >= 0) & (d < WINDOW)
        for g in range(SWA_GROUP):
            for c in range(SWA_KV_HEADS):
                head = c * SWA_GROUP + g
                r0 = (g * SWA_KV_HEADS + c) * s
                bias_ref[r0:r0 + s, :] = _bias_table(rb_ref, head, d, valid)
                sinkcol_ref[r0:r0 + s, :] = jnp.full((s, LANES), sink_ref[0, head] * LOG2E, F32)

    lane_blk = _lane_block((s, KV_COLS))
    zero_rows = jnp.zeros((nk - n_past - s, KV_COLS), F32)

    def body(b, carry):
        r0 = pl.multiple_of(b * s, s)
        k_new = kn_ref[pl.ds(r0, s), :]
        v_new = vn_ref[pl.ds(r0, s), :]
        ok_ref[b, 0:n_past - s, :] = ck_ref[b, s:n_past, :]
        ok_ref[b, n_past - s:n_past, :] = k_new
        ov_ref[b, 0:n_past - s, :] = cv_ref[b, s:n_past, :]
        ov_ref[b, n_past - s:n_past, :] = v_new
        kblk = jnp.concatenate([ck_ref[b], k_new, zero_rows], axis=0).astype(BF16)
        vblk = jnp.concatenate([cv_ref[b], v_new, zero_rows], axis=0).astype(BF16)
        pieces = []
        for g in range(SWA_GROUP):
            slab = q_ref[pl.ds(r0, s), g * KV_COLS:(g + 1) * KV_COLS]
            pieces += [jnp.where(lane_blk == c, slab, 0.0) for c in range(SWA_KV_HEADS)]
        q_stack = jnp.concatenate(pieces, axis=0).astype(BF16)
        logits = _dot_nt(q_stack, kblk) + bias_ref[...]
        probs = _softmax_sink(logits, sinkcol_ref[:, :1]).astype(BF16)
        wide = _dot(probs, vblk)
        for g in range(SWA_GROUP):
            out = jnp.zeros((s, KV_COLS), F32)
            for c in range(SWA_KV_HEADS):
                w0 = (g * SWA_KV_HEADS + c) * s
                out = jnp.where(lane_blk == c, wide[w0:w0 + s], out)
            o_ref[pl.ds(r0, s), g * KV_COLS:(g + 1) * KV_COLS] = out
        return carry

    lax.fori_loop(0, tb, body, 0)
    del rows


def _swa_sample(rel_bias, sinks, q, k_new, v_new, cache_k, cache_v, *, batch, s, tb):
    n_past = cache_k.shape[1]
    rows = lambda cols: pl.BlockSpec((tb * s, cols), lambda i: (i, 0))
    cache = pl.BlockSpec((tb, n_past, KV_COLS), lambda i: (i, 0, 0))
    smem = pl.BlockSpec(memory_space=pltpu.SMEM)
    return pl.pallas_call(
        functools.partial(_swa_sample_kernel, tb=tb, s=s, n_past=n_past),
        grid=(batch // tb,),
        in_specs=[smem, smem, rows(D_MODEL), rows(KV_COLS), rows(KV_COLS), cache, cache],
        out_specs=[rows(D_MODEL), cache, cache],
        out_shape=[jax.ShapeDtypeStruct((batch * s, D_MODEL), F32),
                   jax.ShapeDtypeStruct(cache_k.shape, F32), jax.ShapeDtypeStruct(cache_v.shape, F32)],
        scratch_shapes=[pltpu.VMEM((SWA_HEADS * s, 2 * WINDOW), F32),
                        pltpu.VMEM((SWA_HEADS * s, LANES), F32)],
        compiler_params=_params(1),
        name="swa_sample",
    )(rel_bias, sinks, q, k_new, v_new, cache_k, cache_v)


def _prep_weights(norm_mix, norm_mlp, norm_ple, norm_kv, norm_final, w_in_a, w_a2, b_a2, gla_o_gain,
                  w_out_a, w_kv, w_q_b, w_o_b, w_up, w_down, w_ple, w_ple_gate):
    w_in = w_in_a[0]
    wq, wk, wv, wg, wa = jnp.split(w_in, [NQ, 2 * NQ, 2 * NQ + NV, 2 * NQ + 2 * NV], axis=-1)
    wa = jnp.pad(wa, ((0, 0), (0, RANK_PAD - GLA_RANK)))
    wa2 = jnp.pad(w_a2[0], ((0, RANK_PAD - GLA_RANK), (0, 0)))
    wq_b = w_q_b[0].reshape(D_MODEL, SWA_KV_HEADS, SWA_GROUP, SWA_HEAD_DIM).transpose(0, 2, 1, 3)
    wo_b = w_o_b[0].reshape(SWA_KV_HEADS, SWA_GROUP, SWA_HEAD_DIM, D_MODEL).transpose(1, 0, 2, 3)
    bf = lambda w: w.astype(BF16)
    row = lambda g: g.reshape(1, -1)
    layers = lambda g: g.reshape(g.shape[0], 1, -1)
    stacked = (layers(norm_mlp), bf(w_up), bf(w_down), layers(norm_ple), bf(w_ple_gate), bf(w_ple))
    per_head = lambda m, d: m.reshape(D_MODEL, GLA_HEADS, d)
    wh = jnp.concatenate([per_head(wq, GLA_DK), per_head(wk, GLA_DK), per_head(wv, GLA_DV),
                          per_head(wg, GLA_DV)], axis=-1).reshape(D_MODEL, GLA_HEADS * HEAD_COLS)
    return dict(
        gla=(row(norm_mix[0]), bf(wq), bf(wk), bf(wv), bf(wg), bf(wa), bf(wa2), row(b_a2[0])),
        gla_heads=(row(norm_mix[0]), bf(wh), bf(wa), bf(wa2), row(b_a2[0])),
        o_gain=row(gla_o_gain[0]),
        tail=[(bf(w_out_a[0]),) + stacked, (bf(wo_b.reshape(D_MODEL, D_MODEL)),) + stacked],
        shared=(row(norm_kv), bf(w_kv[:, :KV_COLS]), bf(w_kv[:, KV_COLS:]), row(norm_mix[1]),
                bf(wq_b.reshape(D_MODEL, D_MODEL))),
        final=row(norm_final),
    )


def kernel(x_prompt, x_sample, state_gla, cache_win_k, cache_win_v, p_prompt, p_sample, norm_mix, norm_mlp, norm_ple, norm_kv, norm_final, w_in_a, w_a2, b_a2, gla_o_gain, w_out_a, w_kv, w_q_b, w_o_b, sinks, rel_bias, w_up, w_down, w_ple, w_ple_gate):
    w = _prep_weights(norm_mix, norm_mlp, norm_ple, norm_kv, norm_final, w_in_a, w_a2, b_a2, gla_o_gain,
                      w_out_a, w_kv, w_q_b, w_o_b, w_up, w_down, w_ple, w_ple_gate)
    sink_row = sinks[0].reshape(1, SWA_HEADS)

    bp, sp, _ = x_prompt.shape
    tp = bp * sp
    xp = x_prompt.reshape(tp, D_MODEL)
    pp = p_prompt.reshape(2, tp, PLE_DIM)
    mix, state_prompt = _gla_prompt(xp, *w["gla_heads"], w["o_gain"], batch=bp, seq=sp,
                                    chunk=GLA_PROMPT_CHUNK)
    h, k_sh, v_sh, q_b = _tail0(xp, mix, pp, *w["tail"][0], *w["shared"], tm=TOKEN_TILE, q_dtype=BF16)
    y_prompt = _layer1_prompt(rel_bias, sink_row, q_b, k_sh, v_sh, h, pp, *w["tail"][1], w["final"],
                              seq=sp, qb=SWA_PROMPT_BLOCKS).reshape(bp, sp, D_MODEL)
    keep = min(WINDOW, sp)
    cache_shape = (bp, keep, SWA_KV_HEADS, SWA_HEAD_DIM)
    cache_k_prompt = k_sh.reshape(bp, sp, KV_COLS)[:, sp - keep:].reshape(cache_shape)
    cache_v_prompt = v_sh.reshape(bp, sp, KV_COLS)[:, sp - keep:].reshape(cache_shape)

    bs, ss, _ = x_sample.shape
    ts = bs * ss
    xs = x_sample.reshape(ts, D_MODEL)
    ps = p_sample.reshape(2, ts, PLE_DIM)
    n_past = cache_win_k.shape[1]
    q, k, v, g, lg = _gla_proj(xs, *w["gla"], tm=TOKEN_TILE, act_dtype=F32)
    mix, state_sample = _gla_sample(q, k, v, g, lg, w["o_gain"], state_gla[0], batch=bs, s=ss,
                                    tb=GLA_SAMPLE_BATCH)
    h, k_sh, v_sh, q_b = _tail0(xs, mix, ps, *w["tail"][0], *w["shared"], tm=TOKEN_TILE, q_dtype=F32)
    attn, ck, cv = _swa_sample(rel_bias, sink_row, q_b, k_sh, v_sh,
                               cache_win_k.reshape(bs, n_past, KV_COLS),
                               cache_win_v.reshape(bs, n_past, KV_COLS), batch=bs, s=ss,
                               tb=SWA_SAMPLE_BATCH)
    y_sample = _tail1(h, attn, ps, *w["tail"][1], w["final"], tm=TOKEN_TILE).reshape(bs, ss, D_MODEL)
    cache_shape = (bs, n_past, SWA_KV_HEADS, SWA_HEAD_DIM)

    return (y_prompt, y_sample, state_prompt[None], state_sample[None],
            cache_k_prompt, cache_v_prompt, ck.reshape(cache_shape), cv.reshape(cache_shape))
```

```python
import functools
import math

import numpy as np
import jax
import jax.numpy as jnp
from jax import lax
from jax.experimental import pallas as pl
from jax.experimental.pallas import tpu as pltpu

F32 = jnp.float32
BF16 = jnp.bfloat16

D_MODEL = 1024
D_FF = 4 * D_MODEL
PLE_DIM = 256
GLA_HEADS = 4
GLA_DK = 128
GLA_DV = 256
GLA_RANK = 16
GLA_TAU = 16.0
SWA_HEADS = 16
SWA_KV_HEADS = 4
SWA_GROUP = 4
SWA_HEAD_DIM = 64
WINDOW = 128
REL_BUCKETS = 32
REL_MAX_DIST = 128
EPS = 1e-6

NQ = GLA_HEADS * GLA_DK
NV = GLA_HEADS * GLA_DV
KV_COLS = SWA_KV_HEADS * SWA_HEAD_DIM
LANES = 128
RANK_PAD = LANES
NEG = -1e30
EXP_CLAMP = 80.0
LOG2E = math.log2(math.e)
VMEM_LIMIT = 62 * 1024 * 1024

TOKEN_TILE = 512
GLA_PROMPT_CHUNK = 256
GLA_SAMPLE_BATCH = 4
SWA_PROMPT_BLOCKS = 4
SWA_SAMPLE_BATCH = 8


def _rel_bucket_starts():
    max_exact = REL_BUCKETS // 2
    starts = {}
    for d in range(WINDOW):
        if d < max_exact:
            b = d
        else:
            b = max_exact + int(np.float32(
                np.log(np.float32(d) / np.float32(max_exact)) / np.float32(math.log(REL_MAX_DIST / max_exact))
                * np.float32(REL_BUCKETS - max_exact)))
            b = min(b, REL_BUCKETS - 1)
        starts.setdefault(b, d)
    return sorted((d, b) for b, d in starts.items())


BUCKET_STARTS = _rel_bucket_starts()


def _dot(a, b):
    return jnp.dot(a, b, preferred_element_type=F32)


def _dot_nt(a, b):
    return lax.dot_general(a, b, (((1,), (1,)), ((), ())), preferred_element_type=F32)


def _dot_tn(a, b):
    return lax.dot_general(a, b, (((0,), (0,)), ((), ())), preferred_element_type=F32)


def _rms(x, gain):
    return x * lax.rsqrt(jnp.mean(x * x, axis=-1, keepdims=True) + EPS) * gain


def _sigmoid(x):
    return 1.0 / (1.0 + jnp.exp(-x))


def _const_spec(shape):
    return pl.BlockSpec(shape, lambda *_: (0,) * len(shape))


def _params(n_axes, flags=None):
    return pltpu.CompilerParams(dimension_semantics=("arbitrary",) * n_axes,
                                vmem_limit_bytes=VMEM_LIMIT, flags=flags)


def _log_forget(hn, wa_ref, wa2_ref, ba2_ref):
    z = _dot(_dot(hn, wa_ref[...]).astype(BF16), wa2_ref[...]) + ba2_ref[...]
    return (jnp.minimum(z, 0.0) - jnp.log(1.0 + jnp.exp(-jnp.abs(z)))) * (1.0 / GLA_TAU)


def _gla_proj_kernel(x_ref, gain_ref, wq_ref, wk_ref, wv_ref, wg_ref, wa_ref, wa2_ref, ba2_ref,
                     q_ref, k_ref, v_ref, g_ref, lg_ref):
    hn = _rms(x_ref[...], gain_ref[...]).astype(BF16)
    q_ref[...] = (_dot(hn, wq_ref[...]) * GLA_DK ** -0.5).astype(q_ref.dtype)
    k_ref[...] = _dot(hn, wk_ref[...]).astype(k_ref.dtype)
    v_ref[...] = _dot(hn, wv_ref[...]).astype(v_ref.dtype)
    g_ref[...] = _dot(hn, wg_ref[...]).astype(g_ref.dtype)
    lg_ref[...] = _log_forget(hn, wa_ref, wa2_ref, ba2_ref)


def _gla_proj(x, gain, wq, wk, wv, wg, wa, wa2, ba2, *, tm, act_dtype):
    t = x.shape[0]
    row = lambda cols: pl.BlockSpec((tm, cols), lambda i: (i, 0))
    return pl.pallas_call(
        _gla_proj_kernel,
        grid=(t // tm,),
        in_specs=[row(D_MODEL), _const_spec((1, D_MODEL)),
                  _const_spec((D_MODEL, NQ)), _const_spec((D_MODEL, NQ)),
                  _const_spec((D_MODEL, NV)), _const_spec((D_MODEL, NV)),
                  _const_spec((D_MODEL, RANK_PAD)), _const_spec((RANK_PAD, NQ)), _const_spec((1, NQ))],
        out_specs=[row(NQ), row(NQ), row(NV), row(NV), row(NQ)],
        out_shape=[jax.ShapeDtypeStruct((t, NQ), act_dtype), jax.ShapeDtypeStruct((t, NQ), act_dtype),
                   jax.ShapeDtypeStruct((t, NV), act_dtype), jax.ShapeDtypeStruct((t, NV), act_dtype),
                   jax.ShapeDtypeStruct((t, NQ), F32)],
        compiler_params=_params(1),
        name="gla_proj",
    )(x, gain, wq, wk, wv, wg, wa, wa2, ba2)


def _cumsum_rows(x):
    c = x.shape[0]
    row = lax.broadcasted_iota(jnp.int32, x.shape, 0)
    shift = 1
    while shift < c:
        x = x + jnp.where(row >= shift, pltpu.roll(x, shift, axis=0), 0.0)
        shift *= 2
    return x


def _row_to_col(row):
    return jnp.transpose(jnp.broadcast_to(row, (LANES, LANES)))[:, :1]


def _round_robin(chains):
    results = [None] * len(chains)
    live = list(enumerate(chains))
    while live:
        still = []
        for idx, chain in live:
            try:
                next(chain)
                still.append((idx, chain))
            except StopIteration as done:
                results[idx] = done.value
        live = still
    return results


def _coarsen(chain, n):
    count = 0
    while True:
        try:
            next(chain)
        except StopIteration as done:
            return done.value
        count += 1
        if count % n == 0:
            yield


def _gla_chunk_stages(q, k, v, lg, state):
    c = q.shape[0]
    bc = _cumsum_rows(lg)
    last = bc[c - 1:c, :]
    mid = bc[c // 2 - 1:c // 2, :]
    yield
    q_in = (q * jnp.exp(jnp.clip(bc - mid, -EXP_CLAMP, EXP_CLAMP))).astype(BF16)
    k_in = (k * jnp.exp(jnp.clip(mid - bc, -EXP_CLAMP, EXP_CLAMP))).astype(BF16)
    yield
    attn = _dot_nt(q_in, k_in)
    yield
    causal = lax.broadcasted_iota(jnp.int32, (c, c), 0) >= lax.broadcasted_iota(jnp.int32, (c, c), 1)
    attn = jnp.where(causal, attn, 0.0).astype(BF16)
    q_st = (q * jnp.exp(bc)).astype(BF16)
    yield
    o = _dot(attn, v) + _dot(q_st, state.astype(BF16))
    yield
    k_st = (k * jnp.exp(last - bc)).astype(BF16)
    decay_col = _row_to_col(jnp.exp(last))
    yield
    new_state = decay_col * state + _dot_tn(k_st, v)
    return o, new_state


def _gla_chunk(q, k, v, lg, state):
    return _round_robin([_gla_chunk_stages(q, k, v, lg, state)])[0]


def _gla_gate(o, o_gain, g):
    return _rms(o, o_gain) * (g * _sigmoid(g))


HEAD_COLS = 2 * GLA_DK + 2 * GLA_DV


def _gla_prompt_kernel(x_ref, gain_ref, wh_ref, wa_ref, wa2_ref, ba2_ref, ogain_ref,
                       mix_ref, sfin_ref, proj_a, proj_b, lg_a, lg_b, state_ref, *, tiles_per_seq):
    i = pl.program_id(0)
    lag = jnp.maximum(i - 1, 0)

    @pl.when(i == 0)
    def _():
        proj_b[...] = jnp.zeros_like(proj_b)
        lg_b[...] = jnp.zeros_like(lg_b)

    @pl.when(lag % tiles_per_seq == 0)
    def _():
        state_ref[...] = jnp.zeros_like(state_ref)

    def head_cols(h):
        c0 = h * HEAD_COLS
        return c0, c0 + GLA_DK, c0 + 2 * GLA_DK, c0 + 2 * GLA_DK + GLA_DV, c0 + HEAD_COLS

    def project(proj_w, lg_w):
        hn = _rms(x_ref[...], gain_ref[...]).astype(BF16)
        yield
        lg_w[...] = _log_forget(hn, wa_ref, wa2_ref, ba2_ref)
        for h in range(GLA_HEADS):
            c0, ck, cv, cg, c_end = head_cols(h)
            yield
            qk = _dot(hn, wh_ref[:, c0:cv])
            proj_w[:, c0:ck] = (qk[:, :GLA_DK] * GLA_DK ** -0.5).astype(BF16)
            proj_w[:, ck:cv] = qk[:, GLA_DK:].astype(BF16)
            yield
            proj_w[:, cv:cg] = _dot(hn, wh_ref[:, cv:cg]).astype(BF16)
            yield
            proj_w[:, cg:c_end] = _dot(hn, wh_ref[:, cg:c_end]).astype(BF16)

    def recur(h, proj_r, lg_r):
        c0, ck, cv, cg, c_end = head_cols(h)
        o, new_state = yield from _gla_chunk_stages(
            proj_r[:, c0:ck].astype(F32), proj_r[:, ck:cv].astype(F32), proj_r[:, cv:cg],
            lg_r[:, h * GLA_DK:(h + 1) * GLA_DK], state_ref[h])
        state_ref[h] = new_state
        yield
        mix_ref[:, h * GLA_DV:(h + 1) * GLA_DV] = _gla_gate(
            o, ogain_ref[...], proj_r[:, cg:c_end].astype(F32)).astype(mix_ref.dtype)

    def recur_all(proj_r, lg_r):
        for h in range(GLA_HEADS):
            yield from recur(h, proj_r, lg_r)
            yield

    def body(proj_w, lg_w, proj_r, lg_r):
        _round_robin([project(proj_w, lg_w), _coarsen(recur_all(proj_r, lg_r), 2)])

    @pl.when(i % 2 == 0)
    def _():
        body(proj_a, lg_a, proj_b, lg_b)

    @pl.when(i % 2 == 1)
    def _():
        body(proj_b, lg_b, proj_a, lg_a)

    @pl.when((lag % tiles_per_seq == tiles_per_seq - 1) & (i > 0))
    def _():
        sfin_ref[0] = state_ref[...]


def _gla_prompt(x, gain, wh, wa, wa2, ba2, o_gain, *, batch, seq, chunk):
    tps = seq // chunk
    n = batch * tps
    lag_i = lambda i: jnp.maximum(i - 1, 0)
    return pl.pallas_call(
        functools.partial(_gla_prompt_kernel, tiles_per_seq=tps),
        grid=(n + 1,),
        in_specs=[pl.BlockSpec((chunk, D_MODEL), lambda i: (jnp.minimum(i, n - 1), 0)),
                  _const_spec((1, D_MODEL)), _resident((D_MODEL, GLA_HEADS * HEAD_COLS)),
                  _const_spec((D_MODEL, RANK_PAD)), _const_spec((RANK_PAD, NQ)), _const_spec((1, NQ)),
                  _const_spec((1, GLA_DV))],
        out_specs=[pl.BlockSpec((chunk, NV), lambda i: (lag_i(i), 0)),
                   pl.BlockSpec((1, GLA_HEADS, GLA_DK, GLA_DV), lambda i: (lag_i(i) // tps, 0, 0, 0))],
        out_shape=[jax.ShapeDtypeStruct((batch * seq, NV), BF16),
                   jax.ShapeDtypeStruct((batch, GLA_HEADS, GLA_DK, GLA_DV), F32)],
        scratch_shapes=[pltpu.VMEM((chunk, GLA_HEADS * HEAD_COLS), BF16),
                        pltpu.VMEM((chunk, GLA_HEADS * HEAD_COLS), BF16),
                        pltpu.VMEM((chunk, NQ), F32), pltpu.VMEM((chunk, NQ), F32),
                        pltpu.VMEM((GLA_HEADS, GLA_DK, GLA_DV), F32)],
        compiler_params=_params(1),
        name="gla_prompt",
    )(x, gain, wh, wa, wa2, ba2, o_gain)


def _gla_sample_kernel(q_ref, k_ref, v_ref, g_ref, lg_ref, ogain_ref, s0_ref, mix_ref, sfin_ref, *, tb, s):
    def chain(b, h):
        rows = slice(b * s, (b + 1) * s)
        ks = slice(h * GLA_DK, (h + 1) * GLA_DK)
        vs = slice(h * GLA_DV, (h + 1) * GLA_DV)
        o, new_state = yield from _gla_chunk_stages(
            q_ref[rows, ks], k_ref[rows, ks], v_ref[rows, vs].astype(BF16), lg_ref[rows, ks],
            s0_ref[b, h])
        sfin_ref[b, h] = new_state
        yield
        mix_ref[rows, vs] = _gla_gate(o, ogain_ref[...], g_ref[rows, vs])

    _round_robin([chain(b, h) for b in range(tb) for h in range(GLA_HEADS)])


def _gla_sample(q, k, v, g, lg, o_gain, s0, *, batch, s, tb):
    rows = lambda cols: pl.BlockSpec((tb * s, cols), lambda i: (i, 0))
    st = pl.BlockSpec((tb, GLA_HEADS, GLA_DK, GLA_DV), lambda i: (i, 0, 0, 0))
    return pl.pallas_call(
        functools.partial(_gla_sample_kernel, tb=tb, s=s),
        grid=(batch // tb,),
        in_specs=[rows(NQ), rows(NQ), rows(NV), rows(NV), rows(NQ), _const_spec((1, GLA_DV)), st],
        out_specs=[rows(NV), st],
        out_shape=[jax.ShapeDtypeStruct((batch * s, NV), F32),
                   jax.ShapeDtypeStruct((batch, GLA_HEADS, GLA_DK, GLA_DV), F32)],
        compiler_params=_params(1),
        name="gla_sample",
    )(q, k, v, g, lg, o_gain, s0)


def _tail_body(h_ref, mix, p_ref, wproj_ref, gmlp_ref, wup_ref, wdown_ref, gple_ref, wgate_ref,
               wple_ref, side_work=()):
    h = h_ref[...] + _dot(mix.astype(BF16), wproj_ref[...])
    hn = _rms(h, gmlp_ref[...]).astype(BF16)
    ff_chunk = D_MODEL
    for j in range(D_FF // ff_chunk):
        if j < len(side_work):
            side_work[j]()
        cols = slice(j * ff_chunk, (j + 1) * ff_chunk)
        u = jnp.square(jnp.maximum(_dot(hn, wup_ref[:, cols]), 0.0)).astype(BF16)
        h = h + _dot(u, wdown_ref[cols, :])
    gate = _sigmoid(_dot(_rms(h, gple_ref[...]).astype(BF16), wgate_ref[...]))
    return h + gate * _dot(p_ref[...].astype(BF16), wple_ref[...])


def _tail0_kernel(h_ref, mix_ref, p_ref, wproj_ref, gmlp_ref, wup_ref, wdown_ref, gple_ref, wgate_ref,
                  wple_ref, gkv_ref, wk_ref, wv_ref, gq_ref, wq_ref,
                  hout_ref, kout_ref, vout_ref, qout_ref):
    h = _tail_body(h_ref, mix_ref[...], p_ref, wproj_ref, gmlp_ref, wup_ref, wdown_ref, gple_ref,
                   wgate_ref, wple_ref)
    hout_ref[...] = h
    hkv = _rms(h, gkv_ref[...]).astype(BF16)
    kout_ref[...] = _dot(hkv, wk_ref[...])
    vout_ref[...] = _dot(hkv, wv_ref[...])
    hq = _rms(h, gq_ref[...]).astype(BF16)
    qout_ref[...] = (_dot(hq, wq_ref[...]) * (SWA_HEAD_DIM ** -0.5 * LOG2E)).astype(qout_ref.dtype)


def _tail1_kernel(h_ref, mix_ref, p_ref, wproj_ref, gmlp_ref, wup_ref, wdown_ref, gple_ref, wgate_ref,
                  wple_ref, gfin_ref, y_ref):
    h = _tail_body(h_ref, mix_ref[...], p_ref, wproj_ref, gmlp_ref, wup_ref, wdown_ref, gple_ref,
                   wgate_ref, wple_ref)
    y_ref[...] = _rms(h, gfin_ref[...])


def _resident(shape):
    return pl.BlockSpec(shape, lambda *_: (0,) * len(shape), pipeline_mode=pl.Buffered(1))


def _layer_resident(shape, layer):
    return pl.BlockSpec((None,) + shape, lambda *_: (layer,) + (0,) * len(shape),
                        pipeline_mode=pl.Buffered(1))


def _tail_common_specs(tm, layer):
    row = lambda cols: pl.BlockSpec((tm, cols), lambda i: (i, 0))
    return [row(D_MODEL), row(D_MODEL), pl.BlockSpec((None, tm, PLE_DIM), lambda i: (layer, i, 0)),
            _resident((D_MODEL, D_MODEL)), _layer_resident((1, D_MODEL), layer),
            _layer_resident((D_MODEL, D_FF), layer), _layer_resident((D_FF, D_MODEL), layer),
            _layer_resident((1, D_MODEL), layer), _layer_resident((D_MODEL, D_MODEL), layer),
            _layer_resident((PLE_DIM, D_MODEL), layer)]


def _tail0(h, mix, p, wproj, gmlp, wup, wdown, gple, wgate, wple, gkv, wk, wv, gq, wq, *, tm, q_dtype):
    t = h.shape[0]
    row = lambda cols: pl.BlockSpec((tm, cols), lambda i: (i, 0))
    return pl.pallas_call(
        _tail0_kernel,
        grid=(t // tm,),
        in_specs=_tail_common_specs(tm, 0) + [
            _resident((1, D_MODEL)), _resident((D_MODEL, KV_COLS)), _resident((D_MODEL, KV_COLS)),
            _resident((1, D_MODEL)), _resident((D_MODEL, D_MODEL))],
        out_specs=[row(D_MODEL), row(KV_COLS), row(KV_COLS), row(D_MODEL)],
        out_shape=[jax.ShapeDtypeStruct((t, D_MODEL), F32), jax.ShapeDtypeStruct((t, KV_COLS), F32),
                   jax.ShapeDtypeStruct((t, KV_COLS), F32), jax.ShapeDtypeStruct((t, D_MODEL), q_dtype)],
        compiler_params=_params(1),
        name="tail0",
    )(h, mix, p, wproj, gmlp, wup, wdown, gple, wgate, wple, gkv, wk, wv, gq, wq)


def _tail1(h, mix, p, wproj, gmlp, wup, wdown, gple, wgate, wple, gfin, *, tm):
    t = h.shape[0]
    return pl.pallas_call(
        _tail1_kernel,
        grid=(t // tm,),
        in_specs=_tail_common_specs(tm, 1) + [_resident((1, D_MODEL))],
        out_specs=pl.BlockSpec((tm, D_MODEL), lambda i: (i, 0)),
        out_shape=jax.ShapeDtypeStruct((t, D_MODEL), F32),
        compiler_params=_params(1),
        name="tail1",
    )(h, mix, p, wproj, gmlp, wup, wdown, gple, wgate, wple, gfin)


def _bias_table(rb_ref, head, d, valid):
    val = jnp.full(d.shape, rb_ref[BUCKET_STARTS[0][1], head], F32)
    for start, bucket in BUCKET_STARTS[1:]:
        val = jnp.where(d >= start, rb_ref[bucket, head], val)
    return jnp.where(valid, val * LOG2E, NEG)


def _softmax_sink(logits, sink):
    m = jnp.maximum(jnp.max(logits, axis=-1, keepdims=True), sink)
    e = jnp.exp2(logits - m)
    return e / (jnp.sum(e, axis=-1, keepdims=True) + jnp.exp2(sink - m))


def _lane_block(shape):
    return lax.broadcasted_iota(jnp.int32, shape, 1) // SWA_HEAD_DIM


def _swa_query_block(q_ref, r0, kblk, vblk, table, sink_ref, bias_ref):
    w = WINDOW
    rows = SWA_KV_HEADS * w
    from_prev = (lax.broadcasted_iota(jnp.int32, (rows, w), 1)
                 > (lax.broadcasted_iota(jnp.int32, (rows, w), 0) & (w - 1)))
    lane_blk = _lane_block((w, KV_COLS))
    slabs = []
    for g in range(SWA_GROUP):
        slab = q_ref[r0:r0 + w, g * KV_COLS:(g + 1) * KV_COLS].astype(F32)
        q_stack = jnp.concatenate(
            [jnp.where(lane_blk == c, slab, 0.0) for c in range(SWA_KV_HEADS)], axis=0).astype(BF16)
        logits = _dot_nt(q_stack, kblk)
        folded = jnp.where(from_prev, logits[:, :w], logits[:, w:])
        es, scales = [], []
        for c in range(SWA_KV_HEADS):
            head = c * SWA_GROUP + g
            sink = sink_ref[0, head] * LOG2E
            lg = folded[c * w:(c + 1) * w] + bias_ref[table + head]
            m = jnp.maximum(jnp.max(lg, axis=-1, keepdims=True), sink)
            e = jnp.exp2(lg - m)
            scales.append(1.0 / (jnp.sum(e, axis=-1, keepdims=True) + jnp.exp2(sink - m)))
            es.append(e)
        e_all = jnp.concatenate(es, axis=0)
        probs = jnp.concatenate([jnp.where(from_prev, e_all, 0.0).astype(BF16),
                                 jnp.where(from_prev, 0.0, e_all).astype(BF16)], axis=1)
        wide = _dot(probs, vblk)
        out = jnp.zeros((w, KV_COLS), F32)
        for c in range(SWA_KV_HEADS):
            out = jnp.where(lane_blk == c, wide[c * w:(c + 1) * w] * scales[c], out)
        slabs.append(out.astype(BF16))
    return jnp.concatenate(slabs, axis=1)


def _layer1_prompt_kernel(rb_ref, sink_ref, q_ref, kp_ref, kc_ref, vp_ref, vc_ref,
                          h_ref, p_ref, wproj_ref, gmlp_ref, wup_ref, wdown_ref, gple_ref, wgate_ref,
                          wple_ref, gfin_ref, y_ref, bias_ref, kbuf_ref, vbuf_ref, attn_ref,
                          *, qb, tiles_per_seq):
    i = pl.program_id(0)
    last = pl.num_programs(0) - 2
    w = WINDOW

    @pl.when(i == 0)
    def _():
        r = lax.broadcasted_iota(jnp.int32, (w, w), 0)
        c = lax.broadcasted_iota(jnp.int32, (w, w), 1)
        d = jnp.where(c <= r, r - c, w + r - c)
        for head in range(SWA_HEADS):
            bias_ref[head] = _bias_table(rb_ref, head, d, d >= 0)
            bias_ref[SWA_HEADS + head] = _bias_table(rb_ref, head, d, c <= r)
        attn_ref[1] = jnp.zeros(attn_ref.shape[1:], attn_ref.dtype)

    kbuf_ref[0:w] = kp_ref[...].astype(BF16)
    kbuf_ref[w:] = kc_ref[...].astype(BF16)
    vbuf_ref[0:w] = vp_ref[...].astype(BF16)
    vbuf_ref[w:] = vc_ref[...].astype(BF16)
    tile = jnp.minimum(i, last)
    first_table = jnp.where(tile % tiles_per_seq == 0, SWA_HEADS, 0)
    slot = i % 2

    def attend(jb):
        r0 = jb * w
        attn_ref[slot, r0:r0 + w, :] = _swa_query_block(
            q_ref, r0, kbuf_ref[r0:r0 + 2 * w, :], vbuf_ref[r0:r0 + 2 * w, :],
            first_table if jb == 0 else 0, sink_ref, bias_ref)

    h = _tail_body(h_ref, attn_ref[(i + 1) % 2], p_ref, wproj_ref, gmlp_ref, wup_ref, wdown_ref,
                   gple_ref, wgate_ref, wple_ref,
                   side_work=[functools.partial(attend, jb) for jb in range(qb)])
    y_ref[...] = _rms(h, gfin_ref[...])


def _layer1_prompt(rel_bias, sinks, q, k, v, h, p, wproj, gmlp, wup, wdown, gple, wgate, wple, gfin,
                   *, seq, qb):
    t = h.shape[0]
    tm = qb * WINDOW
    n = t // tm
    cur_i = lambda i: jnp.minimum(i, n - 1)
    lag_i = lambda i: jnp.maximum(i - 1, 0)
    cur = lambda cols: pl.BlockSpec((tm, cols), lambda i: (cur_i(i), 0))
    prev = lambda cols: pl.BlockSpec((WINDOW, cols), lambda i: (jnp.maximum(cur_i(i) * qb - 1, 0), 0))
    lag = lambda cols: pl.BlockSpec((tm, cols), lambda i: (lag_i(i), 0))
    smem = pl.BlockSpec(memory_space=pltpu.SMEM)
    tail_specs = _tail_common_specs(tm, 1)
    return pl.pallas_call(
        functools.partial(_layer1_prompt_kernel, qb=qb, tiles_per_seq=seq // tm),
        grid=(n + 1,),
        in_specs=[smem, smem, cur(D_MODEL), prev(KV_COLS), cur(KV_COLS), prev(KV_COLS), cur(KV_COLS),
                  lag(D_MODEL), pl.BlockSpec((None, tm, PLE_DIM), lambda i: (1, lag_i(i), 0))]
                 + tail_specs[3:] + [_resident((1, D_MODEL))],
        out_specs=lag(D_MODEL),
        out_shape=jax.ShapeDtypeStruct((t, D_MODEL), F32),
        scratch_shapes=[pltpu.VMEM((2 * SWA_HEADS, WINDOW, WINDOW), F32),
                        pltpu.VMEM(((qb + 1) * WINDOW, KV_COLS), BF16),
                        pltpu.VMEM(((qb + 1) * WINDOW, KV_COLS), BF16),
                        pltpu.VMEM((2, tm, D_MODEL), BF16)],
        compiler_params=_params(1),
        name="layer1_prompt",
    )(rel_bias, sinks, q, k, k, v, v, h, p, wproj, gmlp, wup, wdown, gple, wgate, wple, gfin)


def _swa_sample_kernel(rb_ref, sink_ref, q_ref, kn_ref, vn_ref, ck_ref, cv_ref,
                       o_ref, ok_ref, ov_ref, bias_ref, sinkcol_ref, *, tb, s, n_past):
    nk = 2 * WINDOW
    rows = SWA_HEADS * s

    @pl.when(pl.program_id(0) == 0)
    def _():
        d = (n_past + lax.broadcasted_iota(jnp.int32, (s, nk), 0)
             - lax.broadcasted_iota(jnp.int32, (s, nk), 1))
        valid = (d >= 0) & (d < WINDOW)
        for g in range(SWA_GROUP):
            for c in range(SWA_KV_HEADS):
                head = c * SWA_GROUP + g
                r0 = (g * SWA_KV_HEADS + c) * s
                bias_ref[r0:r0 + s, :] = _bias_table(rb_ref, head, d, valid)
                sinkcol_ref[r0:r0 + s, :] = jnp.full((s, LANES), sink_ref[0, head] * LOG2E, F32)

    lane_blk = _lane_block((s, KV_COLS))
    zero_rows = jnp.zeros((nk - n_past - s, KV_COLS), F32)

    def chain(b):
        r0 = b * s
        k_new = kn_ref[r0:r0 + s, :]
        v_new = vn_ref[r0:r0 + s, :]
        ok_ref[b, 0:n_past - s, :] = ck_ref[b, s:n_past, :]
        ok_ref[b, n_past - s:n_past, :] = k_new
        ov_ref[b, 0:n_past - s, :] = cv_ref[b, s:n_past, :]
        ov_ref[b, n_past - s:n_past, :] = v_new
        yield
        kblk = jnp.concatenate([ck_ref[b], k_new, zero_rows], axis=0).astype(BF16)
        vblk = jnp.concatenate([cv_ref[b], v_new, zero_rows], axis=0).astype(BF16)
        pieces = []
        for g in range(SWA_GROUP):
            slab = q_ref[r0:r0 + s, g * KV_COLS:(g + 1) * KV_COLS]
            pieces += [jnp.where(lane_blk == c, slab, 0.0) for c in range(SWA_KV_HEADS)]
        q_stack = jnp.concatenate(pieces, axis=0).astype(BF16)
        yield
        logits = _dot_nt(q_stack, kblk) + bias_ref[...]
        yield
        probs = _softmax_sink(logits, sinkcol_ref[:, :1]).astype(BF16)
        yield
        wide = _dot(probs, vblk)
        yield
        for g in range(SWA_GROUP):
            out = jnp.zeros((s, KV_COLS), F32)
            for c in range(SWA_KV_HEADS):
                w0 = (g * SWA_KV_HEADS + c) * s
                out = jnp.where(lane_blk == c, wide[w0:w0 + s], out)
            o_ref[r0:r0 + s, g * KV_COLS:(g + 1) * KV_COLS] = out

    _round_robin([chain(b) for b in range(tb)])


def _swa_sample(rel_bias, sinks, q, k_new, v_new, cache_k, cache_v, *, batch, s, tb):
    n_past = cache_k.shape[1]
    rows = lambda cols: pl.BlockSpec((tb * s, cols), lambda i: (i, 0))
    cache = pl.BlockSpec((tb, n_past, KV_COLS), lambda i: (i, 0, 0))
    smem = pl.BlockSpec(memory_space=pltpu.SMEM)
    return pl.pallas_call(
        functools.partial(_swa_sample_kernel, tb=tb, s=s, n_past=n_past),
        grid=(batch // tb,),
        in_specs=[smem, smem, rows(D_MODEL), rows(KV_COLS), rows(KV_COLS), cache, cache],
        out_specs=[rows(D_MODEL), cache, cache],
        out_shape=[jax.ShapeDtypeStruct((batch * s, D_MODEL), F32),
                   jax.ShapeDtypeStruct(cache_k.shape, F32), jax.ShapeDtypeStruct(cache_v.shape, F32)],
        scratch_shapes=[pltpu.VMEM((SWA_HEADS * s, 2 * WINDOW), F32),
                        pltpu.VMEM((SWA_HEADS * s, LANES), F32)],
        compiler_params=_params(1),
        name="swa_sample",
    )(rel_bias, sinks, q, k_new, v_new, cache_k, cache_v)


def _prep_weights(norm_mix, norm_mlp, norm_ple, norm_kv, norm_final, w_in_a, w_a2, b_a2, gla_o_gain,
                  w_out_a, w_kv, w_q_b, w_o_b, w_up, w_down, w_ple, w_ple_gate):
    w_in = w_in_a[0]
    wq, wk, wv, wg, wa = jnp.split(w_in, [NQ, 2 * NQ, 2 * NQ + NV, 2 * NQ + 2 * NV], axis=-1)
    wa = jnp.pad(wa, ((0, 0), (0, RANK_PAD - GLA_RANK)))
    wa2 = jnp.pad(w_a2[0], ((0, RANK_PAD - GLA_RANK), (0, 0)))
    wq_b = w_q_b[0].reshape(D_MODEL, SWA_KV_HEADS, SWA_GROUP, SWA_HEAD_DIM).transpose(0, 2, 1, 3)
    wo_b = w_o_b[0].reshape(SWA_KV_HEADS, SWA_GROUP, SWA_HEAD_DIM, D_MODEL).transpose(1, 0, 2, 3)
    bf = lambda w: w.astype(BF16)
    row = lambda g: g.reshape(1, -1)
    layers = lambda g: g.reshape(g.shape[0], 1, -1)
    stacked = (layers(norm_mlp), bf(w_up), bf(w_down), layers(norm_ple), bf(w_ple_gate), bf(w_ple))
    per_head = lambda m, d: m.reshape(D_MODEL, GLA_HEADS, d)
    wh = jnp.concatenate([per_head(wq, GLA_DK), per_head(wk, GLA_DK), per_head(wv, GLA_DV),
                          per_head(wg, GLA_DV)], axis=-1).reshape(D_MODEL, GLA_HEADS * HEAD_COLS)
    return dict(
        gla=(row(norm_mix[0]), bf(wq), bf(wk), bf(wv), bf(wg), bf(wa), bf(wa2), row(b_a2[0])),
        gla_heads=(row(norm_mix[0]), bf(wh), bf(wa), bf(wa2), row(b_a2[0])),
        o_gain=row(gla_o_gain[0]),
        tail=[(bf(w_out_a[0]),) + stacked, (bf(wo_b.reshape(D_MODEL, D_MODEL)),) + stacked],
        shared=(row(norm_kv), bf(w_kv[:, :KV_COLS]), bf(w_kv[:, KV_COLS:]), row(norm_mix[1]),
                bf(wq_b.reshape(D_MODEL, D_MODEL))),
        final=row(norm_final),
    )


def kernel(x_prompt, x_sample, state_gla, cache_win_k, cache_win_v, p_prompt, p_sample, norm_mix, norm_mlp, norm_ple, norm_kv, norm_final, w_in_a, w_a2, b_a2, gla_o_gain, w_out_a, w_kv, w_q_b, w_o_b, sinks, rel_bias, w_up, w_down, w_ple, w_ple_gate):
    w = _prep_weights(norm_mix, norm_mlp, norm_ple, norm_kv, norm_final, w_in_a, w_a2, b_a2, gla_o_gain,
                      w_out_a, w_kv, w_q_b, w_o_b, w_up, w_down, w_ple, w_ple_gate)
    sink_row = sinks[0].reshape(1, SWA_HEADS)

    bp, sp, _ = x_prompt.shape
    tp = bp * sp
    xp = x_prompt.reshape(tp, D_MODEL)
    pp = p_prompt.reshape(2, tp, PLE_DIM)
    mix, state_prompt = _gla_prompt(xp, *w["gla_heads"], w["o_gain"], batch=bp, seq=sp,
                                    chunk=GLA_PROMPT_CHUNK)
    h, k_sh, v_sh, q_b = _tail0(xp, mix, pp, *w["tail"][0], *w["shared"], tm=TOKEN_TILE, q_dtype=BF16)
    y_prompt = _layer1_prompt(rel_bias, sink_row, q_b, k_sh, v_sh, h, pp, *w["tail"][1], w["final"],
                              seq=sp, qb=SWA_PROMPT_BLOCKS).reshape(bp, sp, D_MODEL)
    keep = min(WINDOW, sp)
    cache_shape = (bp, keep, SWA_KV_HEADS, SWA_HEAD_DIM)
    cache_k_prompt = k_sh.reshape(bp, sp, KV_COLS)[:, sp - keep:].reshape(cache_shape)
    cache_v_prompt = v_sh.reshape(bp, sp, KV_COLS)[:, sp - keep:].reshape(cache_shape)

    bs, ss, _ = x_sample.shape
    ts = bs * ss
    xs = x_sample.reshape(ts, D_MODEL)
    ps = p_sample.reshape(2, ts, PLE_DIM)
    n_past = cache_win_k.shape[1]
    q, k, v, g, lg = _gla_proj(xs, *w["gla"], tm=TOKEN_TILE, act_dtype=F32)
    mix, state_sample = _gla_sample(q, k, v, g, lg, w["o_gain"], state_gla[0], batch=bs, s=ss,
                                    tb=GLA_SAMPLE_BATCH)
    h, k_sh, v_sh, q_b = _tail0(xs, mix, ps, *w["tail"][0], *w["shared"], tm=TOKEN_TILE, q_dtype=F32)
    attn, ck, cv = _swa_sample(rel_bias, sink_row, q_b, k_sh, v_sh,
                               cache_win_k.reshape(bs, n_past, KV_COLS),
                               cache_win_v.reshape(bs, n_past, KV_COLS), batch=bs, s=ss,
                               tb=SWA_SAMPLE_BATCH)
    y_sample = _tail1(h, attn, ps, *w["tail"][1], w["final"], tm=TOKEN_TILE).reshape(bs, ss, D_MODEL)
    cache_shape = (bs, n_past, SWA_KV_HEADS, SWA_HEAD_DIM)

    return (y_prompt, y_sample, state_prompt[None], state_sample[None],
            cache_k_prompt, cache_v_prompt, ck.reshape(cache_shape), cv.reshape(cache_shape))
```

```python
import functools
import math

import numpy as np
import jax
import jax.numpy as jnp
from jax import lax
from jax.experimental import pallas as pl
from jax.experimental.pallas import tpu as pltpu

F32 = jnp.float32
BF16 = jnp.bfloat16

D_MODEL = 1024
D_FF = 4 * D_MODEL
PLE_DIM = 256
GLA_HEADS = 4
GLA_DK = 128
GLA_DV = 256
GLA_RANK = 16
GLA_TAU = 16.0
SWA_HEADS = 16
SWA_KV_HEADS = 4
SWA_GROUP = 4
SWA_HEAD_DIM = 64
WINDOW = 128
REL_BUCKETS = 32
REL_MAX_DIST = 128
EPS = 1e-6

NQ = GLA_HEADS * GLA_DK
NV = GLA_HEADS * GLA_DV
KV_COLS = SWA_KV_HEADS * SWA_HEAD_DIM
LANES = 128
RANK_PAD = LANES
NEG = -1e30
EXP_CLAMP = 80.0
LOG2E = math.log2(math.e)
VMEM_LIMIT = 62 * 1024 * 1024

TOKEN_TILE = 512
GLA_PROMPT_CHUNK = 256
GLA_SAMPLE_BATCH = 8
SWA_PROMPT_BLOCKS = 4
SWA_SAMPLE_BATCH = 16


def _rel_bucket_starts():
    max_exact = REL_BUCKETS // 2
    starts = {}
    for d in range(WINDOW):
        if d < max_exact:
            b = d
        else:
            b = max_exact + int(np.float32(
                np.log(np.float32(d) / np.float32(max_exact)) / np.float32(math.log(REL_MAX_DIST / max_exact))
                * np.float32(REL_BUCKETS - max_exact)))
            b = min(b, REL_BUCKETS - 1)
        starts.setdefault(b, d)
    return sorted((d, b) for b, d in starts.items())


BUCKET_STARTS = _rel_bucket_starts()


def _dot(a, b):
    return jnp.dot(a, b, preferred_element_type=F32)


def _dot_nt(a, b):
    return lax.dot_general(a, b, (((1,), (1,)), ((), ())), preferred_element_type=F32)


def _dot_tn(a, b):
    return lax.dot_general(a, b, (((0,), (0,)), ((), ())), preferred_element_type=F32)


def _rms(x, gain):
    return x * lax.rsqrt(jnp.mean(x * x, axis=-1, keepdims=True) + EPS) * gain


def _sigmoid(x):
    return 1.0 / (1.0 + jnp.exp(-x))


def _const_spec(shape):
    return pl.BlockSpec(shape, lambda *_: (0,) * len(shape))


def _params(n_axes, flags=None):
    return pltpu.CompilerParams(dimension_semantics=("arbitrary",) * n_axes,
                                vmem_limit_bytes=VMEM_LIMIT, flags=flags)


def _log_forget(hn, wa_ref, wa2_ref, ba2_ref):
    z = _dot(_dot(hn, wa_ref[...]).astype(BF16), wa2_ref[...]) + ba2_ref[...]
    return (jnp.minimum(z, 0.0) - jnp.log(1.0 + jnp.exp(-jnp.abs(z)))) * (1.0 / GLA_TAU)


def _gla_proj_kernel(x_ref, gain_ref, wq_ref, wk_ref, wv_ref, wg_ref, wa_ref, wa2_ref, ba2_ref,
                     q_ref, k_ref, v_ref, g_ref, lg_ref):
    hn = _rms(x_ref[...], gain_ref[...]).astype(BF16)
    q_ref[...] = (_dot(hn, wq_ref[...]) * GLA_DK ** -0.5).astype(q_ref.dtype)
    k_ref[...] = _dot(hn, wk_ref[...]).astype(k_ref.dtype)
    v_ref[...] = _dot(hn, wv_ref[...]).astype(v_ref.dtype)
    g_ref[...] = _dot(hn, wg_ref[...]).astype(g_ref.dtype)
    lg_ref[...] = _log_forget(hn, wa_ref, wa2_ref, ba2_ref)


def _gla_proj(x, gain, wq, wk, wv, wg, wa, wa2, ba2, *, tm, act_dtype):
    t = x.shape[0]
    row = lambda cols: pl.BlockSpec((tm, cols), lambda i: (i, 0))
    return pl.pallas_call(
        _gla_proj_kernel,
        grid=(t // tm,),
        in_specs=[row(D_MODEL), _const_spec((1, D_MODEL)),
                  _const_spec((D_MODEL, NQ)), _const_spec((D_MODEL, NQ)),
                  _const_spec((D_MODEL, NV)), _const_spec((D_MODEL, NV)),
                  _const_spec((D_MODEL, RANK_PAD)), _const_spec((RANK_PAD, NQ)), _const_spec((1, NQ))],
        out_specs=[row(NQ), row(NQ), row(NV), row(NV), row(NQ)],
        out_shape=[jax.ShapeDtypeStruct((t, NQ), act_dtype), jax.ShapeDtypeStruct((t, NQ), act_dtype),
                   jax.ShapeDtypeStruct((t, NV), act_dtype), jax.ShapeDtypeStruct((t, NV), act_dtype),
                   jax.ShapeDtypeStruct((t, NQ), F32)],
        compiler_params=_params(1),
        name="gla_proj",
    )(x, gain, wq, wk, wv, wg, wa, wa2, ba2)


def _cumsum_rows(x):
    c = x.shape[0]
    row = lax.broadcasted_iota(jnp.int32, x.shape, 0)
    shift = 1
    while shift < c:
        x = x + jnp.where(row >= shift, pltpu.roll(x, shift, axis=0), 0.0)
        shift *= 2
    return x


def _row_to_col(row):
    return jnp.transpose(jnp.broadcast_to(row, (LANES, LANES)))[:, :1]


def _round_robin(chains):
    results = [None] * len(chains)
    live = list(enumerate(chains))
    while live:
        still = []
        for idx, chain in live:
            try:
                next(chain)
                still.append((idx, chain))
            except StopIteration as done:
                results[idx] = done.value
        live = still
    return results


def _coarsen(chain, n):
    count = 0
    while True:
        try:
            next(chain)
        except StopIteration as done:
            return done.value
        count += 1
        if count % n == 0:
            yield


def _gla_chunk_stages(q, k, v, lg, state):
    c = q.shape[0]
    bc = _cumsum_rows(lg)
    last = bc[c - 1:c, :]
    mid = bc[c // 2 - 1:c // 2, :]
    yield
    q_in = (q * jnp.exp(jnp.clip(bc - mid, -EXP_CLAMP, EXP_CLAMP))).astype(BF16)
    k_in = (k * jnp.exp(jnp.clip(mid - bc, -EXP_CLAMP, EXP_CLAMP))).astype(BF16)
    yield
    attn = _dot_nt(q_in, k_in)
    yield
    causal = lax.broadcasted_iota(jnp.int32, (c, c), 0) >= lax.broadcasted_iota(jnp.int32, (c, c), 1)
    attn = jnp.where(causal, attn, 0.0).astype(BF16)
    q_st = (q * jnp.exp(bc)).astype(BF16)
    yield
    o = _dot(attn, v) + _dot(q_st, state.astype(BF16))
    yield
    k_st = (k * jnp.exp(last - bc)).astype(BF16)
    decay_col = _row_to_col(jnp.exp(last))
    yield
    new_state = decay_col * state + _dot_tn(k_st, v)
    return o, new_state


def _gla_chunk(q, k, v, lg, state):
    return _round_robin([_gla_chunk_stages(q, k, v, lg, state)])[0]


def _gla_gate(o, o_gain, g):
    return _rms(o, o_gain) * (g * _sigmoid(g))


HEAD_COLS = 2 * GLA_DK + 2 * GLA_DV


def _gla_prompt_kernel(x_ref, gain_ref, wh_ref, wa_ref, wa2_ref, ba2_ref, ogain_ref,
                       mix_ref, sfin_ref, proj_a, proj_b, lg_a, lg_b, state_ref, *, tiles_per_seq):
    i = pl.program_id(0)
    lag = jnp.maximum(i - 1, 0)

    @pl.when(i == 0)
    def _():
        proj_b[...] = jnp.zeros_like(proj_b)
        lg_b[...] = jnp.zeros_like(lg_b)

    @pl.when(lag % tiles_per_seq == 0)
    def _():
        state_ref[...] = jnp.zeros_like(state_ref)

    def head_cols(h):
        c0 = h * HEAD_COLS
        return c0, c0 + GLA_DK, c0 + 2 * GLA_DK, c0 + 2 * GLA_DK + GLA_DV, c0 + HEAD_COLS

    def project(proj_w, lg_w):
        hn = _rms(x_ref[...], gain_ref[...]).astype(BF16)
        yield
        lg_w[...] = _log_forget(hn, wa_ref, wa2_ref, ba2_ref)
        for h in range(GLA_HEADS):
            c0, ck, cv, cg, c_end = head_cols(h)
            yield
            qk = _dot(hn, wh_ref[:, c0:cv])
            proj_w[:, c0:ck] = (qk[:, :GLA_DK] * GLA_DK ** -0.5).astype(BF16)
            proj_w[:, ck:cv] = qk[:, GLA_DK:].astype(BF16)
            yield
            proj_w[:, cv:cg] = _dot(hn, wh_ref[:, cv:cg]).astype(BF16)
            yield
            proj_w[:, cg:c_end] = _dot(hn, wh_ref[:, cg:c_end]).astype(BF16)

    def recur(h, proj_r, lg_r):
        c0, ck, cv, cg, c_end = head_cols(h)
        o, new_state = yield from _gla_chunk_stages(
            proj_r[:, c0:ck].astype(F32), proj_r[:, ck:cv].astype(F32), proj_r[:, cv:cg],
            lg_r[:, h * GLA_DK:(h + 1) * GLA_DK], state_ref[h])
        state_ref[h] = new_state
        yield
        mix_ref[:, h * GLA_DV:(h + 1) * GLA_DV] = _gla_gate(
            o, ogain_ref[...], proj_r[:, cg:c_end].astype(F32)).astype(mix_ref.dtype)

    def recur_all(proj_r, lg_r):
        for h in range(GLA_HEADS):
            yield from recur(h, proj_r, lg_r)
            yield

    def body(proj_w, lg_w, proj_r, lg_r):
        _round_robin([project(proj_w, lg_w), _coarsen(recur_all(proj_r, lg_r), 2)])

    @pl.when(i % 2 == 0)
    def _():
        body(proj_a, lg_a, proj_b, lg_b)

    @pl.when(i % 2 == 1)
    def _():
        body(proj_b, lg_b, proj_a, lg_a)

    @pl.when((lag % tiles_per_seq == tiles_per_seq - 1) & (i > 0))
    def _():
        sfin_ref[0] = state_ref[...]


def _gla_prompt(x, gain, wh, wa, wa2, ba2, o_gain, *, batch, seq, chunk):
    tps = seq // chunk
    n = batch * tps
    lag_i = lambda i: jnp.maximum(i - 1, 0)
    return pl.pallas_call(
        functools.partial(_gla_prompt_kernel, tiles_per_seq=tps),
        grid=(n + 1,),
        in_specs=[pl.BlockSpec((chunk, D_MODEL), lambda i: (jnp.minimum(i, n - 1), 0)),
                  _const_spec((1, D_MODEL)), _resident((D_MODEL, GLA_HEADS * HEAD_COLS)),
                  _const_spec((D_MODEL, RANK_PAD)), _const_spec((RANK_PAD, NQ)), _const_spec((1, NQ)),
                  _const_spec((1, GLA_DV))],
        out_specs=[pl.BlockSpec((chunk, NV), lambda i: (lag_i(i), 0)),
                   pl.BlockSpec((1, GLA_HEADS, GLA_DK, GLA_DV), lambda i: (lag_i(i) // tps, 0, 0, 0))],
        out_shape=[jax.ShapeDtypeStruct((batch * seq, NV), BF16),
                   jax.ShapeDtypeStruct((batch, GLA_HEADS, GLA_DK, GLA_DV), F32)],
        scratch_shapes=[pltpu.VMEM((chunk, GLA_HEADS * HEAD_COLS), BF16),
                        pltpu.VMEM((chunk, GLA_HEADS * HEAD_COLS), BF16),
                        pltpu.VMEM((chunk, NQ), F32), pltpu.VMEM((chunk, NQ), F32),
                        pltpu.VMEM((GLA_HEADS, GLA_DK, GLA_DV), F32)],
        compiler_params=_params(1),
        name="gla_prompt",
    )(x, gain, wh, wa, wa2, ba2, o_gain)


def _gla_sample_kernel(q_ref, k_ref, v_ref, g_ref, lg_ref, ogain_ref, s0_ref, mix_ref, sfin_ref, *, tb, s):
    def chain(b, h):
        rows = slice(b * s, (b + 1) * s)
        ks = slice(h * GLA_DK, (h + 1) * GLA_DK)
        vs = slice(h * GLA_DV, (h + 1) * GLA_DV)
        o, new_state = yield from _gla_chunk_stages(
            q_ref[rows, ks], k_ref[rows, ks], v_ref[rows, vs].astype(BF16), lg_ref[rows, ks],
            s0_ref[b, h])
        sfin_ref[b, h] = new_state
        yield
        mix_ref[rows, vs] = _gla_gate(o, ogain_ref[...], g_ref[rows, vs])

    _round_robin([chain(b, h) for b in range(tb) for h in range(GLA_HEADS)])


def _gla_sample(q, k, v, g, lg, o_gain, s0, *, batch, s, tb):
    rows = lambda cols: pl.BlockSpec((tb * s, cols), lambda i: (i, 0))
    st = pl.BlockSpec((tb, GLA_HEADS, GLA_DK, GLA_DV), lambda i: (i, 0, 0, 0))
    return pl.pallas_call(
        functools.partial(_gla_sample_kernel, tb=tb, s=s),
        grid=(batch // tb,),
        in_specs=[rows(NQ), rows(NQ), rows(NV), rows(NV), rows(NQ), _const_spec((1, GLA_DV)), st],
        out_specs=[rows(NV), st],
        out_shape=[jax.ShapeDtypeStruct((batch * s, NV), F32),
                   jax.ShapeDtypeStruct((batch, GLA_HEADS, GLA_DK, GLA_DV), F32)],
        compiler_params=_params(1),
        name="gla_sample",
    )(q, k, v, g, lg, o_gain, s0)


def _tail_body(h_ref, mix, p_ref, wproj_ref, gmlp_ref, wup_ref, wdown_ref, gple_ref, wgate_ref,
               wple_ref, side_work=()):
    h = h_ref[...] + _dot(mix.astype(BF16), wproj_ref[...])
    hn = _rms(h, gmlp_ref[...]).astype(BF16)
    ff_chunk = D_MODEL
    for j in range(D_FF // ff_chunk):
        if j < len(side_work):
            side_work[j]()
        cols = slice(j * ff_chunk, (j + 1) * ff_chunk)
        u = jnp.square(jnp.maximum(_dot(hn, wup_ref[:, cols]), 0.0)).astype(BF16)
        h = h + _dot(u, wdown_ref[cols, :])
    gate = _sigmoid(_dot(_rms(h, gple_ref[...]).astype(BF16), wgate_ref[...]))
    return h + gate * _dot(p_ref[...].astype(BF16), wple_ref[...])


def _tail0_kernel(h_ref, mix_ref, p_ref, wproj_ref, gmlp_ref, wup_ref, wdown_ref, gple_ref, wgate_ref,
                  wple_ref, gkv_ref, wk_ref, wv_ref, gq_ref, wq_ref,
                  hout_ref, kout_ref, vout_ref, qout_ref):
    h = _tail_body(h_ref, mix_ref[...], p_ref, wproj_ref, gmlp_ref, wup_ref, wdown_ref, gple_ref,
                   wgate_ref, wple_ref)
    hout_ref[...] = h
    hkv = _rms(h, gkv_ref[...]).astype(BF16)
    kout_ref[...] = _dot(hkv, wk_ref[...])
    vout_ref[...] = _dot(hkv, wv_ref[...])
    hq = _rms(h, gq_ref[...]).astype(BF16)
    qout_ref[...] = (_dot(hq, wq_ref[...]) * (SWA_HEAD_DIM ** -0.5 * LOG2E)).astype(qout_ref.dtype)


def _tail1_kernel(h_ref, mix_ref, p_ref, wproj_ref, gmlp_ref, wup_ref, wdown_ref, gple_ref, wgate_ref,
                  wple_ref, gfin_ref, y_ref):
    h = _tail_body(h_ref, mix_ref[...], p_ref, wproj_ref, gmlp_ref, wup_ref, wdown_ref, gple_ref,
                   wgate_ref, wple_ref)
    y_ref[...] = _rms(h, gfin_ref[...])


def _resident(shape):
    return pl.BlockSpec(shape, lambda *_: (0,) * len(shape), pipeline_mode=pl.Buffered(1))


def _layer_resident(shape, layer):
    return pl.BlockSpec((None,) + shape, lambda *_: (layer,) + (0,) * len(shape),
                        pipeline_mode=pl.Buffered(1))


def _tail_common_specs(tm, layer):
    row = lambda cols: pl.BlockSpec((tm, cols), lambda i: (i, 0))
    return [row(D_MODEL), row(D_MODEL), pl.BlockSpec((None, tm, PLE_DIM), lambda i: (layer, i, 0)),
            _resident((D_MODEL, D_MODEL)), _layer_resident((1, D_MODEL), layer),
            _layer_resident((D_MODEL, D_FF), layer), _layer_resident((D_FF, D_MODEL), layer),
            _layer_resident((1, D_MODEL), layer), _layer_resident((D_MODEL, D_MODEL), layer),
            _layer_resident((PLE_DIM, D_MODEL), layer)]


def _tail0(h, mix, p, wproj, gmlp, wup, wdown, gple, wgate, wple, gkv, wk, wv, gq, wq, *, tm, q_dtype):
    t = h.shape[0]
    row = lambda cols: pl.BlockSpec((tm, cols), lambda i: (i, 0))
    return pl.pallas_call(
        _tail0_kernel,
        grid=(t // tm,),
        in_specs=_tail_common_specs(tm, 0) + [
            _resident((1, D_MODEL)), _resident((D_MODEL, KV_COLS)), _resident((D_MODEL, KV_COLS)),
            _resident((1, D_MODEL)), _resident((D_MODEL, D_MODEL))],
        out_specs=[row(D_MODEL), row(KV_COLS), row(KV_COLS), row(D_MODEL)],
        out_shape=[jax.ShapeDtypeStruct((t, D_MODEL), F32), jax.ShapeDtypeStruct((t, KV_COLS), F32),
                   jax.ShapeDtypeStruct((t, KV_COLS), F32), jax.ShapeDtypeStruct((t, D_MODEL), q_dtype)],
        compiler_params=_params(1),
        name="tail0",
    )(h, mix, p, wproj, gmlp, wup, wdown, gple, wgate, wple, gkv, wk, wv, gq, wq)


def _tail1(h, mix, p, wproj, gmlp, wup, wdown, gple, wgate, wple, gfin, *, tm):
    t = h.shape[0]
    return pl.pallas_call(
        _tail1_kernel,
        grid=(t // tm,),
        in_specs=_tail_common_specs(tm, 1) + [_resident((1, D_MODEL))],
        out_specs=pl.BlockSpec((tm, D_MODEL), lambda i: (i, 0)),
        out_shape=jax.ShapeDtypeStruct((t, D_MODEL), F32),
        compiler_params=_params(1),
        name="tail1",
    )(h, mix, p, wproj, gmlp, wup, wdown, gple, wgate, wple, gfin)


def _bias_table(rb_ref, head, d, valid):
    val = jnp.full(d.shape, rb_ref[BUCKET_STARTS[0][1], head], F32)
    for start, bucket in BUCKET_STARTS[1:]:
        val = jnp.where(d >= start, rb_ref[bucket, head], val)
    return jnp.where(valid, val * LOG2E, NEG)


def _softmax_sink(logits, sink):
    m = jnp.maximum(jnp.max(logits, axis=-1, keepdims=True), sink)
    e = jnp.exp2(logits - m)
    return e / (jnp.sum(e, axis=-1, keepdims=True) + jnp.exp2(sink - m))


def _lane_block(shape):
    return lax.broadcasted_iota(jnp.int32, shape, 1) // SWA_HEAD_DIM


def _swa_query_block(q_ref, r0, kblk, vblk, table, sink_ref, bias_ref):
    w = WINDOW
    rows = SWA_KV_HEADS * w
    from_prev = (lax.broadcasted_iota(jnp.int32, (rows, w), 1)
                 > (lax.broadcasted_iota(jnp.int32, (rows, w), 0) & (w - 1)))
    lane_blk = _lane_block((w, KV_COLS))
    slabs = []
    for g in range(SWA_GROUP):
        slab = q_ref[r0:r0 + w, g * KV_COLS:(g + 1) * KV_COLS].astype(F32)
        q_stack = jnp.concatenate(
            [jnp.where(lane_blk == c, slab, 0.0) for c in range(SWA_KV_HEADS)], axis=0).astype(BF16)
        logits = _dot_nt(q_stack, kblk)
        folded = jnp.where(from_prev, logits[:, :w], logits[:, w:])
        es, scales = [], []
        for c in range(SWA_KV_HEADS):
            head = c * SWA_GROUP + g
            sink = sink_ref[0, head] * LOG2E
            lg = folded[c * w:(c + 1) * w] + bias_ref[table + head]
            m = jnp.maximum(jnp.max(lg, axis=-1, keepdims=True), sink)
            e = jnp.exp2(lg - m)
            scales.append(1.0 / (jnp.sum(e, axis=-1, keepdims=True) + jnp.exp2(sink - m)))
            es.append(e)
        e_all = jnp.concatenate(es, axis=0)
        probs = jnp.concatenate([jnp.where(from_prev, e_all, 0.0).astype(BF16),
                                 jnp.where(from_prev, 0.0, e_all).astype(BF16)], axis=1)
        wide = _dot(probs, vblk)
        out = jnp.zeros((w, KV_COLS), F32)
        for c in range(SWA_KV_HEADS):
            out = jnp.where(lane_blk == c, wide[c * w:(c + 1) * w] * scales[c], out)
        slabs.append(out.astype(BF16))
    return jnp.concatenate(slabs, axis=1)


def _layer1_prompt_kernel(rb_ref, sink_ref, q_ref, kp_ref, kc_ref, vp_ref, vc_ref,
                          h_ref, p_ref, wproj_ref, gmlp_ref, wup_ref, wdown_ref, gple_ref, wgate_ref,
                          wple_ref, gfin_ref, y_ref, bias_ref, kbuf_ref, vbuf_ref, attn_ref,
                          *, qb, tiles_per_seq):
    i = pl.program_id(0)
    last = pl.num_programs(0) - 2
    w = WINDOW

    @pl.when(i == 0)
    def _():
        r = lax.broadcasted_iota(jnp.int32, (w, w), 0)
        c = lax.broadcasted_iota(jnp.int32, (w, w), 1)
        d = jnp.where(c <= r, r - c, w + r - c)
        for head in range(SWA_HEADS):
            bias_ref[head] = _bias_table(rb_ref, head, d, d >= 0)
            bias_ref[SWA_HEADS + head] = _bias_table(rb_ref, head, d, c <= r)
        attn_ref[1] = jnp.zeros(attn_ref.shape[1:], attn_ref.dtype)

    kbuf_ref[0:w] = kp_ref[...].astype(BF16)
    kbuf_ref[w:] = kc_ref[...].astype(BF16)
    vbuf_ref[0:w] = vp_ref[...].astype(BF16)
    vbuf_ref[w:] = vc_ref[...].astype(BF16)
    tile = jnp.minimum(i, last)
    first_table = jnp.where(tile % tiles_per_seq == 0, SWA_HEADS, 0)
    slot = i % 2

    def attend(jb):
        r0 = jb * w
        attn_ref[slot, r0:r0 + w, :] = _swa_query_block(
            q_ref, r0, kbuf_ref[r0:r0 + 2 * w, :], vbuf_ref[r0:r0 + 2 * w, :],
            first_table if jb == 0 else 0, sink_ref, bias_ref)

    h = _tail_body(h_ref, attn_ref[(i + 1) % 2], p_ref, wproj_ref, gmlp_ref, wup_ref, wdown_ref,
                   gple_ref, wgate_ref, wple_ref,
                   side_work=[functools.partial(attend, jb) for jb in range(qb)])
    y_ref[...] = _rms(h, gfin_ref[...])


def _layer1_prompt(rel_bias, sinks, q, k, v, h, p, wproj, gmlp, wup, wdown, gple, wgate, wple, gfin,
                   *, seq, qb):
    t = h.shape[0]
    tm = qb * WINDOW
    n = t // tm
    cur_i = lambda i: jnp.minimum(i, n - 1)
    lag_i = lambda i: jnp.maximum(i - 1, 0)
    cur = lambda cols: pl.BlockSpec((tm, cols), lambda i: (cur_i(i), 0))
    prev = lambda cols: pl.BlockSpec((WINDOW, cols), lambda i: (jnp.maximum(cur_i(i) * qb - 1, 0), 0))
    lag = lambda cols: pl.BlockSpec((tm, cols), lambda i: (lag_i(i), 0))
    smem = pl.BlockSpec(memory_space=pltpu.SMEM)
    tail_specs = _tail_common_specs(tm, 1)
    return pl.pallas_call(
        functools.partial(_layer1_prompt_kernel, qb=qb, tiles_per_seq=seq // tm),
        grid=(n + 1,),
        in_specs=[smem, smem, cur(D_MODEL), prev(KV_COLS), cur(KV_COLS), prev(KV_COLS), cur(KV_COLS),
                  lag(D_MODEL), pl.BlockSpec((None, tm, PLE_DIM), lambda i: (1, lag_i(i), 0))]
                 + tail_specs[3:] + [_resident((1, D_MODEL))],
        out_specs=lag(D_MODEL),
        out_shape=jax.ShapeDtypeStruct((t, D_MODEL), F32),
        scratch_shapes=[pltpu.VMEM((2 * SWA_HEADS, WINDOW, WINDOW), F32),
                        pltpu.VMEM(((qb + 1) * WINDOW, KV_COLS), BF16),
                        pltpu.VMEM(((qb + 1) * WINDOW, KV_COLS), BF16),
                        pltpu.VMEM((2, tm, D_MODEL), BF16)],
        compiler_params=_params(1),
        name="layer1_prompt",
    )(rel_bias, sinks, q, k, k, v, v, h, p, wproj, gmlp, wup, wdown, gple, wgate, wple, gfin)


def _swa_sample_kernel(rb_ref, sink_ref, q_ref, kn_ref, vn_ref, ck_ref, cv_ref,
                       o_ref, ok_ref, ov_ref, bias_ref, sinkcol_ref, *, tb, s, n_past):
    nk = 2 * WINDOW

    @pl.when(pl.program_id(0) == 0)
    def _():
        d = (n_past + lax.broadcasted_iota(jnp.int32, (s, nk), 0)
             - lax.broadcasted_iota(jnp.int32, (s, nk), 1))
        valid = (d >= 0) & (d < WINDOW)
        for g in range(SWA_GROUP):
            for c in range(SWA_KV_HEADS):
                head = c * SWA_GROUP + g
                r0 = (g * SWA_KV_HEADS + c) * s
                bias_ref[r0:r0 + s, :] = _bias_table(rb_ref, head, d, valid)
                sinkcol_ref[r0:r0 + s, :] = jnp.full((s, LANES), sink_ref[0, head] * LOG2E, F32)

    lane_blk = _lane_block((s, KV_COLS))
    zero_rows = jnp.zeros((WINDOW - s, KV_COLS), F32)
    kn_t = jnp.transpose(kn_ref[...])
    vn_t = jnp.transpose(vn_ref[...])
    keep_old = lax.broadcasted_iota(jnp.int32, (KV_COLS, n_past), 1) < n_past - s

    def chain(b):
        r0 = b * s
        new_shift = (n_past - s - r0) % LANES
        place = (lambda x: pltpu.roll(x, new_shift, axis=1)) if new_shift else (lambda x: x)
        ok_ref[b] = jnp.where(keep_old, pltpu.roll(ck_ref[b], n_past - s, axis=1), place(kn_t))
        ov_ref[b] = jnp.where(keep_old, pltpu.roll(cv_ref[b], n_past - s, axis=1), place(vn_t))
        yield
        k_new = jnp.concatenate([kn_ref[r0:r0 + s, :], zero_rows], axis=0).astype(BF16)
        v_new = jnp.concatenate([vn_ref[r0:r0 + s, :], zero_rows], axis=0).astype(BF16)
        pieces = []
        for g in range(SWA_GROUP):
            slab = q_ref[r0:r0 + s, g * KV_COLS:(g + 1) * KV_COLS]
            pieces += [jnp.where(lane_blk == c, slab, 0.0) for c in range(SWA_KV_HEADS)]
        q_stack = jnp.concatenate(pieces, axis=0).astype(BF16)
        yield
        logits = jnp.concatenate([_dot(q_stack, ck_ref[b].astype(BF16)), _dot_nt(q_stack, k_new)],
                                 axis=1) + bias_ref[...]
        yield
        probs = _softmax_sink(logits, sinkcol_ref[:, :1]).astype(BF16)
        yield
        wide = _dot_nt(probs[:, :n_past], cv_ref[b].astype(BF16)) + _dot(probs[:, n_past:], v_new)
        yield
        for g in range(SWA_GROUP):
            out = jnp.zeros((s, KV_COLS), F32)
            for c in range(SWA_KV_HEADS):
                w0 = (g * SWA_KV_HEADS + c) * s
                out = jnp.where(lane_blk == c, wide[w0:w0 + s], out)
            o_ref[r0:r0 + s, g * KV_COLS:(g + 1) * KV_COLS] = out

    _round_robin([chain(b) for b in range(tb)])


def _swa_sample(rel_bias, sinks, q, k_new, v_new, cache_k, cache_v, *, batch, s, tb):
    n_past = cache_k.shape[2]
    assert n_past == WINDOW and tb * s == LANES
    rows = lambda cols: pl.BlockSpec((tb * s, cols), lambda i: (i, 0))
    cache = pl.BlockSpec((tb, KV_COLS, n_past), lambda i: (i, 0, 0))
    smem = pl.BlockSpec(memory_space=pltpu.SMEM)
    return pl.pallas_call(
        functools.partial(_swa_sample_kernel, tb=tb, s=s, n_past=n_past),
        grid=(batch // tb,),
        in_specs=[smem, smem, rows(D_MODEL), rows(KV_COLS), rows(KV_COLS), cache, cache],
        out_specs=[rows(D_MODEL), cache, cache],
        out_shape=[jax.ShapeDtypeStruct((batch * s, D_MODEL), F32),
                   jax.ShapeDtypeStruct(cache_k.shape, F32), jax.ShapeDtypeStruct(cache_v.shape, F32)],
        scratch_shapes=[pltpu.VMEM((SWA_HEADS * s, 2 * WINDOW), F32),
                        pltpu.VMEM((SWA_HEADS * s, LANES), F32)],
        compiler_params=_params(1),
        name="swa_sample",
    )(rel_bias, sinks, q, k_new, v_new, cache_k, cache_v)


def _prep_weights(norm_mix, norm_mlp, norm_ple, norm_kv, norm_final, w_in_a, w_a2, b_a2, gla_o_gain,
                  w_out_a, w_kv, w_q_b, w_o_b, w_up, w_down, w_ple, w_ple_gate):
    w_in = w_in_a[0]
    wq, wk, wv, wg, wa = jnp.split(w_in, [NQ, 2 * NQ, 2 * NQ + NV, 2 * NQ + 2 * NV], axis=-1)
    wa = jnp.pad(wa, ((0, 0), (0, RANK_PAD - GLA_RANK)))
    wa2 = jnp.pad(w_a2[0], ((0, RANK_PAD - GLA_RANK), (0, 0)))
    wq_b = w_q_b[0].reshape(D_MODEL, SWA_KV_HEADS, SWA_GROUP, SWA_HEAD_DIM).transpose(0, 2, 1, 3)
    wo_b = w_o_b[0].reshape(SWA_KV_HEADS, SWA_GROUP, SWA_HEAD_DIM, D_MODEL).transpose(1, 0, 2, 3)
    bf = lambda w: w.astype(BF16)
    row = lambda g: g.reshape(1, -1)
    layers = lambda g: g.reshape(g.shape[0], 1, -1)
    stacked = (layers(norm_mlp), bf(w_up), bf(w_down), layers(norm_ple), bf(w_ple_gate), bf(w_ple))
    per_head = lambda m, d: m.reshape(D_MODEL, GLA_HEADS, d)
    wh = jnp.concatenate([per_head(wq, GLA_DK), per_head(wk, GLA_DK), per_head(wv, GLA_DV),
                          per_head(wg, GLA_DV)], axis=-1).reshape(D_MODEL, GLA_HEADS * HEAD_COLS)
    return dict(
        gla=(row(norm_mix[0]), bf(wq), bf(wk), bf(wv), bf(wg), bf(wa), bf(wa2), row(b_a2[0])),
        gla_heads=(row(norm_mix[0]), bf(wh), bf(wa), bf(wa2), row(b_a2[0])),
        o_gain=row(gla_o_gain[0]),
        tail=[(bf(w_out_a[0]),) + stacked, (bf(wo_b.reshape(D_MODEL, D_MODEL)),) + stacked],
        shared=(row(norm_kv), bf(w_kv[:, :KV_COLS]), bf(w_kv[:, KV_COLS:]), row(norm_mix[1]),
                bf(wq_b.reshape(D_MODEL, D_MODEL))),
        final=row(norm_final),
    )


def kernel(x_prompt, x_sample, state_gla, cache_win_k, cache_win_v, p_prompt, p_sample, norm_mix, norm_mlp, norm_ple, norm_kv, norm_final, w_in_a, w_a2, b_a2, gla_o_gain, w_out_a, w_kv, w_q_b, w_o_b, sinks, rel_bias, w_up, w_down, w_ple, w_ple_gate):
    w = _prep_weights(norm_mix, norm_mlp, norm_ple, norm_kv, norm_final, w_in_a, w_a2, b_a2, gla_o_gain,
                      w_out_a, w_kv, w_q_b, w_o_b, w_up, w_down, w_ple, w_ple_gate)
    sink_row = sinks[0].reshape(1, SWA_HEADS)

    bp, sp, _ = x_prompt.shape
    tp = bp * sp
    xp = x_prompt.reshape(tp, D_MODEL)
    pp = p_prompt.reshape(2, tp, PLE_DIM)
    mix, state_prompt = _gla_prompt(xp, *w["gla_heads"], w["o_gain"], batch=bp, seq=sp,
                                    chunk=GLA_PROMPT_CHUNK)
    h, k_sh, v_sh, q_b = _tail0(xp, mix, pp, *w["tail"][0], *w["shared"], tm=TOKEN_TILE, q_dtype=BF16)
    y_prompt = _layer1_prompt(rel_bias, sink_row, q_b, k_sh, v_sh, h, pp, *w["tail"][1], w["final"],
                              seq=sp, qb=SWA_PROMPT_BLOCKS).reshape(bp, sp, D_MODEL)
    keep = min(WINDOW, sp)
    cache_shape = (bp, keep, SWA_KV_HEADS, SWA_HEAD_DIM)
    cache_k_prompt = k_sh.reshape(bp, sp, KV_COLS)[:, sp - keep:].reshape(cache_shape)
    cache_v_prompt = v_sh.reshape(bp, sp, KV_COLS)[:, sp - keep:].reshape(cache_shape)

    bs, ss, _ = x_sample.shape
    ts = bs * ss
    xs = x_sample.reshape(ts, D_MODEL)
    ps = p_sample.reshape(2, ts, PLE_DIM)
    n_past = cache_win_k.shape[1]
    q, k, v, g, lg = _gla_proj(xs, *w["gla"], tm=TOKEN_TILE, act_dtype=F32)
    mix, state_sample = _gla_sample(q, k, v, g, lg, w["o_gain"], state_gla[0], batch=bs, s=ss,
                                    tb=GLA_SAMPLE_BATCH)
    h, k_sh, v_sh, q_b = _tail0(xs, mix, ps, *w["tail"][0], *w["shared"], tm=TOKEN_TILE, q_dtype=F32)
    to_feature_major = lambda c: jnp.transpose(c, (0, 2, 3, 1)).reshape(bs, KV_COLS, n_past)
    from_feature_major = lambda c: jnp.transpose(
        c.reshape(bs, SWA_KV_HEADS, SWA_HEAD_DIM, n_past), (0, 3, 1, 2))
    attn, ck, cv = _swa_sample(rel_bias, sink_row, q_b, k_sh, v_sh, to_feature_major(cache_win_k),
                               to_feature_major(cache_win_v), batch=bs, s=ss, tb=SWA_SAMPLE_BATCH)
    y_sample = _tail1(h, attn, ps, *w["tail"][1], w["final"], tm=TOKEN_TILE).reshape(bs, ss, D_MODEL)

    return (y_prompt, y_sample, state_prompt[None], state_sample[None],
            cache_k_prompt, cache_v_prompt, from_feature_major(ck), from_feature_major(cv))
```

```python
import functools
import math

import numpy as np
import jax
import jax.numpy as jnp
from jax import lax
from jax.experimental import pallas as pl
from jax.experimental.pallas import tpu as pltpu

F32 = jnp.float32
BF16 = jnp.bfloat16

D_MODEL = 1024
D_FF = 4 * D_MODEL
PLE_DIM = 256
GLA_HEADS = 4
GLA_DK = 128
GLA_DV = 256
GLA_RANK = 16
GLA_TAU = 16.0
SWA_HEADS = 16
SWA_KV_HEADS = 4
SWA_GROUP = 4
SWA_HEAD_DIM = 64
WINDOW = 128
REL_BUCKETS = 32
REL_MAX_DIST = 128
EPS = 1e-6

NQ = GLA_HEADS * GLA_DK
NV = GLA_HEADS * GLA_DV
KV_COLS = SWA_KV_HEADS * SWA_HEAD_DIM
LANES = 128
RANK_PAD = LANES
NEG = -1e30
EXP_CLAMP = 80.0
LOG2E = math.log2(math.e)
VMEM_LIMIT = 62 * 1024 * 1024

TOKEN_TILE = 512
GLA_PROMPT_CHUNK = 256
GLA_SAMPLE_BATCH = 8
SWA_PROMPT_BLOCKS = 4
SWA_SAMPLE_BATCH = 16


def _rel_bucket_starts():
    max_exact = REL_BUCKETS // 2
    starts = {}
    for d in range(WINDOW):
        if d < max_exact:
            b = d
        else:
            b = max_exact + int(np.float32(
                np.log(np.float32(d) / np.float32(max_exact)) / np.float32(math.log(REL_MAX_DIST / max_exact))
                * np.float32(REL_BUCKETS - max_exact)))
            b = min(b, REL_BUCKETS - 1)
        starts.setdefault(b, d)
    return sorted((d, b) for b, d in starts.items())


BUCKET_STARTS = _rel_bucket_starts()


def _dot(a, b):
    return jnp.dot(a, b, preferred_element_type=F32)


def _dot_nt(a, b):
    return lax.dot_general(a, b, (((1,), (1,)), ((), ())), preferred_element_type=F32)


def _dot_tn(a, b):
    return lax.dot_general(a, b, (((0,), (0,)), ((), ())), preferred_element_type=F32)


def _rms(x, gain):
    return x * lax.rsqrt(jnp.mean(x * x, axis=-1, keepdims=True) + EPS) * gain


def _sigmoid(x):
    return 1.0 / (1.0 + jnp.exp(-x))


def _const_spec(shape):
    return pl.BlockSpec(shape, lambda *_: (0,) * len(shape))


def _params(n_axes, flags=None):
    return pltpu.CompilerParams(dimension_semantics=("arbitrary",) * n_axes,
                                vmem_limit_bytes=VMEM_LIMIT, flags=flags)


def _log_forget(hn, wa_ref, wa2_ref, ba2_ref):
    z = _dot(_dot(hn, wa_ref[...]).astype(BF16), wa2_ref[...]) + ba2_ref[...]
    return (jnp.minimum(z, 0.0) - jnp.log(1.0 + jnp.exp(-jnp.abs(z)))) * (1.0 / GLA_TAU)


def _gla_proj_kernel(x_ref, gain_ref, wq_ref, wk_ref, wv_ref, wg_ref, wa_ref, wa2_ref, ba2_ref,
                     q_ref, k_ref, v_ref, g_ref, lg_ref):
    hn = _rms(x_ref[...], gain_ref[...]).astype(BF16)
    q_ref[...] = (_dot(hn, wq_ref[...]) * GLA_DK ** -0.5).astype(q_ref.dtype)
    k_ref[...] = _dot(hn, wk_ref[...]).astype(k_ref.dtype)
    v_ref[...] = _dot(hn, wv_ref[...]).astype(v_ref.dtype)
    g_ref[...] = _dot(hn, wg_ref[...]).astype(g_ref.dtype)
    lg_ref[...] = _log_forget(hn, wa_ref, wa2_ref, ba2_ref)


def _gla_proj(x, gain, wq, wk, wv, wg, wa, wa2, ba2, *, tm, act_dtype):
    t = x.shape[0]
    row = lambda cols: pl.BlockSpec((tm, cols), lambda i: (i, 0))
    return pl.pallas_call(
        _gla_proj_kernel,
        grid=(t // tm,),
        in_specs=[row(D_MODEL), _const_spec((1, D_MODEL)),
                  _const_spec((D_MODEL, NQ)), _const_spec((D_MODEL, NQ)),
                  _const_spec((D_MODEL, NV)), _const_spec((D_MODEL, NV)),
                  _const_spec((D_MODEL, RANK_PAD)), _const_spec((RANK_PAD, NQ)), _const_spec((1, NQ))],
        out_specs=[row(NQ), row(NQ), row(NV), row(NV), row(NQ)],
        out_shape=[jax.ShapeDtypeStruct((t, NQ), act_dtype), jax.ShapeDtypeStruct((t, NQ), act_dtype),
                   jax.ShapeDtypeStruct((t, NV), act_dtype), jax.ShapeDtypeStruct((t, NV), act_dtype),
                   jax.ShapeDtypeStruct((t, NQ), F32)],
        compiler_params=_params(1),
        name="gla_proj",
    )(x, gain, wq, wk, wv, wg, wa, wa2, ba2)


def _cumsum_rows(x):
    c = x.shape[0]
    row = lax.broadcasted_iota(jnp.int32, x.shape, 0)
    shift = 1
    while shift < c:
        x = x + jnp.where(row >= shift, pltpu.roll(x, shift, axis=0), 0.0)
        shift *= 2
    return x


def _row_to_col(row):
    return jnp.transpose(jnp.broadcast_to(row, (LANES, LANES)))[:, :1]


def _round_robin(chains):
    results = [None] * len(chains)
    live = list(enumerate(chains))
    while live:
        still = []
        for idx, chain in live:
            try:
                next(chain)
                still.append((idx, chain))
            except StopIteration as done:
                results[idx] = done.value
        live = still
    return results


def _coarsen(chain, n):
    count = 0
    while True:
        try:
            next(chain)
        except StopIteration as done:
            return done.value
        count += 1
        if count % n == 0:
            yield


def _gla_chunk_stages(q, k, v, lg, state):
    c = q.shape[0]
    bc = _cumsum_rows(lg)
    last = bc[c - 1:c, :]
    mid = bc[c // 2 - 1:c // 2, :]
    out_of_range = jnp.max(jnp.maximum(bc[0:1, :] - mid, mid - last)) > EXP_CLAMP
    yield
    q_in = (q * jnp.exp(jnp.clip(bc - mid, -EXP_CLAMP, EXP_CLAMP))).astype(BF16)
    k_in = (k * jnp.exp(jnp.clip(mid - bc, -EXP_CLAMP, EXP_CLAMP))).astype(BF16)
    yield
    attn = _dot_nt(q_in, k_in)
    yield
    causal = lax.broadcasted_iota(jnp.int32, (c, c), 0) >= lax.broadcasted_iota(jnp.int32, (c, c), 1)
    attn = jnp.where(causal, attn, 0.0).astype(BF16)
    q_st = (q * jnp.exp(bc)).astype(BF16)
    yield
    o = _dot(attn, v) + _dot(q_st, state.astype(BF16))
    yield
    k_st = (k * jnp.exp(last - bc)).astype(BF16)
    decay_col = _row_to_col(jnp.exp(last))
    yield
    new_state = decay_col * state + _dot_tn(k_st, v)
    return o, new_state, out_of_range


def _gla_exact_output(q, k, v, lg, state, scratch):
    qs_ref, ks_ref, bs_ref, attn_ref = scratch
    c = q.shape[0]
    bc = _cumsum_rows(lg)
    qs_ref[...] = q
    ks_ref[...] = k
    bs_ref[...] = bc
    attn_ref[...] = jnp.zeros_like(attn_ref)
    col_id = lax.broadcasted_iota(jnp.int32, (c, c), 1)

    def key_row(j, carry):
        decay = jnp.exp(jnp.minimum(bs_ref[...] - bs_ref[pl.ds(j, 1), :], 0.0))
        col = jnp.sum(qs_ref[...] * decay * ks_ref[pl.ds(j, 1), :], axis=1, keepdims=True)
        attn_ref[...] = jnp.where(col_id == j, col, attn_ref[...])
        return carry

    lax.fori_loop(0, c, key_row, 0)
    causal = lax.broadcasted_iota(jnp.int32, (c, c), 0) >= col_id
    attn = jnp.where(causal, attn_ref[...], 0.0).astype(BF16)
    return _dot(attn, v) + _dot((q * jnp.exp(bc)).astype(BF16), state.astype(BF16))


def _exact_scratch(c):
    return [pltpu.VMEM((c, GLA_DK), F32), pltpu.VMEM((c, GLA_DK), F32), pltpu.VMEM((c, GLA_DK), F32),
            pltpu.VMEM((c, c), F32)]


def _gla_gate(o, o_gain, g):
    return _rms(o, o_gain) * (g * _sigmoid(g))


HEAD_COLS = 2 * GLA_DK + 2 * GLA_DV


def _gla_prompt_kernel(x_ref, gain_ref, wh_ref, wa_ref, wa2_ref, ba2_ref, ogain_ref,
                       mix_ref, sfin_ref, proj_a, proj_b, lg_a, lg_b, state_ref, sprev_ref,
                       *exact_scratch, tiles_per_seq):
    i = pl.program_id(0)
    lag = jnp.maximum(i - 1, 0)

    @pl.when(i == 0)
    def _():
        proj_b[...] = jnp.zeros_like(proj_b)
        lg_b[...] = jnp.zeros_like(lg_b)

    @pl.when(lag % tiles_per_seq == 0)
    def _():
        state_ref[...] = jnp.zeros_like(state_ref)

    def head_cols(h):
        c0 = h * HEAD_COLS
        return c0, c0 + GLA_DK, c0 + 2 * GLA_DK, c0 + 2 * GLA_DK + GLA_DV, c0 + HEAD_COLS

    def project(proj_w, lg_w):
        hn = _rms(x_ref[...], gain_ref[...]).astype(BF16)
        yield
        lg_w[...] = _log_forget(hn, wa_ref, wa2_ref, ba2_ref)
        for h in range(GLA_HEADS):
            c0, ck, cv, cg, c_end = head_cols(h)
            yield
            qk = _dot(hn, wh_ref[:, c0:cv])
            proj_w[:, c0:ck] = (qk[:, :GLA_DK] * GLA_DK ** -0.5).astype(BF16)
            proj_w[:, ck:cv] = qk[:, GLA_DK:].astype(BF16)
            yield
            proj_w[:, cv:cg] = _dot(hn, wh_ref[:, cv:cg]).astype(BF16)
            yield
            proj_w[:, cg:c_end] = _dot(hn, wh_ref[:, cg:c_end]).astype(BF16)

    def head_inputs(h, proj_r, lg_r):
        c0, ck, cv, cg, c_end = head_cols(h)
        return (proj_r[:, c0:ck].astype(F32), proj_r[:, ck:cv].astype(F32), proj_r[:, cv:cg],
                lg_r[:, h * GLA_DK:(h + 1) * GLA_DK])

    def write_mix(h, o, proj_r):
        cg, c_end = head_cols(h)[3:]
        mix_ref[:, h * GLA_DV:(h + 1) * GLA_DV] = _gla_gate(
            o, ogain_ref[...], proj_r[:, cg:c_end].astype(F32)).astype(mix_ref.dtype)

    def recur(h, proj_r, lg_r):
        state = state_ref[h]
        sprev_ref[h] = state
        o, new_state, out_of_range = yield from _gla_chunk_stages(*head_inputs(h, proj_r, lg_r), state)
        state_ref[h] = new_state
        yield
        write_mix(h, o, proj_r)
        return out_of_range

    def recur_all(proj_r, lg_r):
        flags = []
        for h in range(GLA_HEADS):
            flags.append((yield from recur(h, proj_r, lg_r)))
            yield
        return flags

    def body(proj_w, lg_w, proj_r, lg_r):
        _, flags = _round_robin([project(proj_w, lg_w), _coarsen(recur_all(proj_r, lg_r), 2)])
        for h, out_of_range in enumerate(flags):
            @pl.when(out_of_range)
            def _(h=h):
                o = _gla_exact_output(*head_inputs(h, proj_r, lg_r), sprev_ref[h], exact_scratch)
                write_mix(h, o, proj_r)

    @pl.when(i % 2 == 0)
    def _():
        body(proj_a, lg_a, proj_b, lg_b)

    @pl.when(i % 2 == 1)
    def _():
        body(proj_b, lg_b, proj_a, lg_a)

    @pl.when((lag % tiles_per_seq == tiles_per_seq - 1) & (i > 0))
    def _():
        sfin_ref[0] = state_ref[...]


def _gla_prompt(x, gain, wh, wa, wa2, ba2, o_gain, *, batch, seq, chunk):
    tps = seq // chunk
    n = batch * tps
    lag_i = lambda i: jnp.maximum(i - 1, 0)
    return pl.pallas_call(
        functools.partial(_gla_prompt_kernel, tiles_per_seq=tps),
        grid=(n + 1,),
        in_specs=[pl.BlockSpec((chunk, D_MODEL), lambda i: (jnp.minimum(i, n - 1), 0)),
                  _const_spec((1, D_MODEL)), _resident((D_MODEL, GLA_HEADS * HEAD_COLS)),
                  _const_spec((D_MODEL, RANK_PAD)), _const_spec((RANK_PAD, NQ)), _const_spec((1, NQ)),
                  _const_spec((1, GLA_DV))],
        out_specs=[pl.BlockSpec((chunk, NV), lambda i: (lag_i(i), 0)),
                   pl.BlockSpec((1, GLA_HEADS, GLA_DK, GLA_DV), lambda i: (lag_i(i) // tps, 0, 0, 0))],
        out_shape=[jax.ShapeDtypeStruct((batch * seq, NV), BF16),
                   jax.ShapeDtypeStruct((batch, GLA_HEADS, GLA_DK, GLA_DV), F32)],
        scratch_shapes=[pltpu.VMEM((chunk, GLA_HEADS * HEAD_COLS), BF16),
                        pltpu.VMEM((chunk, GLA_HEADS * HEAD_COLS), BF16),
                        pltpu.VMEM((chunk, NQ), F32), pltpu.VMEM((chunk, NQ), F32),
                        pltpu.VMEM((GLA_HEADS, GLA_DK, GLA_DV), F32),
                        pltpu.VMEM((GLA_HEADS, GLA_DK, GLA_DV), F32)] + _exact_scratch(chunk),
        compiler_params=_params(1),
        name="gla_prompt",
    )(x, gain, wh, wa, wa2, ba2, o_gain)


def _gla_sample_kernel(q_ref, k_ref, v_ref, g_ref, lg_ref, ogain_ref, s0_ref, mix_ref, sfin_ref,
                       *exact_scratch, tb, s):
    def inputs(b, h):
        rows = slice(b * s, (b + 1) * s)
        ks = slice(h * GLA_DK, (h + 1) * GLA_DK)
        vs = slice(h * GLA_DV, (h + 1) * GLA_DV)
        return (q_ref[rows, ks], k_ref[rows, ks], v_ref[rows, vs].astype(BF16), lg_ref[rows, ks],
                s0_ref[b, h])

    def write_mix(b, h, o):
        rows = slice(b * s, (b + 1) * s)
        vs = slice(h * GLA_DV, (h + 1) * GLA_DV)
        mix_ref[rows, vs] = _gla_gate(o, ogain_ref[...], g_ref[rows, vs])

    def chain(b, h):
        o, new_state, out_of_range = yield from _gla_chunk_stages(*inputs(b, h))
        sfin_ref[b, h] = new_state
        yield
        write_mix(b, h, o)
        return out_of_range

    pairs = [(b, h) for b in range(tb) for h in range(GLA_HEADS)]
    flags = _round_robin([chain(b, h) for b, h in pairs])
    for (b, h), out_of_range in zip(pairs, flags):
        @pl.when(out_of_range)
        def _(b=b, h=h):
            write_mix(b, h, _gla_exact_output(*inputs(b, h), exact_scratch))


def _gla_sample(q, k, v, g, lg, o_gain, s0, *, batch, s, tb):
    rows = lambda cols: pl.BlockSpec((tb * s, cols), lambda i: (i, 0))
    st = pl.BlockSpec((tb, GLA_HEADS, GLA_DK, GLA_DV), lambda i: (i, 0, 0, 0))
    return pl.pallas_call(
        functools.partial(_gla_sample_kernel, tb=tb, s=s),
        grid=(batch // tb,),
        in_specs=[rows(NQ), rows(NQ), rows(NV), rows(NV), rows(NQ), _const_spec((1, GLA_DV)), st],
        out_specs=[rows(NV), st],
        out_shape=[jax.ShapeDtypeStruct((batch * s, NV), F32),
                   jax.ShapeDtypeStruct((batch, GLA_HEADS, GLA_DK, GLA_DV), F32)],
        scratch_shapes=_exact_scratch(s),
        compiler_params=_params(1),
        name="gla_sample",
    )(q, k, v, g, lg, o_gain, s0)


def _tail_body(h_ref, mix, p_ref, wproj_ref, gmlp_ref, wup_ref, wdown_ref, gple_ref, wgate_ref,
               wple_ref, side_work=()):
    h = h_ref[...] + _dot(mix.astype(BF16), wproj_ref[...])
    hn = _rms(h, gmlp_ref[...]).astype(BF16)
    ff_chunk = D_MODEL
    for j in range(D_FF // ff_chunk):
        if j < len(side_work):
            side_work[j]()
        cols = slice(j * ff_chunk, (j + 1) * ff_chunk)
        u = jnp.square(jnp.maximum(_dot(hn, wup_ref[:, cols]), 0.0)).astype(BF16)
        h = h + _dot(u, wdown_ref[cols, :])
    gate = _sigmoid(_dot(_rms(h, gple_ref[...]).astype(BF16), wgate_ref[...]))
    return h + gate * _dot(p_ref[...].astype(BF16), wple_ref[...])


def _tail0_kernel(h_ref, mix_ref, p_ref, wproj_ref, gmlp_ref, wup_ref, wdown_ref, gple_ref, wgate_ref,
                  wple_ref, gkv_ref, wk_ref, wv_ref, gq_ref, wq_ref,
                  hout_ref, kout_ref, vout_ref, qout_ref):
    h = _tail_body(h_ref, mix_ref[...], p_ref, wproj_ref, gmlp_ref, wup_ref, wdown_ref, gple_ref,
                   wgate_ref, wple_ref)
    hout_ref[...] = h
    hkv = _rms(h, gkv_ref[...]).astype(BF16)
    kout_ref[...] = _dot(hkv, wk_ref[...])
    vout_ref[...] = _dot(hkv, wv_ref[...])
    hq = _rms(h, gq_ref[...]).astype(BF16)
    qout_ref[...] = (_dot(hq, wq_ref[...]) * (SWA_HEAD_DIM ** -0.5 * LOG2E)).astype(qout_ref.dtype)


def _tail1_kernel(h_ref, mix_ref, p_ref, wproj_ref, gmlp_ref, wup_ref, wdown_ref, gple_ref, wgate_ref,
                  wple_ref, gfin_ref, y_ref):
    h = _tail_body(h_ref, mix_ref[...], p_ref, wproj_ref, gmlp_ref, wup_ref, wdown_ref, gple_ref,
                   wgate_ref, wple_ref)
    y_ref[...] = _rms(h, gfin_ref[...])


def _resident(shape):
    return pl.BlockSpec(shape, lambda *_: (0,) * len(shape), pipeline_mode=pl.Buffered(1))


def _layer_resident(shape, layer):
    return pl.BlockSpec((None,) + shape, lambda *_: (layer,) + (0,) * len(shape),
                        pipeline_mode=pl.Buffered(1))


def _tail_common_specs(tm, layer):
    row = lambda cols: pl.BlockSpec((tm, cols), lambda i: (i, 0))
    return [row(D_MODEL), row(D_MODEL), pl.BlockSpec((None, tm, PLE_DIM), lambda i: (layer, i, 0)),
            _resident((D_MODEL, D_MODEL)), _layer_resident((1, D_MODEL), layer),
            _layer_resident((D_MODEL, D_FF), layer), _layer_resident((D_FF, D_MODEL), layer),
            _layer_resident((1, D_MODEL), layer), _layer_resident((D_MODEL, D_MODEL), layer),
            _layer_resident((PLE_DIM, D_MODEL), layer)]


def _tail0(h, mix, p, wproj, gmlp, wup, wdown, gple, wgate, wple, gkv, wk, wv, gq, wq, *, tm, q_dtype):
    t = h.shape[0]
    row = lambda cols: pl.BlockSpec((tm, cols), lambda i: (i, 0))
    return pl.pallas_call(
        _tail0_kernel,
        grid=(t // tm,),
        in_specs=_tail_common_specs(tm, 0) + [
            _resident((1, D_MODEL)), _resident((D_MODEL, KV_COLS)), _resident((D_MODEL, KV_COLS)),
            _resident((1, D_MODEL)), _resident((D_MODEL, D_MODEL))],
        out_specs=[row(D_MODEL), row(KV_COLS), row(KV_COLS), row(D_MODEL)],
        out_shape=[jax.ShapeDtypeStruct((t, D_MODEL), F32), jax.ShapeDtypeStruct((t, KV_COLS), F32),
                   jax.ShapeDtypeStruct((t, KV_COLS), F32), jax.ShapeDtypeStruct((t, D_MODEL), q_dtype)],
        compiler_params=_params(1),
        name="tail0",
    )(h, mix, p, wproj, gmlp, wup, wdown, gple, wgate, wple, gkv, wk, wv, gq, wq)


def _tail1(h, mix, p, wproj, gmlp, wup, wdown, gple, wgate, wple, gfin, *, tm):
    t = h.shape[0]
    return pl.pallas_call(
        _tail1_kernel,
        grid=(t // tm,),
        in_specs=_tail_common_specs(tm, 1) + [_resident((1, D_MODEL))],
        out_specs=pl.BlockSpec((tm, D_MODEL), lambda i: (i, 0)),
        out_shape=jax.ShapeDtypeStruct((t, D_MODEL), F32),
        compiler_params=_params(1),
        name="tail1",
    )(h, mix, p, wproj, gmlp, wup, wdown, gple, wgate, wple, gfin)


def _bias_table(rb_ref, head, d, valid):
    val = jnp.full(d.shape, rb_ref[BUCKET_STARTS[0][1], head], F32)
    for start, bucket in BUCKET_STARTS[1:]:
        val = jnp.where(d >= start, rb_ref[bucket, head], val)
    return jnp.where(valid, val * LOG2E, NEG)


def _softmax_sink(logits, sink):
    m = jnp.maximum(jnp.max(logits, axis=-1, keepdims=True), sink)
    e = jnp.exp2(logits - m)
    return e / (jnp.sum(e, axis=-1, keepdims=True) + jnp.exp2(sink - m))


def _lane_block(shape):
    return lax.broadcasted_iota(jnp.int32, shape, 1) // SWA_HEAD_DIM


def _swa_query_block(q_ref, r0, kblk, vblk, table, sink_ref, bias_ref):
    w = WINDOW
    rows = SWA_KV_HEADS * w
    from_prev = (lax.broadcasted_iota(jnp.int32, (rows, w), 1)
                 > (lax.broadcasted_iota(jnp.int32, (rows, w), 0) & (w - 1)))
    lane_blk = _lane_block((w, KV_COLS))
    slabs = []
    for g in range(SWA_GROUP):
        slab = q_ref[r0:r0 + w, g * KV_COLS:(g + 1) * KV_COLS].astype(F32)
        q_stack = jnp.concatenate(
            [jnp.where(lane_blk == c, slab, 0.0) for c in range(SWA_KV_HEADS)], axis=0).astype(BF16)
        logits = _dot_nt(q_stack, kblk)
        folded = jnp.where(from_prev, logits[:, :w], logits[:, w:])
        es, scales = [], []
        for c in range(SWA_KV_HEADS):
            head = c * SWA_GROUP + g
            sink = sink_ref[0, head] * LOG2E
            lg = folded[c * w:(c + 1) * w] + bias_ref[table + head]
            m = jnp.maximum(jnp.max(lg, axis=-1, keepdims=True), sink)
            e = jnp.exp2(lg - m)
            scales.append(1.0 / (jnp.sum(e, axis=-1, keepdims=True) + jnp.exp2(sink - m)))
            es.append(e)
        e_all = jnp.concatenate(es, axis=0)
        probs = jnp.concatenate([jnp.where(from_prev, e_all, 0.0).astype(BF16),
                                 jnp.where(from_prev, 0.0, e_all).astype(BF16)], axis=1)
        wide = _dot(probs, vblk)
        out = jnp.zeros((w, KV_COLS), F32)
        for c in range(SWA_KV_HEADS):
            out = jnp.where(lane_blk == c, wide[c * w:(c + 1) * w] * scales[c], out)
        slabs.append(out.astype(BF16))
    return jnp.concatenate(slabs, axis=1)


def _layer1_prompt_kernel(rb_ref, sink_ref, q_ref, kp_ref, kc_ref, vp_ref, vc_ref,
                          h_ref, p_ref, wproj_ref, gmlp_ref, wup_ref, wdown_ref, gple_ref, wgate_ref,
                          wple_ref, gfin_ref, y_ref, bias_ref, kbuf_ref, vbuf_ref, attn_ref,
                          *, qb, tiles_per_seq):
    i = pl.program_id(0)
    last = pl.num_programs(0) - 2
    w = WINDOW

    @pl.when(i == 0)
    def _():
        r = lax.broadcasted_iota(jnp.int32, (w, w), 0)
        c = lax.broadcasted_iota(jnp.int32, (w, w), 1)
        d = jnp.where(c <= r, r - c, w + r - c)
        for head in range(SWA_HEADS):
            bias_ref[head] = _bias_table(rb_ref, head, d, d >= 0)
            bias_ref[SWA_HEADS + head] = _bias_table(rb_ref, head, d, c <= r)
        attn_ref[1] = jnp.zeros(attn_ref.shape[1:], attn_ref.dtype)

    kbuf_ref[0:w] = kp_ref[...].astype(BF16)
    kbuf_ref[w:] = kc_ref[...].astype(BF16)
    vbuf_ref[0:w] = vp_ref[...].astype(BF16)
    vbuf_ref[w:] = vc_ref[...].astype(BF16)
    tile = jnp.minimum(i, last)
    first_table = jnp.where(tile % tiles_per_seq == 0, SWA_HEADS, 0)
    slot = i % 2

    def attend(jb):
        r0 = jb * w
        attn_ref[slot, r0:r0 + w, :] = _swa_query_block(
            q_ref, r0, kbuf_ref[r0:r0 + 2 * w, :], vbuf_ref[r0:r0 + 2 * w, :],
            first_table if jb == 0 else 0, sink_ref, bias_ref)

    h = _tail_body(h_ref, attn_ref[(i + 1) % 2], p_ref, wproj_ref, gmlp_ref, wup_ref, wdown_ref,
                   gple_ref, wgate_ref, wple_ref,
                   side_work=[functools.partial(attend, jb) for jb in range(qb)])
    y_ref[...] = _rms(h, gfin_ref[...])


def _layer1_prompt(rel_bias, sinks, q, k, v, h, p, wproj, gmlp, wup, wdown, gple, wgate, wple, gfin,
                   *, seq, qb):
    t = h.shape[0]
    tm = qb * WINDOW
    n = t // tm
    cur_i = lambda i: jnp.minimum(i, n - 1)
    lag_i = lambda i: jnp.maximum(i - 1, 0)
    cur = lambda cols: pl.BlockSpec((tm, cols), lambda i: (cur_i(i), 0))
    prev = lambda cols: pl.BlockSpec((WINDOW, cols), lambda i: (jnp.maximum(cur_i(i) * qb - 1, 0), 0))
    lag = lambda cols: pl.BlockSpec((tm, cols), lambda i: (lag_i(i), 0))
    smem = pl.BlockSpec(memory_space=pltpu.SMEM)
    tail_specs = _tail_common_specs(tm, 1)
    return pl.pallas_call(
        functools.partial(_layer1_prompt_kernel, qb=qb, tiles_per_seq=seq // tm),
        grid=(n + 1,),
        in_specs=[smem, smem, cur(D_MODEL), prev(KV_COLS), cur(KV_COLS), prev(KV_COLS), cur(KV_COLS),
                  lag(D_MODEL), pl.BlockSpec((None, tm, PLE_DIM), lambda i: (1, lag_i(i), 0))]
                 + tail_specs[3:] + [_resident((1, D_MODEL))],
        out_specs=lag(D_MODEL),
        out_shape=jax.ShapeDtypeStruct((t, D_MODEL), F32),
        scratch_shapes=[pltpu.VMEM((2 * SWA_HEADS, WINDOW, WINDOW), F32),
                        pltpu.VMEM(((qb + 1) * WINDOW, KV_COLS), BF16),
                        pltpu.VMEM(((qb + 1) * WINDOW, KV_COLS), BF16),
                        pltpu.VMEM((2, tm, D_MODEL), BF16)],
        compiler_params=_params(1),
        name="layer1_prompt",
    )(rel_bias, sinks, q, k, k, v, v, h, p, wproj, gmlp, wup, wdown, gple, wgate, wple, gfin)


def _swa_sample_kernel(rb_ref, sink_ref, q_ref, kn_ref, vn_ref, ck_ref, cv_ref,
                       o_ref, ok_ref, ov_ref, bias_ref, sinkcol_ref, *, tb, s, n_past):
    nk = 2 * WINDOW

    @pl.when(pl.program_id(0) == 0)
    def _():
        d = (n_past + lax.broadcasted_iota(jnp.int32, (s, nk), 0)
             - lax.broadcasted_iota(jnp.int32, (s, nk), 1))
        valid = (d >= 0) & (d < WINDOW)
        for g in range(SWA_GROUP):
            for c in range(SWA_KV_HEADS):
                head = c * SWA_GROUP + g
                r0 = (g * SWA_KV_HEADS + c) * s
                bias_ref[r0:r0 + s, :] = _bias_table(rb_ref, head, d, valid)
                sinkcol_ref[r0:r0 + s, :] = jnp.full((s, LANES), sink_ref[0, head] * LOG2E, F32)

    lane_blk = _lane_block((s, KV_COLS))
    zero_rows = jnp.zeros((WINDOW - s, KV_COLS), F32)
    kn_t = jnp.transpose(kn_ref[...])
    vn_t = jnp.transpose(vn_ref[...])
    keep_old = lax.broadcasted_iota(jnp.int32, (KV_COLS, n_past), 1) < n_past - s

    def chain(b):
        r0 = b * s
        new_shift = (n_past - s - r0) % LANES
        place = (lambda x: pltpu.roll(x, new_shift, axis=1)) if new_shift else (lambda x: x)
        ok_ref[b] = jnp.where(keep_old, pltpu.roll(ck_ref[b], n_past - s, axis=1), place(kn_t))
        ov_ref[b] = jnp.where(keep_old, pltpu.roll(cv_ref[b], n_past - s, axis=1), place(vn_t))
        yield
        k_new = jnp.concatenate([kn_ref[r0:r0 + s, :], zero_rows], axis=0).astype(BF16)
        v_new = jnp.concatenate([vn_ref[r0:r0 + s, :], zero_rows], axis=0).astype(BF16)
        pieces = []
        for g in range(SWA_GROUP):
            slab = q_ref[r0:r0 + s, g * KV_COLS:(g + 1) * KV_COLS]
            pieces += [jnp.where(lane_blk == c, slab, 0.0) for c in range(SWA_KV_HEADS)]
        q_stack = jnp.concatenate(pieces, axis=0).astype(BF16)
        yield
        logits = jnp.concatenate([_dot(q_stack, ck_ref[b].astype(BF16)), _dot_nt(q_stack, k_new)],
                                 axis=1) + bias_ref[...]
        yield
        probs = _softmax_sink(logits, sinkcol_ref[:, :1]).astype(BF16)
        yield
        wide = _dot_nt(probs[:, :n_past], cv_ref[b].astype(BF16)) + _dot(probs[:, n_past:], v_new)
        yield
        for g in range(SWA_GROUP):
            out = jnp.zeros((s, KV_COLS), F32)
            for c in range(SWA_KV_HEADS):
                w0 = (g * SWA_KV_HEADS + c) * s
                out = jnp.where(lane_blk == c, wide[w0:w0 + s], out)
            o_ref[r0:r0 + s, g * KV_COLS:(g + 1) * KV_COLS] = out

    _round_robin([chain(b) for b in range(tb)])


def _swa_sample(rel_bias, sinks, q, k_new, v_new, cache_k, cache_v, *, batch, s, tb):
    n_past = cache_k.shape[2]
    assert n_past == WINDOW and tb * s == LANES
    rows = lambda cols: pl.BlockSpec((tb * s, cols), lambda i: (i, 0))
    cache = pl.BlockSpec((tb, KV_COLS, n_past), lambda i: (i, 0, 0))
    smem = pl.BlockSpec(memory_space=pltpu.SMEM)
    return pl.pallas_call(
        functools.partial(_swa_sample_kernel, tb=tb, s=s, n_past=n_past),
        grid=(batch // tb,),
        in_specs=[smem, smem, rows(D_MODEL), rows(KV_COLS), rows(KV_COLS), cache, cache],
        out_specs=[rows(D_MODEL), cache, cache],
        out_shape=[jax.ShapeDtypeStruct((batch * s, D_MODEL), F32),
                   jax.ShapeDtypeStruct(cache_k.shape, F32), jax.ShapeDtypeStruct(cache_v.shape, F32)],
        scratch_shapes=[pltpu.VMEM((SWA_HEADS * s, 2 * WINDOW), F32),
                        pltpu.VMEM((SWA_HEADS * s, LANES), F32)],
        compiler_params=_params(1),
        name="swa_sample",
    )(rel_bias, sinks, q, k_new, v_new, cache_k, cache_v)


def _prep_weights(norm_mix, norm_mlp, norm_ple, norm_kv, norm_final, w_in_a, w_a2, b_a2, gla_o_gain,
                  w_out_a, w_kv, w_q_b, w_o_b, w_up, w_down, w_ple, w_ple_gate):
    w_in = w_in_a[0]
    wq, wk, wv, wg, wa = jnp.split(w_in, [NQ, 2 * NQ, 2 * NQ + NV, 2 * NQ + 2 * NV], axis=-1)
    wa = jnp.pad(wa, ((0, 0), (0, RANK_PAD - GLA_RANK)))
    wa2 = jnp.pad(w_a2[0], ((0, RANK_PAD - GLA_RANK), (0, 0)))
    wq_b = w_q_b[0].reshape(D_MODEL, SWA_KV_HEADS, SWA_GROUP, SWA_HEAD_DIM).transpose(0, 2, 1, 3)
    wo_b = w_o_b[0].reshape(SWA_KV_HEADS, SWA_GROUP, SWA_HEAD_DIM, D_MODEL).transpose(1, 0, 2, 3)
    bf = lambda w: w.astype(BF16)
    row = lambda g: g.reshape(1, -1)
    layers = lambda g: g.reshape(g.shape[0], 1, -1)
    stacked = (layers(norm_mlp), bf(w_up), bf(w_down), layers(norm_ple), bf(w_ple_gate), bf(w_ple))
    per_head = lambda m, d: m.reshape(D_MODEL, GLA_HEADS, d)
    wh = jnp.concatenate([per_head(wq, GLA_DK), per_head(wk, GLA_DK), per_head(wv, GLA_DV),
                          per_head(wg, GLA_DV)], axis=-1).reshape(D_MODEL, GLA_HEADS * HEAD_COLS)
    return dict(
        gla=(row(norm_mix[0]), bf(wq), bf(wk), bf(wv), bf(wg), bf(wa), bf(wa2), row(b_a2[0])),
        gla_heads=(row(norm_mix[0]), bf(wh), bf(wa), bf(wa2), row(b_a2[0])),
        o_gain=row(gla_o_gain[0]),
        tail=[(bf(w_out_a[0]),) + stacked, (bf(wo_b.reshape(D_MODEL, D_MODEL)),) + stacked],
        shared=(row(norm_kv), bf(w_kv[:, :KV_COLS]), bf(w_kv[:, KV_COLS:]), row(norm_mix[1]),
                bf(wq_b.reshape(D_MODEL, D_MODEL))),
        final=row(norm_final),
    )


def kernel(x_prompt, x_sample, state_gla, cache_win_k, cache_win_v, p_prompt, p_sample, norm_mix, norm_mlp, norm_ple, norm_kv, norm_final, w_in_a, w_a2, b_a2, gla_o_gain, w_out_a, w_kv, w_q_b, w_o_b, sinks, rel_bias, w_up, w_down, w_ple, w_ple_gate):
    w = _prep_weights(norm_mix, norm_mlp, norm_ple, norm_kv, norm_final, w_in_a, w_a2, b_a2, gla_o_gain,
                      w_out_a, w_kv, w_q_b, w_o_b, w_up, w_down, w_ple, w_ple_gate)
    sink_row = sinks[0].reshape(1, SWA_HEADS)

    bp, sp, _ = x_prompt.shape
    tp = bp * sp
    xp = x_prompt.reshape(tp, D_MODEL)
    pp = p_prompt.reshape(2, tp, PLE_DIM)
    mix, state_prompt = _gla_prompt(xp, *w["gla_heads"], w["o_gain"], batch=bp, seq=sp,
                                    chunk=GLA_PROMPT_CHUNK)
    h, k_sh, v_sh, q_b = _tail0(xp, mix, pp, *w["tail"][0], *w["shared"], tm=TOKEN_TILE, q_dtype=BF16)
    y_prompt = _layer1_prompt(rel_bias, sink_row, q_b, k_sh, v_sh, h, pp, *w["tail"][1], w["final"],
                              seq=sp, qb=SWA_PROMPT_BLOCKS).reshape(bp, sp, D_MODEL)
    keep = min(WINDOW, sp)
    cache_shape = (bp, keep, SWA_KV_HEADS, SWA_HEAD_DIM)
    cache_k_prompt = k_sh.reshape(bp, sp, KV_COLS)[:, sp - keep:].reshape(cache_shape)
    cache_v_prompt = v_sh.reshape(bp, sp, KV_COLS)[:, sp - keep:].reshape(cache_shape)

    bs, ss, _ = x_sample.shape
    ts = bs * ss
    xs = x_sample.reshape(ts, D_MODEL)
    ps = p_sample.reshape(2, ts, PLE_DIM)
    n_past = cache_win_k.shape[1]
    q, k, v, g, lg = _gla_proj(xs, *w["gla"], tm=TOKEN_TILE, act_dtype=F32)
    mix, state_sample = _gla_sample(q, k, v, g, lg, w["o_gain"], state_gla[0], batch=bs, s=ss,
                                    tb=GLA_SAMPLE_BATCH)
    h, k_sh, v_sh, q_b = _tail0(xs, mix, ps, *w["tail"][0], *w["shared"], tm=TOKEN_TILE, q_dtype=F32)
    to_feature_major = lambda c: jnp.transpose(c, (0, 2, 3, 1)).reshape(bs, KV_COLS, n_past)
    from_feature_major = lambda c: jnp.transpose(
        c.reshape(bs, SWA_KV_HEADS, SWA_HEAD_DIM, n_past), (0, 3, 1, 2))
    attn, ck, cv = _swa_sample(rel_bias, sink_row, q_b, k_sh, v_sh, to_feature_major(cache_win_k),
                               to_feature_major(cache_win_v), batch=bs, s=ss, tb=SWA_SAMPLE_BATCH)
    y_sample = _tail1(h, attn, ps, *w["tail"][1], w["final"], tm=TOKEN_TILE).reshape(bs, ss, D_MODEL)

    return (y_prompt, y_sample, state_prompt[None], state_sample[None],
            cache_k_prompt, cache_v_prompt, from_feature_major(ck), from_feature_major(cv))
```

```python
import functools
import math

import numpy as np
import jax
import jax.numpy as jnp
from jax import lax
from jax.experimental import pallas as pl
from jax.experimental.pallas import tpu as pltpu

F32 = jnp.float32
BF16 = jnp.bfloat16

D_MODEL = 1024
D_FF = 4 * D_MODEL
PLE_DIM = 256
GLA_HEADS = 4
GLA_DK = 128
GLA_DV = 256
GLA_RANK = 16
GLA_TAU = 16.0
SWA_HEADS = 16
SWA_KV_HEADS = 4
SWA_GROUP = 4
SWA_HEAD_DIM = 64
WINDOW = 128
REL_BUCKETS = 32
REL_MAX_DIST = 128
EPS = 1e-6

NQ = GLA_HEADS * GLA_DK
NV = GLA_HEADS * GLA_DV
KV_COLS = SWA_KV_HEADS * SWA_HEAD_DIM
LANES = 128
RANK_PAD = LANES
NEG = -1e30
EXP_CLAMP = 80.0
LOG2E = math.log2(math.e)
VMEM_LIMIT = 62 * 1024 * 1024

TOKEN_TILE = 512
GLA_PROMPT_CHUNK = 256
GLA_SAMPLE_BATCH = 8
SWA_PROMPT_BLOCKS = 4
SWA_SAMPLE_BATCH = 16


def _rel_bucket_starts():
    max_exact = REL_BUCKETS // 2
    starts = {}
    for d in range(WINDOW):
        if d < max_exact:
            b = d
        else:
            b = max_exact + int(np.float32(
                np.log(np.float32(d) / np.float32(max_exact)) / np.float32(math.log(REL_MAX_DIST / max_exact))
                * np.float32(REL_BUCKETS - max_exact)))
            b = min(b, REL_BUCKETS - 1)
        starts.setdefault(b, d)
    return sorted((d, b) for b, d in starts.items())


BUCKET_STARTS = _rel_bucket_starts()


def _dot(a, b):
    return jnp.dot(a, b, preferred_element_type=F32)


def _dot_nt(a, b):
    return lax.dot_general(a, b, (((1,), (1,)), ((), ())), preferred_element_type=F32)


def _dot_tn(a, b):
    return lax.dot_general(a, b, (((0,), (0,)), ((), ())), preferred_element_type=F32)


def _rms(x, gain):
    return x * lax.rsqrt(jnp.mean(x * x, axis=-1, keepdims=True) + EPS) * gain


def _sigmoid(x):
    return 1.0 / (1.0 + jnp.exp(-x))


def _const_spec(shape):
    return pl.BlockSpec(shape, lambda *_: (0,) * len(shape))


def _params(n_axes, flags=None):
    return pltpu.CompilerParams(dimension_semantics=("arbitrary",) * n_axes,
                                vmem_limit_bytes=VMEM_LIMIT, flags=flags)


def _log_forget(hn, wa_ref, wa2_ref, ba2_ref):
    z = _dot(_dot(hn, wa_ref[...]).astype(BF16), wa2_ref[...]) + ba2_ref[...]
    return (jnp.minimum(z, 0.0) - jnp.log(1.0 + jnp.exp(-jnp.abs(z)))) * (1.0 / GLA_TAU)


HEAD_COLS = 2 * GLA_DK + 2 * GLA_DV


def _head_cols(h):
    c0 = h * HEAD_COLS
    return c0, c0 + GLA_DK, c0 + 2 * GLA_DK, c0 + 2 * GLA_DK + GLA_DV, c0 + HEAD_COLS


def _project_stages(hn, wh_ref, wa_ref, wa2_ref, ba2_ref, proj_ref, lg_ref):
    lg_ref[...] = _log_forget(hn, wa_ref, wa2_ref, ba2_ref)
    dt = proj_ref.dtype
    for h in range(GLA_HEADS):
        c0, ck, cv, cg, c_end = _head_cols(h)
        yield
        qk = _dot(hn, wh_ref[:, c0:cv])
        proj_ref[:, c0:ck] = (qk[:, :GLA_DK] * GLA_DK ** -0.5).astype(dt)
        proj_ref[:, ck:cv] = qk[:, GLA_DK:].astype(dt)
        yield
        proj_ref[:, cv:cg] = _dot(hn, wh_ref[:, cv:cg]).astype(dt)
        yield
        proj_ref[:, cg:c_end] = _dot(hn, wh_ref[:, cg:c_end]).astype(dt)


def _gla_proj_kernel(x_ref, gain_ref, wh_ref, wa_ref, wa2_ref, ba2_ref, proj_ref, lg_ref):
    hn = _rms(x_ref[...], gain_ref[...]).astype(BF16)
    _round_robin([_project_stages(hn, wh_ref, wa_ref, wa2_ref, ba2_ref, proj_ref, lg_ref)])


def _gla_proj(x, gain, wh, wa, wa2, ba2, *, tm):
    t = x.shape[0]
    row = lambda cols: pl.BlockSpec((tm, cols), lambda i: (i, 0))
    return pl.pallas_call(
        _gla_proj_kernel,
        grid=(t // tm,),
        in_specs=[row(D_MODEL), _const_spec((1, D_MODEL)), _const_spec((D_MODEL, GLA_HEADS * HEAD_COLS)),
                  _const_spec((D_MODEL, RANK_PAD)), _const_spec((RANK_PAD, NQ)), _const_spec((1, NQ))],
        out_specs=[row(GLA_HEADS * HEAD_COLS), row(NQ)],
        out_shape=[jax.ShapeDtypeStruct((t, GLA_HEADS * HEAD_COLS), F32),
                   jax.ShapeDtypeStruct((t, NQ), F32)],
        compiler_params=_params(1),
        name="gla_proj",
    )(x, gain, wh, wa, wa2, ba2)


def _cumsum_rows(x):
    c = x.shape[0]
    row = lax.broadcasted_iota(jnp.int32, x.shape, 0)
    shift = 1
    while shift < c:
        x = x + jnp.where(row >= shift, pltpu.roll(x, shift, axis=0), 0.0)
        shift *= 2
    return x


def _row_to_col(row):
    return jnp.transpose(jnp.broadcast_to(row, (LANES, LANES)))[:, :1]


def _round_robin(chains):
    results = [None] * len(chains)
    live = list(enumerate(chains))
    while live:
        still = []
        for idx, chain in live:
            try:
                next(chain)
                still.append((idx, chain))
            except StopIteration as done:
                results[idx] = done.value
        live = still
    return results


def _coarsen(chain, n):
    count = 0
    while True:
        try:
            next(chain)
        except StopIteration as done:
            return done.value
        count += 1
        if count % n == 0:
            yield


def _gla_chunk_stages(q, k, v, lg, state):
    c = q.shape[0]
    bc = _cumsum_rows(lg)
    last = bc[c - 1:c, :]
    mid = bc[c // 2 - 1:c // 2, :]
    out_of_range = jnp.max(jnp.maximum(bc[0:1, :] - mid, mid - last)) > EXP_CLAMP
    yield
    q_in = (q * jnp.exp(jnp.clip(bc - mid, -EXP_CLAMP, EXP_CLAMP))).astype(BF16)
    k_in = (k * jnp.exp(jnp.clip(mid - bc, -EXP_CLAMP, EXP_CLAMP))).astype(BF16)
    yield
    attn = _dot_nt(q_in, k_in)
    yield
    causal = lax.broadcasted_iota(jnp.int32, (c, c), 0) >= lax.broadcasted_iota(jnp.int32, (c, c), 1)
    attn = jnp.where(causal, attn, 0.0).astype(BF16)
    q_st = (q * jnp.exp(bc)).astype(BF16)
    yield
    o = _dot(attn, v) + _dot(q_st, state.astype(BF16))
    yield
    k_st = (k * jnp.exp(last - bc)).astype(BF16)
    decay_col = _row_to_col(jnp.exp(last))
    yield
    new_state = decay_col * state + _dot_tn(k_st, v)
    return o, new_state, out_of_range


def _gla_exact_output(q, k, v, lg, state, scratch):
    qs_ref, ks_ref, bs_ref, attn_ref = scratch
    c = q.shape[0]
    bc = _cumsum_rows(lg)
    qs_ref[...] = q
    ks_ref[...] = k
    bs_ref[...] = bc
    attn_ref[...] = jnp.zeros_like(attn_ref)
    col_id = lax.broadcasted_iota(jnp.int32, (c, c), 1)

    def key_row(j, carry):
        decay = jnp.exp(jnp.minimum(bs_ref[...] - bs_ref[pl.ds(j, 1), :], 0.0))
        col = jnp.sum(qs_ref[...] * decay * ks_ref[pl.ds(j, 1), :], axis=1, keepdims=True)
        attn_ref[...] = jnp.where(col_id == j, col, attn_ref[...])
        return carry

    lax.fori_loop(0, c, key_row, 0)
    causal = lax.broadcasted_iota(jnp.int32, (c, c), 0) >= col_id
    attn = jnp.where(causal, attn_ref[...], 0.0).astype(BF16)
    return _dot(attn, v) + _dot((q * jnp.exp(bc)).astype(BF16), state.astype(BF16))


def _exact_scratch(c):
    return [pltpu.VMEM((c, GLA_DK), F32), pltpu.VMEM((c, GLA_DK), F32), pltpu.VMEM((c, GLA_DK), F32),
            pltpu.VMEM((c, c), F32)]


def _gla_gate(o, o_gain, g):
    return _rms(o, o_gain) * (g * _sigmoid(g))


def _gla_prompt_kernel(x_ref, gain_ref, wh_ref, wa_ref, wa2_ref, ba2_ref, ogain_ref,
                       mix_ref, sfin_ref, proj_a, proj_b, lg_a, lg_b, state_ref, sprev_ref,
                       *exact_scratch, tiles_per_seq):
    i = pl.program_id(0)
    lag = jnp.maximum(i - 1, 0)

    @pl.when(i == 0)
    def _():
        proj_b[...] = jnp.zeros_like(proj_b)
        lg_b[...] = jnp.zeros_like(lg_b)

    @pl.when(lag % tiles_per_seq == 0)
    def _():
        state_ref[...] = jnp.zeros_like(state_ref)

    head_cols = _head_cols

    def project(proj_w, lg_w):
        hn = _rms(x_ref[...], gain_ref[...]).astype(BF16)
        yield
        yield from _project_stages(hn, wh_ref, wa_ref, wa2_ref, ba2_ref, proj_w, lg_w)

    def head_inputs(h, proj_r, lg_r):
        c0, ck, cv, cg, c_end = head_cols(h)
        return (proj_r[:, c0:ck].astype(F32), proj_r[:, ck:cv].astype(F32), proj_r[:, cv:cg],
                lg_r[:, h * GLA_DK:(h + 1) * GLA_DK])

    def write_mix(h, o, proj_r):
        cg, c_end = head_cols(h)[3:]
        mix_ref[:, h * GLA_DV:(h + 1) * GLA_DV] = _gla_gate(
            o, ogain_ref[...], proj_r[:, cg:c_end].astype(F32)).astype(mix_ref.dtype)

    def recur(h, proj_r, lg_r):
        state = state_ref[h]
        sprev_ref[h] = state
        o, new_state, out_of_range = yield from _gla_chunk_stages(*head_inputs(h, proj_r, lg_r), state)
        state_ref[h] = new_state
        yield
        write_mix(h, o, proj_r)
        return out_of_range

    def recur_all(proj_r, lg_r):
        flags = []
        for h in range(GLA_HEADS):
            flags.append((yield from recur(h, proj_r, lg_r)))
            yield
        return flags

    def body(proj_w, lg_w, proj_r, lg_r):
        _, flags = _round_robin([project(proj_w, lg_w), _coarsen(recur_all(proj_r, lg_r), 2)])
        for h, out_of_range in enumerate(flags):
            @pl.when(out_of_range)
            def _(h=h):
                o = _gla_exact_output(*head_inputs(h, proj_r, lg_r), sprev_ref[h], exact_scratch)
                write_mix(h, o, proj_r)

    @pl.when(i % 2 == 0)
    def _():
        body(proj_a, lg_a, proj_b, lg_b)

    @pl.when(i % 2 == 1)
    def _():
        body(proj_b, lg_b, proj_a, lg_a)

    @pl.when((lag % tiles_per_seq == tiles_per_seq - 1) & (i > 0))
    def _():
        sfin_ref[0] = state_ref[...]


def _gla_prompt(x, gain, wh, wa, wa2, ba2, o_gain, *, batch, seq, chunk):
    tps = seq // chunk
    n = batch * tps
    lag_i = lambda i: jnp.maximum(i - 1, 0)
    return pl.pallas_call(
        functools.partial(_gla_prompt_kernel, tiles_per_seq=tps),
        grid=(n + 1,),
        in_specs=[pl.BlockSpec((chunk, D_MODEL), lambda i: (jnp.minimum(i, n - 1), 0)),
                  _const_spec((1, D_MODEL)), _resident((D_MODEL, GLA_HEADS * HEAD_COLS)),
                  _const_spec((D_MODEL, RANK_PAD)), _const_spec((RANK_PAD, NQ)), _const_spec((1, NQ)),
                  _const_spec((1, GLA_DV))],
        out_specs=[pl.BlockSpec((chunk, NV), lambda i: (lag_i(i), 0)),
                   pl.BlockSpec((1, GLA_HEADS, GLA_DK, GLA_DV), lambda i: (lag_i(i) // tps, 0, 0, 0))],
        out_shape=[jax.ShapeDtypeStruct((batch * seq, NV), BF16),
                   jax.ShapeDtypeStruct((batch, GLA_HEADS, GLA_DK, GLA_DV), F32)],
        scratch_shapes=[pltpu.VMEM((chunk, GLA_HEADS * HEAD_COLS), BF16),
                        pltpu.VMEM((chunk, GLA_HEADS * HEAD_COLS), BF16),
                        pltpu.VMEM((chunk, NQ), F32), pltpu.VMEM((chunk, NQ), F32),
                        pltpu.VMEM((GLA_HEADS, GLA_DK, GLA_DV), F32),
                        pltpu.VMEM((GLA_HEADS, GLA_DK, GLA_DV), F32)] + _exact_scratch(chunk),
        compiler_params=_params(1),
        name="gla_prompt",
    )(x, gain, wh, wa, wa2, ba2, o_gain)


def _gla_sample_kernel(proj_ref, lg_ref, ogain_ref, s0_ref, mix_ref, sfin_ref, *exact_scratch, tb, s):
    def inputs(b, h):
        rows = slice(b * s, (b + 1) * s)
        c0, ck, cv, cg, _ = _head_cols(h)
        return (proj_ref[rows, c0:ck], proj_ref[rows, ck:cv], proj_ref[rows, cv:cg].astype(BF16),
                lg_ref[rows, h * GLA_DK:(h + 1) * GLA_DK], s0_ref[b, h])

    def write_mix(b, h, o):
        rows = slice(b * s, (b + 1) * s)
        cg, c_end = _head_cols(h)[3:]
        mix_ref[rows, h * GLA_DV:(h + 1) * GLA_DV] = _gla_gate(o, ogain_ref[...], proj_ref[rows, cg:c_end])

    def chain(b, h):
        o, new_state, out_of_range = yield from _gla_chunk_stages(*inputs(b, h))
        sfin_ref[b, h] = new_state
        yield
        write_mix(b, h, o)
        return out_of_range

    pairs = [(b, h) for b in range(tb) for h in range(GLA_HEADS)]
    flags = _round_robin([chain(b, h) for b, h in pairs])
    for (b, h), out_of_range in zip(pairs, flags):
        @pl.when(out_of_range)
        def _(b=b, h=h):
            write_mix(b, h, _gla_exact_output(*inputs(b, h), exact_scratch))


def _gla_sample(proj, lg, o_gain, s0, *, batch, s, tb):
    rows = lambda cols: pl.BlockSpec((tb * s, cols), lambda i: (i, 0))
    st = pl.BlockSpec((tb, GLA_HEADS, GLA_DK, GLA_DV), lambda i: (i, 0, 0, 0))
    return pl.pallas_call(
        functools.partial(_gla_sample_kernel, tb=tb, s=s),
        grid=(batch // tb,),
        in_specs=[rows(GLA_HEADS * HEAD_COLS), rows(NQ), _const_spec((1, GLA_DV)), st],
        out_specs=[rows(NV), st],
        out_shape=[jax.ShapeDtypeStruct((batch * s, NV), F32),
                   jax.ShapeDtypeStruct((batch, GLA_HEADS, GLA_DK, GLA_DV), F32)],
        scratch_shapes=_exact_scratch(s),
        compiler_params=_params(1),
        name="gla_sample",
    )(proj, lg, o_gain, s0)


def _tail_body(h_ref, mix, p_ref, wproj_ref, gmlp_ref, wup_ref, wdown_ref, gple_ref, wgate_ref,
               wple_ref, side_work=()):
    h = h_ref[...] + _dot(mix.astype(BF16), wproj_ref[...])
    hn = _rms(h, gmlp_ref[...]).astype(BF16)
    ff_chunk = D_MODEL
    for j in range(D_FF // ff_chunk):
        if j < len(side_work):
            side_work[j]()
        cols = slice(j * ff_chunk, (j + 1) * ff_chunk)
        u = jnp.square(jnp.maximum(_dot(hn, wup_ref[:, cols]), 0.0)).astype(BF16)
        h = h + _dot(u, wdown_ref[cols, :])
    gate = _sigmoid(_dot(_rms(h, gple_ref[...]).astype(BF16), wgate_ref[...]))
    return h + gate * _dot(p_ref[...].astype(BF16), wple_ref[...])


def _tail0_kernel(h_ref, mix_ref, p_ref, wproj_ref, gmlp_ref, wup_ref, wdown_ref, gple_ref, wgate_ref,
                  wple_ref, gkv_ref, wkv_ref, gq_ref, wq_ref,
                  hout_ref, kout_ref, vout_ref, qout_ref):
    h = _tail_body(h_ref, mix_ref[...], p_ref, wproj_ref, gmlp_ref, wup_ref, wdown_ref, gple_ref,
                   wgate_ref, wple_ref)
    hout_ref[...] = h
    hkv = _rms(h, gkv_ref[...]).astype(BF16)
    kout_ref[...] = _dot(hkv, wkv_ref[:, :KV_COLS])
    vout_ref[...] = _dot(hkv, wkv_ref[:, KV_COLS:])
    hq = _rms(h, gq_ref[...]).astype(BF16)
    qout_ref[...] = (_dot(hq, wq_ref[...]) * (SWA_HEAD_DIM ** -0.5 * LOG2E)).astype(qout_ref.dtype)


def _tail1_kernel(h_ref, mix_ref, p_ref, wproj_ref, gmlp_ref, wup_ref, wdown_ref, gple_ref, wgate_ref,
                  wple_ref, gfin_ref, y_ref):
    h = _tail_body(h_ref, mix_ref[...], p_ref, wproj_ref, gmlp_ref, wup_ref, wdown_ref, gple_ref,
                   wgate_ref, wple_ref)
    y_ref[...] = _rms(h, gfin_ref[...])


def _resident(shape):
    return pl.BlockSpec(shape, lambda *_: (0,) * len(shape), pipeline_mode=pl.Buffered(1))


def _layer_resident(shape, layer):
    return pl.BlockSpec((None,) + shape, lambda *_: (layer,) + (0,) * len(shape),
                        pipeline_mode=pl.Buffered(1))


def _tail_common_specs(tm, layer):
    row = lambda cols: pl.BlockSpec((tm, cols), lambda i: (i, 0))
    return [row(D_MODEL), row(D_MODEL), pl.BlockSpec((None, tm, PLE_DIM), lambda i: (layer, i, 0)),
            _resident((D_MODEL, D_MODEL)), _layer_resident((1, D_MODEL), layer),
            _layer_resident((D_MODEL, D_FF), layer), _layer_resident((D_FF, D_MODEL), layer),
            _layer_resident((1, D_MODEL), layer), _layer_resident((D_MODEL, D_MODEL), layer),
            _layer_resident((PLE_DIM, D_MODEL), layer)]


def _tail0(h, mix, p, wproj, gmlp, wup, wdown, gple, wgate, wple, gkv, wkv, gq, wq, *, tm, q_dtype):
    t = h.shape[0]
    row = lambda cols: pl.BlockSpec((tm, cols), lambda i: (i, 0))
    return pl.pallas_call(
        _tail0_kernel,
        grid=(t // tm,),
        in_specs=_tail_common_specs(tm, 0) + [
            _resident((1, D_MODEL)), _resident((D_MODEL, 2 * KV_COLS)),
            _resident((1, D_MODEL)), _resident((D_MODEL, D_MODEL))],
        out_specs=[row(D_MODEL), row(KV_COLS), row(KV_COLS), row(D_MODEL)],
        out_shape=[jax.ShapeDtypeStruct((t, D_MODEL), F32), jax.ShapeDtypeStruct((t, KV_COLS), F32),
                   jax.ShapeDtypeStruct((t, KV_COLS), F32), jax.ShapeDtypeStruct((t, D_MODEL), q_dtype)],
        compiler_params=_params(1),
        name="tail0",
    )(h, mix, p, wproj, gmlp, wup, wdown, gple, wgate, wple, gkv, wkv, gq, wq)


def _tail1(h, mix, p, wproj, gmlp, wup, wdown, gple, wgate, wple, gfin, *, tm):
    t = h.shape[0]
    return pl.pallas_call(
        _tail1_kernel,
        grid=(t // tm,),
        in_specs=_tail_common_specs(tm, 1) + [_resident((1, D_MODEL))],
        out_specs=pl.BlockSpec((tm, D_MODEL), lambda i: (i, 0)),
        out_shape=jax.ShapeDtypeStruct((t, D_MODEL), F32),
        compiler_params=_params(1),
        name="tail1",
    )(h, mix, p, wproj, gmlp, wup, wdown, gple, wgate, wple, gfin)


def _bias_table(rb_ref, head, d, valid):
    val = jnp.full(d.shape, rb_ref[BUCKET_STARTS[0][1], head], F32)
    for start, bucket in BUCKET_STARTS[1:]:
        val = jnp.where(d >= start, rb_ref[bucket, head], val)
    return jnp.where(valid, val * LOG2E, NEG)


def _softmax_sink(logits, sink):
    m = jnp.maximum(jnp.max(logits, axis=-1, keepdims=True), sink)
    e = jnp.exp2(logits - m)
    return e / (jnp.sum(e, axis=-1, keepdims=True) + jnp.exp2(sink - m))


def _lane_block(shape):
    return lax.broadcasted_iota(jnp.int32, shape, 1) // SWA_HEAD_DIM


def _swa_query_block(q_ref, r0, kblk, vblk, table, sink_ref, bias_ref):
    w = WINDOW
    rows = SWA_KV_HEADS * w
    from_prev = (lax.broadcasted_iota(jnp.int32, (rows, w), 1)
                 > (lax.broadcasted_iota(jnp.int32, (rows, w), 0) & (w - 1)))
    lane_blk = _lane_block((w, KV_COLS))
    slabs = []
    for g in range(SWA_GROUP):
        slab = q_ref[r0:r0 + w, g * KV_COLS:(g + 1) * KV_COLS].astype(F32)
        q_stack = jnp.concatenate(
            [jnp.where(lane_blk == c, slab, 0.0) for c in range(SWA_KV_HEADS)], axis=0).astype(BF16)
        logits = _dot_nt(q_stack, kblk)
        folded = jnp.where(from_prev, logits[:, :w], logits[:, w:])
        es, scales = [], []
        for c in range(SWA_KV_HEADS):
            head = c * SWA_GROUP + g
            sink = sink_ref[0, head] * LOG2E
            lg = folded[c * w:(c + 1) * w] + bias_ref[table + head]
            m = jnp.maximum(jnp.max(lg, axis=-1, keepdims=True), sink)
            e = jnp.exp2(lg - m)
            scales.append(1.0 / (jnp.sum(e, axis=-1, keepdims=True) + jnp.exp2(sink - m)))
            es.append(e)
        e_all = jnp.concatenate(es, axis=0)
        probs = jnp.concatenate([jnp.where(from_prev, e_all, 0.0).astype(BF16),
                                 jnp.where(from_prev, 0.0, e_all).astype(BF16)], axis=1)
        wide = _dot(probs, vblk)
        out = jnp.zeros((w, KV_COLS), F32)
        for c in range(SWA_KV_HEADS):
            out = jnp.where(lane_blk == c, wide[c * w:(c + 1) * w] * scales[c], out)
        slabs.append(out.astype(BF16))
    return jnp.concatenate(slabs, axis=1)


def _layer1_prompt_kernel(rb_ref, sink_ref, q_ref, kp_ref, kc_ref, vp_ref, vc_ref,
                          h_ref, p_ref, wproj_ref, gmlp_ref, wup_ref, wdown_ref, gple_ref, wgate_ref,
                          wple_ref, gfin_ref, y_ref, bias_ref, kbuf_ref, vbuf_ref, attn_ref,
                          *, qb, tiles_per_seq):
    i = pl.program_id(0)
    last = pl.num_programs(0) - 2
    w = WINDOW

    @pl.when(i == 0)
    def _():
        r = lax.broadcasted_iota(jnp.int32, (w, w), 0)
        c = lax.broadcasted_iota(jnp.int32, (w, w), 1)
        d = jnp.where(c <= r, r - c, w + r - c)
        for head in range(SWA_HEADS):
            bias_ref[head] = _bias_table(rb_ref, head, d, d >= 0)
            bias_ref[SWA_HEADS + head] = _bias_table(rb_ref, head, d, c <= r)
        attn_ref[1] = jnp.zeros(attn_ref.shape[1:], attn_ref.dtype)

    kbuf_ref[0:w] = kp_ref[...].astype(BF16)
    kbuf_ref[w:] = kc_ref[...].astype(BF16)
    vbuf_ref[0:w] = vp_ref[...].astype(BF16)
    vbuf_ref[w:] = vc_ref[...].astype(BF16)
    tile = jnp.minimum(i, last)
    first_table = jnp.where(tile % tiles_per_seq == 0, SWA_HEADS, 0)
    slot = i % 2

    def attend(jb):
        r0 = jb * w
        attn_ref[slot, r0:r0 + w, :] = _swa_query_block(
            q_ref, r0, kbuf_ref[r0:r0 + 2 * w, :], vbuf_ref[r0:r0 + 2 * w, :],
            first_table if jb == 0 else 0, sink_ref, bias_ref)

    h = _tail_body(h_ref, attn_ref[(i + 1) % 2], p_ref, wproj_ref, gmlp_ref, wup_ref, wdown_ref,
                   gple_ref, wgate_ref, wple_ref,
                   side_work=[functools.partial(attend, jb) for jb in range(qb)])
    y_ref[...] = _rms(h, gfin_ref[...])


def _layer1_prompt(rel_bias, sinks, q, k, v, h, p, wproj, gmlp, wup, wdown, gple, wgate, wple, gfin,
                   *, seq, qb):
    t = h.shape[0]
    tm = qb * WINDOW
    n = t // tm
    cur_i = lambda i: jnp.minimum(i, n - 1)
    lag_i = lambda i: jnp.maximum(i - 1, 0)
    cur = lambda cols: pl.BlockSpec((tm, cols), lambda i: (cur_i(i), 0))
    prev = lambda cols: pl.BlockSpec((WINDOW, cols), lambda i: (jnp.maximum(cur_i(i) * qb - 1, 0), 0))
    lag = lambda cols: pl.BlockSpec((tm, cols), lambda i: (lag_i(i), 0))
    smem = pl.BlockSpec(memory_space=pltpu.SMEM)
    tail_specs = _tail_common_specs(tm, 1)
    return pl.pallas_call(
        functools.partial(_layer1_prompt_kernel, qb=qb, tiles_per_seq=seq // tm),
        grid=(n + 1,),
        in_specs=[smem, smem, cur(D_MODEL), prev(KV_COLS), cur(KV_COLS), prev(KV_COLS), cur(KV_COLS),
                  lag(D_MODEL), pl.BlockSpec((None, tm, PLE_DIM), lambda i: (1, lag_i(i), 0))]
                 + tail_specs[3:] + [_resident((1, D_MODEL))],
        out_specs=lag(D_MODEL),
        out_shape=jax.ShapeDtypeStruct((t, D_MODEL), F32),
        scratch_shapes=[pltpu.VMEM((2 * SWA_HEADS, WINDOW, WINDOW), F32),
                        pltpu.VMEM(((qb + 1) * WINDOW, KV_COLS), BF16),
                        pltpu.VMEM(((qb + 1) * WINDOW, KV_COLS), BF16),
                        pltpu.VMEM((2, tm, D_MODEL), BF16)],
        compiler_params=_params(1),
        name="layer1_prompt",
    )(rel_bias, sinks, q, k, k, v, v, h, p, wproj, gmlp, wup, wdown, gple, wgate, wple, gfin)


def _swa_sample_kernel(rb_ref, sink_ref, q_ref, kn_ref, vn_ref, ck_ref, cv_ref,
                       o_ref, ok_ref, ov_ref, bias_ref, sinkcol_ref, *, tb, s, n_past):
    nk = 2 * WINDOW

    @pl.when(pl.program_id(0) == 0)
    def _():
        d = (n_past + lax.broadcasted_iota(jnp.int32, (s, nk), 0)
             - lax.broadcasted_iota(jnp.int32, (s, nk), 1))
        valid = (d >= 0) & (d < WINDOW)
        for g in range(SWA_GROUP):
            for c in range(SWA_KV_HEADS):
                head = c * SWA_GROUP + g
                r0 = (g * SWA_KV_HEADS + c) * s
                bias_ref[r0:r0 + s, :] = _bias_table(rb_ref, head, d, valid)
                sinkcol_ref[r0:r0 + s, :] = jnp.full((s, LANES), sink_ref[0, head] * LOG2E, F32)

    lane_blk = _lane_block((s, KV_COLS))
    zero_rows = jnp.zeros((WINDOW - s, KV_COLS), F32)
    kn_t = jnp.transpose(kn_ref[...])
    vn_t = jnp.transpose(vn_ref[...])
    keep_old = lax.broadcasted_iota(jnp.int32, (KV_COLS, n_past), 1) < n_past - s

    def chain(b):
        r0 = b * s
        new_shift = (n_past - s - r0) % LANES
        place = (lambda x: pltpu.roll(x, new_shift, axis=1)) if new_shift else (lambda x: x)
        ok_ref[b] = jnp.where(keep_old, pltpu.roll(ck_ref[b], n_past - s, axis=1), place(kn_t))
        ov_ref[b] = jnp.where(keep_old, pltpu.roll(cv_ref[b], n_past - s, axis=1), place(vn_t))
        yield
        k_new = jnp.concatenate([kn_ref[r0:r0 + s, :], zero_rows], axis=0).astype(BF16)
        v_new = jnp.concatenate([vn_ref[r0:r0 + s, :], zero_rows], axis=0).astype(BF16)
        pieces = []
        for g in range(SWA_GROUP):
            slab = q_ref[r0:r0 + s, g * KV_COLS:(g + 1) * KV_COLS]
            pieces += [jnp.where(lane_blk == c, slab, 0.0) for c in range(SWA_KV_HEADS)]
        q_stack = jnp.concatenate(pieces, axis=0).astype(BF16)
        yield
        logits = jnp.concatenate([_dot(q_stack, ck_ref[b].astype(BF16)), _dot_nt(q_stack, k_new)],
                                 axis=1) + bias_ref[...]
        yield
        probs = _softmax_sink(logits, sinkcol_ref[:, :1]).astype(BF16)
        yield
        wide = _dot_nt(probs[:, :n_past], cv_ref[b].astype(BF16)) + _dot(probs[:, n_past:], v_new)
        yield
        for g in range(SWA_GROUP):
            out = jnp.zeros((s, KV_COLS), F32)
            for c in range(SWA_KV_HEADS):
                w0 = (g * SWA_KV_HEADS + c) * s
                out = jnp.where(lane_blk == c, wide[w0:w0 + s], out)
            o_ref[r0:r0 + s, g * KV_COLS:(g + 1) * KV_COLS] = out

    _round_robin([chain(b) for b in range(tb)])


def _swa_sample(rel_bias, sinks, q, k_new, v_new, cache_k, cache_v, *, batch, s, tb):
    n_past = cache_k.shape[2]
    assert n_past == WINDOW and tb * s == LANES
    rows = lambda cols: pl.BlockSpec((tb * s, cols), lambda i: (i, 0))
    cache = pl.BlockSpec((tb, KV_COLS, n_past), lambda i: (i, 0, 0))
    smem = pl.BlockSpec(memory_space=pltpu.SMEM)
    return pl.pallas_call(
        functools.partial(_swa_sample_kernel, tb=tb, s=s, n_past=n_past),
        grid=(batch // tb,),
        in_specs=[smem, smem, rows(D_MODEL), rows(KV_COLS), rows(KV_COLS), cache, cache],
        out_specs=[rows(D_MODEL), cache, cache],
        out_shape=[jax.ShapeDtypeStruct((batch * s, D_MODEL), F32),
                   jax.ShapeDtypeStruct(cache_k.shape, F32), jax.ShapeDtypeStruct(cache_v.shape, F32)],
        scratch_shapes=[pltpu.VMEM((SWA_HEADS * s, 2 * WINDOW), F32),
                        pltpu.VMEM((SWA_HEADS * s, LANES), F32)],
        compiler_params=_params(1),
        name="swa_sample",
    )(rel_bias, sinks, q, k_new, v_new, cache_k, cache_v)


def _prep_weights(norm_mix, norm_mlp, norm_ple, norm_kv, norm_final, w_in_a, w_a2, b_a2, gla_o_gain,
                  w_out_a, w_kv, w_q_b, w_o_b, w_up, w_down, w_ple, w_ple_gate):
    w_in = w_in_a[0]
    wq, wk, wv, wg, wa = jnp.split(w_in, [NQ, 2 * NQ, 2 * NQ + NV, 2 * NQ + 2 * NV], axis=-1)
    wa = jnp.pad(wa, ((0, 0), (0, RANK_PAD - GLA_RANK)))
    wa2 = jnp.pad(w_a2[0], ((0, RANK_PAD - GLA_RANK), (0, 0)))
    wq_b = w_q_b[0].reshape(D_MODEL, SWA_KV_HEADS, SWA_GROUP, SWA_HEAD_DIM).transpose(0, 2, 1, 3)
    wo_b = w_o_b[0].reshape(SWA_KV_HEADS, SWA_GROUP, SWA_HEAD_DIM, D_MODEL).transpose(1, 0, 2, 3)
    bf = lambda w: w.astype(BF16)
    row = lambda g: g.reshape(1, -1)
    layers = lambda g: g.reshape(g.shape[0], 1, -1)
    stacked = (layers(norm_mlp), bf(w_up), bf(w_down), layers(norm_ple), bf(w_ple_gate), bf(w_ple))
    per_head = lambda m, d: m.reshape(D_MODEL, GLA_HEADS, d)
    wh = jnp.concatenate([per_head(wq, GLA_DK), per_head(wk, GLA_DK), per_head(wv, GLA_DV),
                          per_head(wg, GLA_DV)], axis=-1).reshape(D_MODEL, GLA_HEADS * HEAD_COLS)
    return dict(
        gla_heads=(row(norm_mix[0]), bf(wh), bf(wa), bf(wa2), row(b_a2[0])),
        o_gain=row(gla_o_gain[0]),
        tail=[(bf(w_out_a[0]),) + stacked, (bf(wo_b.reshape(D_MODEL, D_MODEL)),) + stacked],
        shared=(row(norm_kv), bf(w_kv), row(norm_mix[1]),
                bf(wq_b.reshape(D_MODEL, D_MODEL))),
        final=row(norm_final),
    )


def kernel(x_prompt, x_sample, state_gla, cache_win_k, cache_win_v, p_prompt, p_sample, norm_mix, norm_mlp, norm_ple, norm_kv, norm_final, w_in_a, w_a2, b_a2, gla_o_gain, w_out_a, w_kv, w_q_b, w_o_b, sinks, rel_bias, w_up, w_down, w_ple, w_ple_gate):
    w = _prep_weights(norm_mix, norm_mlp, norm_ple, norm_kv, norm_final, w_in_a, w_a2, b_a2, gla_o_gain,
                      w_out_a, w_kv, w_q_b, w_o_b, w_up, w_down, w_ple, w_ple_gate)
    sink_row = sinks[0].reshape(1, SWA_HEADS)

    bp, sp, _ = x_prompt.shape
    tp = bp * sp
    xp = x_prompt.reshape(tp, D_MODEL)
    pp = p_prompt.reshape(2, tp, PLE_DIM)
    mix, state_prompt = _gla_prompt(xp, *w["gla_heads"], w["o_gain"], batch=bp, seq=sp,
                                    chunk=GLA_PROMPT_CHUNK)
    h, k_sh, v_sh, q_b = _tail0(xp, mix, pp, *w["tail"][0], *w["shared"], tm=TOKEN_TILE, q_dtype=BF16)
    y_prompt = _layer1_prompt(rel_bias, sink_row, q_b, k_sh, v_sh, h, pp, *w["tail"][1], w["final"],
                              seq=sp, qb=SWA_PROMPT_BLOCKS).reshape(bp, sp, D_MODEL)
    keep = min(WINDOW, sp)
    cache_shape = (bp, keep, SWA_KV_HEADS, SWA_HEAD_DIM)
    cache_k_prompt = k_sh.reshape(bp, sp, KV_COLS)[:, sp - keep:].reshape(cache_shape)
    cache_v_prompt = v_sh.reshape(bp, sp, KV_COLS)[:, sp - keep:].reshape(cache_shape)

    bs, ss, _ = x_sample.shape
    ts = bs * ss
    xs = x_sample.reshape(ts, D_MODEL)
    ps = p_sample.reshape(2, ts, PLE_DIM)
    n_past = cache_win_k.shape[1]
    proj, lg = _gla_proj(xs, *w["gla_heads"], tm=TOKEN_TILE)
    mix, state_sample = _gla_sample(proj, lg, w["o_gain"], state_gla[0], batch=bs, s=ss,
                                    tb=GLA_SAMPLE_BATCH)
    h, k_sh, v_sh, q_b = _tail0(xs, mix, ps, *w["tail"][0], *w["shared"], tm=TOKEN_TILE, q_dtype=F32)
    to_feature_major = lambda c: jnp.transpose(c, (0, 2, 3, 1)).reshape(bs, KV_COLS, n_past)
    from_feature_major = lambda c: jnp.transpose(
        c.reshape(bs, SWA_KV_HEADS, SWA_HEAD_DIM, n_past), (0, 3, 1, 2))
    attn, ck, cv = _swa_sample(rel_bias, sink_row, q_b, k_sh, v_sh, to_feature_major(cache_win_k),
                               to_feature_major(cache_win_v), batch=bs, s=ss, tb=SWA_SAMPLE_BATCH)
    y_sample = _tail1(h, attn, ps, *w["tail"][1], w["final"], tm=TOKEN_TILE).reshape(bs, ss, D_MODEL)

    return (y_prompt, y_sample, state_prompt[None], state_sample[None],
            cache_k_prompt, cache_v_prompt, from_feature_major(ck), from_feature_major(cv))
```

```python
import functools
import math

import numpy as np
import jax
import jax.numpy as jnp
from jax import lax
from jax.experimental import pallas as pl
from jax.experimental.pallas import tpu as pltpu

F32 = jnp.float32
BF16 = jnp.bfloat16

D_MODEL = 1024
D_FF = 4 * D_MODEL
PLE_DIM = 256
GLA_HEADS = 4
GLA_DK = 128
GLA_DV = 256
GLA_RANK = 16
GLA_TAU = 16.0
SWA_HEADS = 16
SWA_KV_HEADS = 4
SWA_GROUP = 4
SWA_HEAD_DIM = 64
WINDOW = 128
REL_BUCKETS = 32
REL_MAX_DIST = 128
EPS = 1e-6

NQ = GLA_HEADS * GLA_DK
NV = GLA_HEADS * GLA_DV
KV_COLS = SWA_KV_HEADS * SWA_HEAD_DIM
LANES = 128
RANK_PAD = LANES
NEG = -1e30
EXP_CLAMP = 80.0
LOG2E = math.log2(math.e)
VMEM_LIMIT = 62 * 1024 * 1024

TOKEN_TILE = 512
GLA_PROMPT_CHUNK = 256
GLA_SAMPLE_BATCH = 8
SWA_PROMPT_BLOCKS = 4
SWA_SAMPLE_BATCH = 16


def _rel_bucket_starts():
    max_exact = REL_BUCKETS // 2
    starts = {}
    for d in range(WINDOW):
        if d < max_exact:
            b = d
        else:
            b = max_exact + int(np.float32(
                np.log(np.float32(d) / np.float32(max_exact)) / np.float32(math.log(REL_MAX_DIST / max_exact))
                * np.float32(REL_BUCKETS - max_exact)))
            b = min(b, REL_BUCKETS - 1)
        starts.setdefault(b, d)
    return sorted((d, b) for b, d in starts.items())


BUCKET_STARTS = _rel_bucket_starts()


def _dot(a, b):
    return jnp.dot(a, b, preferred_element_type=F32)


def _dot_nt(a, b):
    return lax.dot_general(a, b, (((1,), (1,)), ((), ())), preferred_element_type=F32)


def _dot_tn(a, b):
    return lax.dot_general(a, b, (((0,), (0,)), ((), ())), preferred_element_type=F32)


def _rms(x, gain):
    return x * lax.rsqrt(jnp.mean(x * x, axis=-1, keepdims=True) + EPS) * gain


def _sigmoid(x):
    return 1.0 / (1.0 + jnp.exp(-x))


def _const_spec(shape):
    return pl.BlockSpec(shape, lambda *_: (0,) * len(shape))


def _params(n_axes, flags=None):
    return pltpu.CompilerParams(dimension_semantics=("arbitrary",) * n_axes,
                                vmem_limit_bytes=VMEM_LIMIT, flags=flags)


def _log_forget(hn, wa_ref, wa2_ref, ba2_ref):
    z = _dot(_dot(hn, wa_ref[...]).astype(BF16), wa2_ref[...]) + ba2_ref[...]
    return (jnp.minimum(z, 0.0) - jnp.log(1.0 + jnp.exp(-jnp.abs(z)))) * (1.0 / GLA_TAU)


HEAD_COLS = 2 * GLA_DK + 2 * GLA_DV


def _head_cols(h):
    c0 = h * HEAD_COLS
    return c0, c0 + GLA_DK, c0 + 2 * GLA_DK, c0 + 2 * GLA_DK + GLA_DV, c0 + HEAD_COLS


def _project_stages(hn, wh_ref, wa_ref, wa2_ref, ba2_ref, proj_ref, lg_ref):
    lg_ref[...] = _log_forget(hn, wa_ref, wa2_ref, ba2_ref)
    dt = proj_ref.dtype
    for h in range(GLA_HEADS):
        c0, ck, cv, cg, c_end = _head_cols(h)
        yield
        qk = _dot(hn, wh_ref[:, c0:cv])
        proj_ref[:, c0:ck] = (qk[:, :GLA_DK] * GLA_DK ** -0.5).astype(dt)
        proj_ref[:, ck:cv] = qk[:, GLA_DK:].astype(dt)
        yield
        proj_ref[:, cv:cg] = _dot(hn, wh_ref[:, cv:cg]).astype(dt)
        yield
        proj_ref[:, cg:c_end] = _dot(hn, wh_ref[:, cg:c_end]).astype(dt)


def _gla_proj_kernel(x_ref, gain_ref, wh_ref, wa_ref, wa2_ref, ba2_ref, proj_ref, lg_ref):
    hn = _rms(x_ref[...], gain_ref[...]).astype(BF16)
    _round_robin([_project_stages(hn, wh_ref, wa_ref, wa2_ref, ba2_ref, proj_ref, lg_ref)])


def _gla_proj(x, gain, wh, wa, wa2, ba2, *, tm):
    t = x.shape[0]
    row = lambda cols: pl.BlockSpec((tm, cols), lambda i: (i, 0))
    return pl.pallas_call(
        _gla_proj_kernel,
        grid=(t // tm,),
        in_specs=[row(D_MODEL), _const_spec((1, D_MODEL)), _const_spec((D_MODEL, GLA_HEADS * HEAD_COLS)),
                  _const_spec((D_MODEL, RANK_PAD)), _const_spec((RANK_PAD, NQ)), _const_spec((1, NQ))],
        out_specs=[row(GLA_HEADS * HEAD_COLS), row(NQ)],
        out_shape=[jax.ShapeDtypeStruct((t, GLA_HEADS * HEAD_COLS), F32),
                   jax.ShapeDtypeStruct((t, NQ), F32)],
        compiler_params=_params(1),
        name="gla_proj",
    )(x, gain, wh, wa, wa2, ba2)


def _cumsum_rows(x):
    c = x.shape[0]
    row = lax.broadcasted_iota(jnp.int32, x.shape, 0)
    shift = 1
    while shift < c:
        x = x + jnp.where(row >= shift, pltpu.roll(x, shift, axis=0), 0.0)
        shift *= 2
    return x


def _row_to_col(row):
    return jnp.transpose(jnp.broadcast_to(row, (LANES, LANES)))[:, :1]


def _round_robin(chains):
    results = [None] * len(chains)
    live = list(enumerate(chains))
    while live:
        still = []
        for idx, chain in live:
            try:
                next(chain)
                still.append((idx, chain))
            except StopIteration as done:
                results[idx] = done.value
        live = still
    return results


def _coarsen(chain, n):
    count = 0
    while True:
        try:
            next(chain)
        except StopIteration as done:
            return done.value
        count += 1
        if count % n == 0:
            yield


def _gla_chunk_stages(q, k, v, lg, state):
    c = q.shape[0]
    bc = _cumsum_rows(lg)
    last = bc[c - 1:c, :]
    mid = bc[c // 2 - 1:c // 2, :]
    out_of_range = jnp.max(jnp.maximum(bc[0:1, :] - mid, mid - last)) > EXP_CLAMP
    yield
    q_in = (q * jnp.exp(jnp.clip(bc - mid, -EXP_CLAMP, EXP_CLAMP))).astype(BF16)
    k_in = (k * jnp.exp(jnp.clip(mid - bc, -EXP_CLAMP, EXP_CLAMP))).astype(BF16)
    yield
    attn = _dot_nt(q_in, k_in)
    yield
    causal = lax.broadcasted_iota(jnp.int32, (c, c), 0) >= lax.broadcasted_iota(jnp.int32, (c, c), 1)
    attn = jnp.where(causal, attn, 0.0).astype(BF16)
    q_st = (q * jnp.exp(bc)).astype(BF16)
    yield
    o = _dot(attn, v) + _dot(q_st, state.astype(BF16))
    yield
    k_st = (k * jnp.exp(last - bc)).astype(BF16)
    decay_col = _row_to_col(jnp.exp(last))
    yield
    new_state = decay_col * state + _dot_tn(k_st, v)
    return o, new_state, out_of_range


def _gla_exact_output(q, k, v, lg, state, scratch):
    qs_ref, ks_ref, bs_ref, attn_ref = scratch
    c = q.shape[0]
    bc = _cumsum_rows(lg)
    qs_ref[...] = q
    ks_ref[...] = k
    bs_ref[...] = bc
    attn_ref[...] = jnp.zeros_like(attn_ref)
    col_id = lax.broadcasted_iota(jnp.int32, (c, c), 1)

    def key_row(j, carry):
        decay = jnp.exp(jnp.minimum(bs_ref[...] - bs_ref[pl.ds(j, 1), :], 0.0))
        col = jnp.sum(qs_ref[...] * decay * ks_ref[pl.ds(j, 1), :], axis=1, keepdims=True)
        attn_ref[...] = jnp.where(col_id == j, col, attn_ref[...])
        return carry

    lax.fori_loop(0, c, key_row, 0)
    causal = lax.broadcasted_iota(jnp.int32, (c, c), 0) >= col_id
    attn = jnp.where(causal, attn_ref[...], 0.0).astype(BF16)
    return _dot(attn, v) + _dot((q * jnp.exp(bc)).astype(BF16), state.astype(BF16))


def _exact_scratch(c):
    return [pltpu.VMEM((c, GLA_DK), F32), pltpu.VMEM((c, GLA_DK), F32), pltpu.VMEM((c, GLA_DK), F32),
            pltpu.VMEM((c, c), F32)]


def _gla_gate(o, o_gain, g):
    return _rms(o, o_gain) * (g * _sigmoid(g))


def _gla_prompt_kernel(x_ref, gain_ref, wh_ref, wa_ref, wa2_ref, ba2_ref, ogain_ref, wup_ref, wdown_ref,
                       mix_ref, sfin_ref, wup_bf_ref, wdown_bf_ref,
                       proj_a, proj_b, lg_a, lg_b, state_ref, sprev_ref, *exact_scratch, tiles_per_seq):
    i = pl.program_id(0)
    lag = jnp.maximum(i - 1, 0)

    @pl.when(i == 0)
    def _():
        proj_b[...] = jnp.zeros_like(proj_b)
        lg_b[...] = jnp.zeros_like(lg_b)

    @pl.when(lag % tiles_per_seq == 0)
    def _():
        state_ref[...] = jnp.zeros_like(state_ref)

    head_cols = _head_cols

    def project(proj_w, lg_w):
        hn = _rms(x_ref[...], gain_ref[...]).astype(BF16)
        yield
        yield from _project_stages(hn, wh_ref, wa_ref, wa2_ref, ba2_ref, proj_w, lg_w)

    def head_inputs(h, proj_r, lg_r):
        c0, ck, cv, cg, c_end = head_cols(h)
        return (proj_r[:, c0:ck].astype(F32), proj_r[:, ck:cv].astype(F32), proj_r[:, cv:cg],
                lg_r[:, h * GLA_DK:(h + 1) * GLA_DK])

    def write_mix(h, o, proj_r):
        cg, c_end = head_cols(h)[3:]
        mix_ref[:, h * GLA_DV:(h + 1) * GLA_DV] = _gla_gate(
            o, ogain_ref[...], proj_r[:, cg:c_end].astype(F32)).astype(mix_ref.dtype)

    def recur(h, proj_r, lg_r):
        state = state_ref[h]
        sprev_ref[h] = state
        o, new_state, out_of_range = yield from _gla_chunk_stages(*head_inputs(h, proj_r, lg_r), state)
        state_ref[h] = new_state
        yield
        write_mix(h, o, proj_r)
        return out_of_range

    def recur_all(proj_r, lg_r):
        flags = []
        for h in range(GLA_HEADS):
            flags.append((yield from recur(h, proj_r, lg_r)))
            yield
        return flags

    def cast_mlp_weights():
        wup_bf_ref[...] = wup_ref[...].astype(BF16)
        yield
        wdown_bf_ref[...] = wdown_ref[...].astype(BF16)

    def body(proj_w, lg_w, proj_r, lg_r):
        _, flags, _ = _round_robin([project(proj_w, lg_w), _coarsen(recur_all(proj_r, lg_r), 2),
                                    cast_mlp_weights()])
        for h, out_of_range in enumerate(flags):
            @pl.when(out_of_range)
            def _(h=h):
                o = _gla_exact_output(*head_inputs(h, proj_r, lg_r), sprev_ref[h], exact_scratch)
                write_mix(h, o, proj_r)

    @pl.when(i % 2 == 0)
    def _():
        body(proj_a, lg_a, proj_b, lg_b)

    @pl.when(i % 2 == 1)
    def _():
        body(proj_b, lg_b, proj_a, lg_a)

    @pl.when((lag % tiles_per_seq == tiles_per_seq - 1) & (i > 0))
    def _():
        sfin_ref[0] = state_ref[...]


def _gla_prompt(x, gain, wh, wa, wa2, ba2, o_gain, w_up, w_down, *, batch, seq, chunk):
    tps = seq // chunk
    n = batch * tps
    lag_i = lambda i: jnp.maximum(i - 1, 0)
    cur_i = lambda i: jnp.minimum(i, n - 1)
    assert w_up.shape[0] % (16 * n) == 0 and w_down.shape[0] % (16 * n) == 0
    w_slice = lambda w: pl.BlockSpec((w.shape[0] // n, w.shape[1]), lambda i: (cur_i(i), 0))
    return pl.pallas_call(
        functools.partial(_gla_prompt_kernel, tiles_per_seq=tps),
        grid=(n + 1,),
        in_specs=[pl.BlockSpec((chunk, D_MODEL), lambda i: (cur_i(i), 0)),
                  _const_spec((1, D_MODEL)), _resident((D_MODEL, GLA_HEADS * HEAD_COLS)),
                  _const_spec((D_MODEL, RANK_PAD)), _const_spec((RANK_PAD, NQ)), _const_spec((1, NQ)),
                  _const_spec((1, GLA_DV)), w_slice(w_up), w_slice(w_down)],
        out_specs=[pl.BlockSpec((chunk, NV), lambda i: (lag_i(i), 0)),
                   pl.BlockSpec((1, GLA_HEADS, GLA_DK, GLA_DV), lambda i: (lag_i(i) // tps, 0, 0, 0)),
                   w_slice(w_up), w_slice(w_down)],
        out_shape=[jax.ShapeDtypeStruct((batch * seq, NV), BF16),
                   jax.ShapeDtypeStruct((batch, GLA_HEADS, GLA_DK, GLA_DV), F32),
                   jax.ShapeDtypeStruct(w_up.shape, BF16), jax.ShapeDtypeStruct(w_down.shape, BF16)],
        scratch_shapes=[pltpu.VMEM((chunk, GLA_HEADS * HEAD_COLS), BF16),
                        pltpu.VMEM((chunk, GLA_HEADS * HEAD_COLS), BF16),
                        pltpu.VMEM((chunk, NQ), F32), pltpu.VMEM((chunk, NQ), F32),
                        pltpu.VMEM((GLA_HEADS, GLA_DK, GLA_DV), F32),
                        pltpu.VMEM((GLA_HEADS, GLA_DK, GLA_DV), F32)] + _exact_scratch(chunk),
        compiler_params=_params(1),
        name="gla_prompt",
    )(x, gain, wh, wa, wa2, ba2, o_gain, w_up, w_down)


def _gla_sample_kernel(proj_ref, lg_ref, ogain_ref, s0_ref, mix_ref, sfin_ref, *exact_scratch, tb, s):
    def inputs(b, h):
        rows = slice(b * s, (b + 1) * s)
        c0, ck, cv, cg, _ = _head_cols(h)
        return (proj_ref[rows, c0:ck], proj_ref[rows, ck:cv], proj_ref[rows, cv:cg].astype(BF16),
                lg_ref[rows, h * GLA_DK:(h + 1) * GLA_DK], s0_ref[b, h])

    def write_mix(b, h, o):
        rows = slice(b * s, (b + 1) * s)
        cg, c_end = _head_cols(h)[3:]
        mix_ref[rows, h * GLA_DV:(h + 1) * GLA_DV] = _gla_gate(o, ogain_ref[...], proj_ref[rows, cg:c_end])

    def chain(b, h):
        o, new_state, out_of_range = yield from _gla_chunk_stages(*inputs(b, h))
        sfin_ref[b, h] = new_state
        yield
        write_mix(b, h, o)
        return out_of_range

    pairs = [(b, h) for b in range(tb) for h in range(GLA_HEADS)]
    flags = _round_robin([chain(b, h) for b, h in pairs])
    for (b, h), out_of_range in zip(pairs, flags):
        @pl.when(out_of_range)
        def _(b=b, h=h):
            write_mix(b, h, _gla_exact_output(*inputs(b, h), exact_scratch))


def _gla_sample(proj, lg, o_gain, s0, *, batch, s, tb):
    rows = lambda cols: pl.BlockSpec((tb * s, cols), lambda i: (i, 0))
    st = pl.BlockSpec((tb, GLA_HEADS, GLA_DK, GLA_DV), lambda i: (i, 0, 0, 0))
    return pl.pallas_call(
        functools.partial(_gla_sample_kernel, tb=tb, s=s),
        grid=(batch // tb,),
        in_specs=[rows(GLA_HEADS * HEAD_COLS), rows(NQ), _const_spec((1, GLA_DV)), st],
        out_specs=[rows(NV), st],
        out_shape=[jax.ShapeDtypeStruct((batch * s, NV), F32),
                   jax.ShapeDtypeStruct((batch, GLA_HEADS, GLA_DK, GLA_DV), F32)],
        scratch_shapes=_exact_scratch(s),
        compiler_params=_params(1),
        name="gla_sample",
    )(proj, lg, o_gain, s0)


def _tail_body(h_ref, mix, p_ref, wproj_ref, gmlp_ref, wup_ref, wdown_ref, gple_ref, wgate_ref,
               wple_ref, side_work=()):
    h = h_ref[...] + _dot(mix.astype(BF16), wproj_ref[...])
    hn = _rms(h, gmlp_ref[...]).astype(BF16)
    ff_chunk = D_MODEL
    for j in range(D_FF // ff_chunk):
        if j < len(side_work):
            side_work[j]()
        cols = slice(j * ff_chunk, (j + 1) * ff_chunk)
        u = jnp.square(jnp.maximum(_dot(hn, wup_ref[:, cols]), 0.0)).astype(BF16)
        h = h + _dot(u, wdown_ref[cols, :])
    gate = _sigmoid(_dot(_rms(h, gple_ref[...]).astype(BF16), wgate_ref[...]))
    return h + gate * _dot(p_ref[...].astype(BF16), wple_ref[...])


def _tail0_kernel(h_ref, mix_ref, p_ref, wproj_ref, gmlp_ref, wup_ref, wdown_ref, gple_ref, wgate_ref,
                  wple_ref, gkv_ref, wkv_ref, gq_ref, wq_ref,
                  hout_ref, kout_ref, vout_ref, qout_ref):
    h = _tail_body(h_ref, mix_ref[...], p_ref, wproj_ref, gmlp_ref, wup_ref, wdown_ref, gple_ref,
                   wgate_ref, wple_ref)
    hout_ref[...] = h
    hkv = _rms(h, gkv_ref[...]).astype(BF16)
    kout_ref[...] = _dot(hkv, wkv_ref[:, :KV_COLS])
    vout_ref[...] = _dot(hkv, wkv_ref[:, KV_COLS:])
    hq = _rms(h, gq_ref[...]).astype(BF16)
    qout_ref[...] = (_dot(hq, wq_ref[...]) * (SWA_HEAD_DIM ** -0.5 * LOG2E)).astype(qout_ref.dtype)


def _tail1_kernel(h_ref, mix_ref, p_ref, wproj_ref, gmlp_ref, wup_ref, wdown_ref, gple_ref, wgate_ref,
                  wple_ref, gfin_ref, y_ref):
    h = _tail_body(h_ref, mix_ref[...], p_ref, wproj_ref, gmlp_ref, wup_ref, wdown_ref, gple_ref,
                   wgate_ref, wple_ref)
    y_ref[...] = _rms(h, gfin_ref[...])


def _resident(shape):
    return pl.BlockSpec(shape, lambda *_: (0,) * len(shape), pipeline_mode=pl.Buffered(1))


def _layer_resident(shape, layer):
    return pl.BlockSpec((None,) + shape, lambda *_: (layer,) + (0,) * len(shape),
                        pipeline_mode=pl.Buffered(1))


def _tail_common_specs(tm, layer):
    row = lambda cols: pl.BlockSpec((tm, cols), lambda i: (i, 0))
    return [row(D_MODEL), row(D_MODEL), pl.BlockSpec((None, tm, PLE_DIM), lambda i: (layer, i, 0)),
            _resident((D_MODEL, D_MODEL)), _layer_resident((1, D_MODEL), layer),
            _layer_resident((D_MODEL, D_FF), layer), _layer_resident((D_FF, D_MODEL), layer),
            _layer_resident((1, D_MODEL), layer), _layer_resident((D_MODEL, D_MODEL), layer),
            _layer_resident((PLE_DIM, D_MODEL), layer)]


def _tail0(h, mix, p, wproj, gmlp, wup, wdown, gple, wgate, wple, gkv, wkv, gq, wq, *, tm, q_dtype):
    t = h.shape[0]
    row = lambda cols: pl.BlockSpec((tm, cols), lambda i: (i, 0))
    return pl.pallas_call(
        _tail0_kernel,
        grid=(t // tm,),
        in_specs=_tail_common_specs(tm, 0) + [
            _resident((1, D_MODEL)), _resident((D_MODEL, 2 * KV_COLS)),
            _resident((1, D_MODEL)), _resident((D_MODEL, D_MODEL))],
        out_specs=[row(D_MODEL), row(KV_COLS), row(KV_COLS), row(D_MODEL)],
        out_shape=[jax.ShapeDtypeStruct((t, D_MODEL), F32), jax.ShapeDtypeStruct((t, KV_COLS), F32),
                   jax.ShapeDtypeStruct((t, KV_COLS), F32), jax.ShapeDtypeStruct((t, D_MODEL), q_dtype)],
        compiler_params=_params(1),
        name="tail0",
    )(h, mix, p, wproj, gmlp, wup, wdown, gple, wgate, wple, gkv, wkv, gq, wq)


def _tail1(h, mix, p, wproj, gmlp, wup, wdown, gple, wgate, wple, gfin, *, tm):
    t = h.shape[0]
    return pl.pallas_call(
        _tail1_kernel,
        grid=(t // tm,),
        in_specs=_tail_common_specs(tm, 1) + [_resident((1, D_MODEL))],
        out_specs=pl.BlockSpec((tm, D_MODEL), lambda i: (i, 0)),
        out_shape=jax.ShapeDtypeStruct((t, D_MODEL), F32),
        compiler_params=_params(1),
        name="tail1",
    )(h, mix, p, wproj, gmlp, wup, wdown, gple, wgate, wple, gfin)


def _bias_table(rb_ref, head, d, valid):
    val = jnp.full(d.shape, rb_ref[BUCKET_STARTS[0][1], head], F32)
    for start, bucket in BUCKET_STARTS[1:]:
        val = jnp.where(d >= start, rb_ref[bucket, head], val)
    return jnp.where(valid, val * LOG2E, NEG)


def _softmax_sink(logits, sink):
    m = jnp.maximum(jnp.max(logits, axis=-1, keepdims=True), sink)
    e = jnp.exp2(logits - m)
    return e / (jnp.sum(e, axis=-1, keepdims=True) + jnp.exp2(sink - m))


def _lane_block(shape):
    return lax.broadcasted_iota(jnp.int32, shape, 1) // SWA_HEAD_DIM


def _swa_query_block(q_ref, r0, kblk, vblk, table, sink_ref, bias_ref):
    w = WINDOW
    rows = SWA_KV_HEADS * w
    from_prev = (lax.broadcasted_iota(jnp.int32, (rows, w), 1)
                 > (lax.broadcasted_iota(jnp.int32, (rows, w), 0) & (w - 1)))
    lane_blk = _lane_block((w, KV_COLS))
    slabs = []
    for g in range(SWA_GROUP):
        slab = q_ref[r0:r0 + w, g * KV_COLS:(g + 1) * KV_COLS].astype(F32)
        q_stack = jnp.concatenate(
            [jnp.where(lane_blk == c, slab, 0.0) for c in range(SWA_KV_HEADS)], axis=0).astype(BF16)
        logits = _dot_nt(q_stack, kblk)
        folded = jnp.where(from_prev, logits[:, :w], logits[:, w:])
        es, scales = [], []
        for c in range(SWA_KV_HEADS):
            head = c * SWA_GROUP + g
            sink = sink_ref[0, head] * LOG2E
            lg = folded[c * w:(c + 1) * w] + bias_ref[table + head]
            m = jnp.maximum(jnp.max(lg, axis=-1, keepdims=True), sink)
            e = jnp.exp2(lg - m)
            scales.append(1.0 / (jnp.sum(e, axis=-1, keepdims=True) + jnp.exp2(sink - m)))
            es.append(e)
        e_all = jnp.concatenate(es, axis=0)
        probs = jnp.concatenate([jnp.where(from_prev, e_all, 0.0).astype(BF16),
                                 jnp.where(from_prev, 0.0, e_all).astype(BF16)], axis=1)
        wide = _dot(probs, vblk)
        out = jnp.zeros((w, KV_COLS), F32)
        for c in range(SWA_KV_HEADS):
            out = jnp.where(lane_blk == c, wide[c * w:(c + 1) * w] * scales[c], out)
        slabs.append(out.astype(BF16))
    return jnp.concatenate(slabs, axis=1)


def _layer1_prompt_kernel(rb_ref, sink_ref, q_ref, kp_ref, kc_ref, vp_ref, vc_ref,
                          h_ref, p_ref, wproj_ref, gmlp_ref, wup_ref, wdown_ref, gple_ref, wgate_ref,
                          wple_ref, gfin_ref, y_ref, bias_ref, kbuf_ref, vbuf_ref, attn_ref,
                          *, qb, tiles_per_seq):
    i = pl.program_id(0)
    last = pl.num_programs(0) - 2
    w = WINDOW

    @pl.when(i == 0)
    def _():
        r = lax.broadcasted_iota(jnp.int32, (w, w), 0)
        c = lax.broadcasted_iota(jnp.int32, (w, w), 1)
        d = jnp.where(c <= r, r - c, w + r - c)
        for head in range(SWA_HEADS):
            bias_ref[head] = _bias_table(rb_ref, head, d, d >= 0)
            bias_ref[SWA_HEADS + head] = _bias_table(rb_ref, head, d, c <= r)
        attn_ref[1] = jnp.zeros(attn_ref.shape[1:], attn_ref.dtype)

    kbuf_ref[0:w] = kp_ref[...].astype(BF16)
    kbuf_ref[w:] = kc_ref[...].astype(BF16)
    vbuf_ref[0:w] = vp_ref[...].astype(BF16)
    vbuf_ref[w:] = vc_ref[...].astype(BF16)
    tile = jnp.minimum(i, last)
    first_table = jnp.where(tile % tiles_per_seq == 0, SWA_HEADS, 0)
    slot = i % 2

    def attend(jb):
        r0 = jb * w
        attn_ref[slot, r0:r0 + w, :] = _swa_query_block(
            q_ref, r0, kbuf_ref[r0:r0 + 2 * w, :], vbuf_ref[r0:r0 + 2 * w, :],
            first_table if jb == 0 else 0, sink_ref, bias_ref)

    h = _tail_body(h_ref, attn_ref[(i + 1) % 2], p_ref, wproj_ref, gmlp_ref, wup_ref, wdown_ref,
                   gple_ref, wgate_ref, wple_ref,
                   side_work=[functools.partial(attend, jb) for jb in range(qb)])
    y_ref[...] = _rms(h, gfin_ref[...])


def _layer1_prompt(rel_bias, sinks, q, k, v, h, p, wproj, gmlp, wup, wdown, gple, wgate, wple, gfin,
                   *, seq, qb):
    t = h.shape[0]
    tm = qb * WINDOW
    n = t // tm
    cur_i = lambda i: jnp.minimum(i, n - 1)
    lag_i = lambda i: jnp.maximum(i - 1, 0)
    cur = lambda cols: pl.BlockSpec((tm, cols), lambda i: (cur_i(i), 0))
    prev = lambda cols: pl.BlockSpec((WINDOW, cols), lambda i: (jnp.maximum(cur_i(i) * qb - 1, 0), 0))
    lag = lambda cols: pl.BlockSpec((tm, cols), lambda i: (lag_i(i), 0))
    smem = pl.BlockSpec(memory_space=pltpu.SMEM)
    tail_specs = _tail_common_specs(tm, 1)
    return pl.pallas_call(
        functools.partial(_layer1_prompt_kernel, qb=qb, tiles_per_seq=seq // tm),
        grid=(n + 1,),
        in_specs=[smem, smem, cur(D_MODEL), prev(KV_COLS), cur(KV_COLS), prev(KV_COLS), cur(KV_COLS),
                  lag(D_MODEL), pl.BlockSpec((None, tm, PLE_DIM), lambda i: (1, lag_i(i), 0))]
                 + tail_specs[3:] + [_resident((1, D_MODEL))],
        out_specs=lag(D_MODEL),
        out_shape=jax.ShapeDtypeStruct((t, D_MODEL), F32),
        scratch_shapes=[pltpu.VMEM((2 * SWA_HEADS, WINDOW, WINDOW), F32),
                        pltpu.VMEM(((qb + 1) * WINDOW, KV_COLS), BF16),
                        pltpu.VMEM(((qb + 1) * WINDOW, KV_COLS), BF16),
                        pltpu.VMEM((2, tm, D_MODEL), BF16)],
        compiler_params=_params(1),
        name="layer1_prompt",
    )(rel_bias, sinks, q, k, k, v, v, h, p, wproj, gmlp, wup, wdown, gple, wgate, wple, gfin)


def _swa_sample_kernel(rb_ref, sink_ref, q_ref, kn_ref, vn_ref, ck_ref, cv_ref,
                       o_ref, ok_ref, ov_ref, bias_ref, sinkcol_ref, *, tb, s, n_past):
    nk = 2 * WINDOW

    @pl.when(pl.program_id(0) == 0)
    def _():
        d = (n_past + lax.broadcasted_iota(jnp.int32, (s, nk), 0)
             - lax.broadcasted_iota(jnp.int32, (s, nk), 1))
        valid = (d >= 0) & (d < WINDOW)
        for g in range(SWA_GROUP):
            for c in range(SWA_KV_HEADS):
                head = c * SWA_GROUP + g
                r0 = (g * SWA_KV_HEADS + c) * s
                bias_ref[r0:r0 + s, :] = _bias_table(rb_ref, head, d, valid)
                sinkcol_ref[r0:r0 + s, :] = jnp.full((s, LANES), sink_ref[0, head] * LOG2E, F32)

    lane_blk = _lane_block((s, KV_COLS))
    zero_rows = jnp.zeros((WINDOW - s, KV_COLS), F32)
    kn_t = jnp.transpose(kn_ref[...])
    vn_t = jnp.transpose(vn_ref[...])
    keep_old = lax.broadcasted_iota(jnp.int32, (KV_COLS, n_past), 1) < n_past - s

    def chain(b):
        r0 = b * s
        new_shift = (n_past - s - r0) % LANES
        place = (lambda x: pltpu.roll(x, new_shift, axis=1)) if new_shift else (lambda x: x)
        ok_ref[b] = jnp.where(keep_old, pltpu.roll(ck_ref[b], n_past - s, axis=1), place(kn_t))
        ov_ref[b] = jnp.where(keep_old, pltpu.roll(cv_ref[b], n_past - s, axis=1), place(vn_t))
        yield
        k_new = jnp.concatenate([kn_ref[r0:r0 + s, :], zero_rows], axis=0).astype(BF16)
        v_new = jnp.concatenate([vn_ref[r0:r0 + s, :], zero_rows], axis=0).astype(BF16)
        pieces = []
        for g in range(SWA_GROUP):
            slab = q_ref[r0:r0 + s, g * KV_COLS:(g + 1) * KV_COLS]
            pieces += [jnp.where(lane_blk == c, slab, 0.0) for c in range(SWA_KV_HEADS)]
        q_stack = jnp.concatenate(pieces, axis=0).astype(BF16)
        yield
        logits = jnp.concatenate([_dot(q_stack, ck_ref[b].astype(BF16)), _dot_nt(q_stack, k_new)],
                                 axis=1) + bias_ref[...]
        yield
        probs = _softmax_sink(logits, sinkcol_ref[:, :1]).astype(BF16)
        yield
        wide = _dot_nt(probs[:, :n_past], cv_ref[b].astype(BF16)) + _dot(probs[:, n_past:], v_new)
        yield
        for g in range(SWA_GROUP):
            out = jnp.zeros((s, KV_COLS), F32)
            for c in range(SWA_KV_HEADS):
                w0 = (g * SWA_KV_HEADS + c) * s
                out = jnp.where(lane_blk == c, wide[w0:w0 + s], out)
            o_ref[r0:r0 + s, g * KV_COLS:(g + 1) * KV_COLS] = out

    _round_robin([chain(b) for b in range(tb)])


def _swa_sample(rel_bias, sinks, q, k_new, v_new, cache_k, cache_v, *, batch, s, tb):
    n_past = cache_k.shape[2]
    assert n_past == WINDOW and tb * s == LANES
    rows = lambda cols: pl.BlockSpec((tb * s, cols), lambda i: (i, 0))
    cache = pl.BlockSpec((tb, KV_COLS, n_past), lambda i: (i, 0, 0))
    smem = pl.BlockSpec(memory_space=pltpu.SMEM)
    return pl.pallas_call(
        functools.partial(_swa_sample_kernel, tb=tb, s=s, n_past=n_past),
        grid=(batch // tb,),
        in_specs=[smem, smem, rows(D_MODEL), rows(KV_COLS), rows(KV_COLS), cache, cache],
        out_specs=[rows(D_MODEL), cache, cache],
        out_shape=[jax.ShapeDtypeStruct((batch * s, D_MODEL), F32),
                   jax.ShapeDtypeStruct(cache_k.shape, F32), jax.ShapeDtypeStruct(cache_v.shape, F32)],
        scratch_shapes=[pltpu.VMEM((SWA_HEADS * s, 2 * WINDOW), F32),
                        pltpu.VMEM((SWA_HEADS * s, LANES), F32)],
        compiler_params=_params(1),
        name="swa_sample",
    )(rel_bias, sinks, q, k_new, v_new, cache_k, cache_v)


def _prep_weights(norm_mix, norm_mlp, norm_ple, norm_kv, norm_final, w_in_a, w_a2, b_a2, gla_o_gain,
                  w_out_a, w_kv, w_q_b, w_o_b, w_up, w_down, w_ple, w_ple_gate):
    w_in = w_in_a[0]
    wq, wk, wv, wg, wa = jnp.split(w_in, [NQ, 2 * NQ, 2 * NQ + NV, 2 * NQ + 2 * NV], axis=-1)
    wa = jnp.pad(wa, ((0, 0), (0, RANK_PAD - GLA_RANK)))
    wa2 = jnp.pad(w_a2[0], ((0, RANK_PAD - GLA_RANK), (0, 0)))
    wq_b = w_q_b[0].reshape(D_MODEL, SWA_KV_HEADS, SWA_GROUP, SWA_HEAD_DIM).transpose(0, 2, 1, 3)
    wo_b = w_o_b[0].reshape(SWA_KV_HEADS, SWA_GROUP, SWA_HEAD_DIM, D_MODEL).transpose(1, 0, 2, 3)
    bf = lambda w: w.astype(BF16)
    row = lambda g: g.reshape(1, -1)
    layers = lambda g: g.reshape(g.shape[0], 1, -1)
    proj_w = (bf(w_out_a[0]), bf(wo_b.reshape(D_MODEL, D_MODEL)))
    rest = (layers(norm_ple), bf(w_ple_gate), bf(w_ple))

    def tail_weights(layer, wup_bf, wdown_bf):
        return (proj_w[layer], layers(norm_mlp), wup_bf, wdown_bf) + rest
    per_head = lambda m, d: m.reshape(D_MODEL, GLA_HEADS, d)
    wh = jnp.concatenate([per_head(wq, GLA_DK), per_head(wk, GLA_DK), per_head(wv, GLA_DV),
                          per_head(wg, GLA_DV)], axis=-1).reshape(D_MODEL, GLA_HEADS * HEAD_COLS)
    return dict(
        gla_heads=(row(norm_mix[0]), bf(wh), bf(wa), bf(wa2), row(b_a2[0])),
        o_gain=row(gla_o_gain[0]),
        tail=tail_weights,
        shared=(row(norm_kv), bf(w_kv), row(norm_mix[1]),
                bf(wq_b.reshape(D_MODEL, D_MODEL))),
        final=row(norm_final),
    )


def kernel(x_prompt, x_sample, state_gla, cache_win_k, cache_win_v, p_prompt, p_sample, norm_mix, norm_mlp, norm_ple, norm_kv, norm_final, w_in_a, w_a2, b_a2, gla_o_gain, w_out_a, w_kv, w_q_b, w_o_b, sinks, rel_bias, w_up, w_down, w_ple, w_ple_gate):
    w = _prep_weights(norm_mix, norm_mlp, norm_ple, norm_kv, norm_final, w_in_a, w_a2, b_a2, gla_o_gain,
                      w_out_a, w_kv, w_q_b, w_o_b, w_up, w_down, w_ple, w_ple_gate)
    sink_row = sinks[0].reshape(1, SWA_HEADS)

    bp, sp, _ = x_prompt.shape
    tp = bp * sp
    xp = x_prompt.reshape(tp, D_MODEL)
    pp = p_prompt.reshape(2, tp, PLE_DIM)
    mix, state_prompt, wup_bf, wdown_bf = _gla_prompt(
        xp, *w["gla_heads"], w["o_gain"], w_up.reshape(-1, D_FF), w_down.reshape(-1, D_MODEL),
        batch=bp, seq=sp, chunk=GLA_PROMPT_CHUNK)
    tails = [w["tail"](layer, wup_bf.reshape(w_up.shape), wdown_bf.reshape(w_down.shape))
             for layer in range(2)]
    h, k_sh, v_sh, q_b = _tail0(xp, mix, pp, *tails[0], *w["shared"], tm=TOKEN_TILE, q_dtype=BF16)
    y_prompt = _layer1_prompt(rel_bias, sink_row, q_b, k_sh, v_sh, h, pp, *tails[1], w["final"],
                              seq=sp, qb=SWA_PROMPT_BLOCKS).reshape(bp, sp, D_MODEL)
    keep = min(WINDOW, sp)
    cache_shape = (bp, keep, SWA_KV_HEADS, SWA_HEAD_DIM)
    cache_k_prompt = k_sh.reshape(bp, sp, KV_COLS)[:, sp - keep:].reshape(cache_shape)
    cache_v_prompt = v_sh.reshape(bp, sp, KV_COLS)[:, sp - keep:].reshape(cache_shape)

    bs, ss, _ = x_sample.shape
    ts = bs * ss
    xs = x_sample.reshape(ts, D_MODEL)
    ps = p_sample.reshape(2, ts, PLE_DIM)
    n_past = cache_win_k.shape[1]
    proj, lg = _gla_proj(xs, *w["gla_heads"], tm=TOKEN_TILE)
    mix, state_sample = _gla_sample(proj, lg, w["o_gain"], state_gla[0], batch=bs, s=ss,
                                    tb=GLA_SAMPLE_BATCH)
    h, k_sh, v_sh, q_b = _tail0(xs, mix, ps, *tails[0], *w["shared"], tm=TOKEN_TILE, q_dtype=F32)
    to_feature_major = lambda c: jnp.transpose(c, (0, 2, 3, 1)).reshape(bs, KV_COLS, n_past)
    from_feature_major = lambda c: jnp.transpose(
        c.reshape(bs, SWA_KV_HEADS, SWA_HEAD_DIM, n_past), (0, 3, 1, 2))
    attn, ck, cv = _swa_sample(rel_bias, sink_row, q_b, k_sh, v_sh, to_feature_major(cache_win_k),
                               to_feature_major(cache_win_v), batch=bs, s=ss, tb=SWA_SAMPLE_BATCH)
    y_sample = _tail1(h, attn, ps, *tails[1], w["final"], tm=TOKEN_TILE).reshape(bs, ss, D_MODEL)

    return (y_prompt, y_sample, state_prompt[None], state_sample[None],
            cache_k_prompt, cache_v_prompt, from_feature_major(ck), from_feature_major(cv))
```

```python
import functools
import math

import numpy as np
import jax
import jax.numpy as jnp
from jax import lax
from jax.experimental import pallas as pl
from jax.experimental.pallas import tpu as pltpu

F32 = jnp.float32
BF16 = jnp.bfloat16

D_MODEL = 1024
D_FF = 4 * D_MODEL
PLE_DIM = 256
GLA_HEADS = 4
GLA_DK = 128
GLA_DV = 256
GLA_RANK = 16
GLA_TAU = 16.0
SWA_HEADS = 16
SWA_KV_HEADS = 4
SWA_GROUP = 4
SWA_HEAD_DIM = 64
WINDOW = 128
REL_BUCKETS = 32
REL_MAX_DIST = 128
EPS = 1e-6

NQ = GLA_HEADS * GLA_DK
NV = GLA_HEADS * GLA_DV
KV_COLS = SWA_KV_HEADS * SWA_HEAD_DIM
LANES = 128
RANK_PAD = LANES
NEG = -1e30
EXP_CLAMP = 80.0
LOG2E = math.log2(math.e)
VMEM_LIMIT = 62 * 1024 * 1024

TOKEN_TILE = 512
GLA_PROMPT_CHUNK = 256
GLA_SAMPLE_BATCH = 8
SWA_PROMPT_BLOCKS = 4
SWA_SAMPLE_BATCH = 16


def _rel_bucket_starts():
    max_exact = REL_BUCKETS // 2
    starts = {}
    for d in range(WINDOW):
        if d < max_exact:
            b = d
        else:
            b = max_exact + int(np.float32(
                np.log(np.float32(d) / np.float32(max_exact)) / np.float32(math.log(REL_MAX_DIST / max_exact))
                * np.float32(REL_BUCKETS - max_exact)))
            b = min(b, REL_BUCKETS - 1)
        starts.setdefault(b, d)
    return sorted((d, b) for b, d in starts.items())


BUCKET_STARTS = _rel_bucket_starts()


def _dot(a, b):
    return jnp.dot(a, b, preferred_element_type=F32)


def _dot_nt(a, b):
    return lax.dot_general(a, b, (((1,), (1,)), ((), ())), preferred_element_type=F32)


def _dot_tn(a, b):
    return lax.dot_general(a, b, (((0,), (0,)), ((), ())), preferred_element_type=F32)


def _rms(x, gain):
    return x * lax.rsqrt(jnp.mean(x * x, axis=-1, keepdims=True) + EPS) * gain


def _sigmoid(x):
    return 1.0 / (1.0 + jnp.exp(-x))


def _const_spec(shape):
    return pl.BlockSpec(shape, lambda *_: (0,) * len(shape))


def _params(n_axes, flags=None):
    return pltpu.CompilerParams(dimension_semantics=("arbitrary",) * n_axes,
                                vmem_limit_bytes=VMEM_LIMIT, flags=flags)


def _log_forget(hn, wa_ref, wa2_ref, ba2_ref):
    z = _dot(_dot(hn, wa_ref[...]).astype(BF16), wa2_ref[...]) + ba2_ref[...]
    return (jnp.minimum(z, 0.0) - jnp.log(1.0 + jnp.exp(-jnp.abs(z)))) * (1.0 / GLA_TAU)


PROJ_COLS = 2 * NQ + 2 * NV
MXU_COLS = 256


def _head_slices(h):
    q0, k0, v0, g0 = h * GLA_DK, NQ + h * GLA_DK, 2 * NQ + h * GLA_DV, 2 * NQ + NV + h * GLA_DV
    return (slice(q0, q0 + GLA_DK), slice(k0, k0 + GLA_DK), slice(v0, v0 + GLA_DV),
            slice(g0, g0 + GLA_DV))


def _project_stages(hn, w_ref, wa_ref, wa2_ref, ba2_ref, proj_ref, lg_ref):
    lg_ref[...] = _log_forget(hn, wa_ref, wa2_ref, ba2_ref)
    for c0 in range(0, PROJ_COLS, MXU_COLS):
        yield
        out = _dot(hn, w_ref[:, c0:c0 + MXU_COLS])
        if c0 < NQ:
            out = out * GLA_DK ** -0.5
        proj_ref[:, c0:c0 + MXU_COLS] = out.astype(proj_ref.dtype)


def _gla_proj_kernel(x_ref, gain_ref, wh_ref, wa_ref, wa2_ref, ba2_ref, proj_ref, lg_ref):
    hn = _rms(x_ref[...], gain_ref[...]).astype(BF16)
    _round_robin([_project_stages(hn, wh_ref, wa_ref, wa2_ref, ba2_ref, proj_ref, lg_ref)])


def _gla_proj(x, gain, wh, wa, wa2, ba2, *, tm):
    t = x.shape[0]
    row = lambda cols: pl.BlockSpec((tm, cols), lambda i: (i, 0))
    return pl.pallas_call(
        _gla_proj_kernel,
        grid=(t // tm,),
        in_specs=[row(D_MODEL), _const_spec((1, D_MODEL)), _const_spec((D_MODEL, PROJ_COLS)),
                  _const_spec((D_MODEL, RANK_PAD)), _const_spec((RANK_PAD, NQ)), _const_spec((1, NQ))],
        out_specs=[row(PROJ_COLS), row(NQ)],
        out_shape=[jax.ShapeDtypeStruct((t, PROJ_COLS), F32),
                   jax.ShapeDtypeStruct((t, NQ), F32)],
        compiler_params=_params(1),
        name="gla_proj",
    )(x, gain, wh, wa, wa2, ba2)


def _cumsum_rows(x):
    c = x.shape[0]
    row = lax.broadcasted_iota(jnp.int32, x.shape, 0)
    shift = 1
    while shift < c:
        x = x + jnp.where(row >= shift, pltpu.roll(x, shift, axis=0), 0.0)
        shift *= 2
    return x


def _row_to_col(row):
    return jnp.transpose(jnp.broadcast_to(row, (LANES, LANES)))[:, :1]


def _round_robin(chains):
    results = [None] * len(chains)
    live = list(enumerate(chains))
    while live:
        still = []
        for idx, chain in live:
            try:
                next(chain)
                still.append((idx, chain))
            except StopIteration as done:
                results[idx] = done.value
        live = still
    return results


def _coarsen(chain, n):
    count = 0
    while True:
        try:
            next(chain)
        except StopIteration as done:
            return done.value
        count += 1
        if count % n == 0:
            yield


def _gla_chunk_stages(q, k, v, lg, state):
    c = q.shape[0]
    bc = _cumsum_rows(lg)
    last = bc[c - 1:c, :]
    mid = bc[c // 2 - 1:c // 2, :]
    out_of_range = jnp.max(jnp.maximum(bc[0:1, :] - mid, mid - last)) > EXP_CLAMP
    yield
    q_in = (q * jnp.exp(jnp.clip(bc - mid, -EXP_CLAMP, EXP_CLAMP))).astype(BF16)
    k_in = (k * jnp.exp(jnp.clip(mid - bc, -EXP_CLAMP, EXP_CLAMP))).astype(BF16)
    yield
    attn = _dot_nt(q_in, k_in)
    yield
    causal = lax.broadcasted_iota(jnp.int32, (c, c), 0) >= lax.broadcasted_iota(jnp.int32, (c, c), 1)
    attn = jnp.where(causal, attn, 0.0).astype(BF16)
    q_st = (q * jnp.exp(bc)).astype(BF16)
    yield
    o = _dot(attn, v) + _dot(q_st, state.astype(BF16))
    yield
    k_st = (k * jnp.exp(last - bc)).astype(BF16)
    decay_col = _row_to_col(jnp.exp(last))
    yield
    new_state = decay_col * state + _dot_tn(k_st, v)
    return o, new_state, out_of_range


def _gla_exact_output(q, k, v, lg, state, scratch):
    qs_ref, ks_ref, bs_ref, attn_ref = scratch
    c = q.shape[0]
    bc = _cumsum_rows(lg)
    qs_ref[...] = q
    ks_ref[...] = k
    bs_ref[...] = bc
    attn_ref[...] = jnp.zeros_like(attn_ref)
    col_id = lax.broadcasted_iota(jnp.int32, (c, c), 1)

    def key_row(j, carry):
        decay = jnp.exp(jnp.minimum(bs_ref[...] - bs_ref[pl.ds(j, 1), :], 0.0))
        col = jnp.sum(qs_ref[...] * decay * ks_ref[pl.ds(j, 1), :], axis=1, keepdims=True)
        attn_ref[...] = jnp.where(col_id == j, col, attn_ref[...])
        return carry

    lax.fori_loop(0, c, key_row, 0)
    causal = lax.broadcasted_iota(jnp.int32, (c, c), 0) >= col_id
    attn = jnp.where(causal, attn_ref[...], 0.0).astype(BF16)
    return _dot(attn, v) + _dot((q * jnp.exp(bc)).astype(BF16), state.astype(BF16))


def _exact_scratch(c):
    return [pltpu.VMEM((c, GLA_DK), F32), pltpu.VMEM((c, GLA_DK), F32), pltpu.VMEM((c, GLA_DK), F32),
            pltpu.VMEM((c, c), F32)]


def _gla_gate(o, o_gain, g):
    return _rms(o, o_gain) * (g * _sigmoid(g))


def _gla_prompt_kernel(x_ref, gain_ref, win_ref, wa_ref, wa2_ref, ba2_ref, ogain_ref,
                       wup_ref, wdown_ref, wgate_ref,
                       mix_ref, sfin_ref, wbf_ref, wup_bf_ref, wdown_bf_ref, wgate_bf_ref,
                       proj_a, proj_b, lg_a, lg_b, state_ref, sprev_ref, *exact_scratch, tiles_per_seq):
    i = pl.program_id(0)
    lag = jnp.maximum(i - 1, 0)

    @pl.when(i == 0)
    def _():
        proj_b[...] = jnp.zeros_like(proj_b)
        lg_b[...] = jnp.zeros_like(lg_b)
        wbf_ref[...] = win_ref[:, :PROJ_COLS].astype(BF16)

    @pl.when(lag % tiles_per_seq == 0)
    def _():
        state_ref[...] = jnp.zeros_like(state_ref)

    def project(proj_w, lg_w):
        hn = _rms(x_ref[...], gain_ref[...]).astype(BF16)
        yield
        yield from _project_stages(hn, wbf_ref, wa_ref, wa2_ref, ba2_ref, proj_w, lg_w)

    def head_inputs(h, proj_r, lg_r):
        q_sl, k_sl, v_sl, _ = _head_slices(h)
        return (proj_r[:, q_sl].astype(F32), proj_r[:, k_sl].astype(F32), proj_r[:, v_sl],
                lg_r[:, q_sl])

    def write_mix(h, o, proj_r):
        v_sl, g_sl = _head_slices(h)[2:]
        mix_ref[:, h * GLA_DV:(h + 1) * GLA_DV] = _gla_gate(
            o, ogain_ref[...], proj_r[:, g_sl].astype(F32)).astype(mix_ref.dtype)

    def recur(h, proj_r, lg_r):
        state = state_ref[h]
        sprev_ref[h] = state
        o, new_state, out_of_range = yield from _gla_chunk_stages(*head_inputs(h, proj_r, lg_r), state)
        state_ref[h] = new_state
        yield
        write_mix(h, o, proj_r)
        return out_of_range

    def recur_all(proj_r, lg_r):
        flags = []
        for h in range(GLA_HEADS):
            flags.append((yield from recur(h, proj_r, lg_r)))
            yield
        return flags

    def cast_mlp_weights():
        wup_bf_ref[...] = wup_ref[...].astype(BF16)
        yield
        wdown_bf_ref[...] = wdown_ref[...].astype(BF16)
        yield
        wgate_bf_ref[...] = wgate_ref[...].astype(BF16)

    def body(proj_w, lg_w, proj_r, lg_r):
        _, flags, _ = _round_robin([project(proj_w, lg_w), _coarsen(recur_all(proj_r, lg_r), 2),
                                    cast_mlp_weights()])
        for h, out_of_range in enumerate(flags):
            @pl.when(out_of_range)
            def _(h=h):
                o = _gla_exact_output(*head_inputs(h, proj_r, lg_r), sprev_ref[h], exact_scratch)
                write_mix(h, o, proj_r)

    @pl.when(i % 2 == 0)
    def _():
        body(proj_a, lg_a, proj_b, lg_b)

    @pl.when(i % 2 == 1)
    def _():
        body(proj_b, lg_b, proj_a, lg_a)

    @pl.when((lag % tiles_per_seq == tiles_per_seq - 1) & (i > 0))
    def _():
        sfin_ref[0] = state_ref[...]


def _gla_prompt(x, gain, w_in, wa, wa2, ba2, o_gain, w_up, w_down, w_gate, *, batch, seq, chunk):
    tps = seq // chunk
    n = batch * tps
    lag_i = lambda i: jnp.maximum(i - 1, 0)
    cur_i = lambda i: jnp.minimum(i, n - 1)
    streamed = (w_up, w_down, w_gate)
    assert all(w.shape[0] % (16 * n) == 0 for w in streamed)
    w_slice = lambda w: pl.BlockSpec((w.shape[0] // n, w.shape[1]), lambda i: (cur_i(i), 0))
    return pl.pallas_call(
        functools.partial(_gla_prompt_kernel, tiles_per_seq=tps),
        grid=(n + 1,),
        in_specs=[pl.BlockSpec((chunk, D_MODEL), lambda i: (cur_i(i), 0)),
                  _const_spec((1, D_MODEL)), _resident(w_in.shape),
                  _const_spec((D_MODEL, RANK_PAD)), _const_spec((RANK_PAD, NQ)), _const_spec((1, NQ)),
                  _const_spec((1, GLA_DV))] + [w_slice(w) for w in streamed],
        out_specs=[pl.BlockSpec((chunk, NV), lambda i: (lag_i(i), 0)),
                   pl.BlockSpec((1, GLA_HEADS, GLA_DK, GLA_DV), lambda i: (lag_i(i) // tps, 0, 0, 0)),
                   _const_spec((D_MODEL, PROJ_COLS))] + [w_slice(w) for w in streamed],
        out_shape=[jax.ShapeDtypeStruct((batch * seq, NV), BF16),
                   jax.ShapeDtypeStruct((batch, GLA_HEADS, GLA_DK, GLA_DV), F32),
                   jax.ShapeDtypeStruct((D_MODEL, PROJ_COLS), BF16)]
                  + [jax.ShapeDtypeStruct(w.shape, BF16) for w in streamed],
        scratch_shapes=[pltpu.VMEM((chunk, PROJ_COLS), BF16),
                        pltpu.VMEM((chunk, PROJ_COLS), BF16),
                        pltpu.VMEM((chunk, NQ), F32), pltpu.VMEM((chunk, NQ), F32),
                        pltpu.VMEM((GLA_HEADS, GLA_DK, GLA_DV), F32),
                        pltpu.VMEM((GLA_HEADS, GLA_DK, GLA_DV), F32)] + _exact_scratch(chunk),
        compiler_params=_params(1),
        name="gla_prompt",
    )(x, gain, w_in, wa, wa2, ba2, o_gain, w_up, w_down, w_gate)


def _gla_sample_kernel(proj_ref, lg_ref, ogain_ref, s0_ref, mix_ref, sfin_ref, *exact_scratch, tb, s):
    def inputs(b, h):
        rows = slice(b * s, (b + 1) * s)
        q_sl, k_sl, v_sl, _ = _head_slices(h)
        return (proj_ref[rows, q_sl], proj_ref[rows, k_sl], proj_ref[rows, v_sl].astype(BF16),
                lg_ref[rows, q_sl], s0_ref[b, h])

    def write_mix(b, h, o):
        rows = slice(b * s, (b + 1) * s)
        g_sl = _head_slices(h)[3]
        mix_ref[rows, h * GLA_DV:(h + 1) * GLA_DV] = _gla_gate(o, ogain_ref[...], proj_ref[rows, g_sl])

    def chain(b, h):
        o, new_state, out_of_range = yield from _gla_chunk_stages(*inputs(b, h))
        sfin_ref[b, h] = new_state
        yield
        write_mix(b, h, o)
        return out_of_range

    pairs = [(b, h) for b in range(tb) for h in range(GLA_HEADS)]
    flags = _round_robin([chain(b, h) for b, h in pairs])
    for (b, h), out_of_range in zip(pairs, flags):
        @pl.when(out_of_range)
        def _(b=b, h=h):
            write_mix(b, h, _gla_exact_output(*inputs(b, h), exact_scratch))


def _gla_sample(proj, lg, o_gain, s0, *, batch, s, tb):
    rows = lambda cols: pl.BlockSpec((tb * s, cols), lambda i: (i, 0))
    st = pl.BlockSpec((tb, GLA_HEADS, GLA_DK, GLA_DV), lambda i: (i, 0, 0, 0))
    return pl.pallas_call(
        functools.partial(_gla_sample_kernel, tb=tb, s=s),
        grid=(batch // tb,),
        in_specs=[rows(PROJ_COLS), rows(NQ), _const_spec((1, GLA_DV)), st],
        out_specs=[rows(NV), st],
        out_shape=[jax.ShapeDtypeStruct((batch * s, NV), F32),
                   jax.ShapeDtypeStruct((batch, GLA_HEADS, GLA_DK, GLA_DV), F32)],
        scratch_shapes=_exact_scratch(s),
        compiler_params=_params(1),
        name="gla_sample",
    )(proj, lg, o_gain, s0)


def _tail_body(h_ref, mix, p_ref, wproj_ref, gmlp_ref, wup_ref, wdown_ref, gple_ref, wgate_ref,
               wple_ref, side_work=()):
    h = h_ref[...] + _dot(mix.astype(BF16), wproj_ref[...])
    hn = _rms(h, gmlp_ref[...]).astype(BF16)
    ff_chunk = D_MODEL
    for j in range(D_FF // ff_chunk):
        if j < len(side_work):
            side_work[j]()
        cols = slice(j * ff_chunk, (j + 1) * ff_chunk)
        u = jnp.square(jnp.maximum(_dot(hn, wup_ref[:, cols]), 0.0)).astype(BF16)
        h = h + _dot(u, wdown_ref[cols, :])
    gate = _sigmoid(_dot(_rms(h, gple_ref[...]).astype(BF16), wgate_ref[...]))
    return h + gate * _dot(p_ref[...].astype(BF16), wple_ref[...])


def _tail0_kernel(h_ref, mix_ref, p_ref, wproj_ref, gmlp_ref, wup_ref, wdown_ref, gple_ref, wgate_ref,
                  wple_ref, gkv_ref, wkv_ref, gq_ref, wq_ref,
                  hout_ref, kout_ref, vout_ref, qout_ref):
    h = _tail_body(h_ref, mix_ref[...], p_ref, wproj_ref, gmlp_ref, wup_ref, wdown_ref, gple_ref,
                   wgate_ref, wple_ref)
    hout_ref[...] = h
    hkv = _rms(h, gkv_ref[...]).astype(BF16)
    kout_ref[...] = _dot(hkv, wkv_ref[:, :KV_COLS])
    vout_ref[...] = _dot(hkv, wkv_ref[:, KV_COLS:])
    hq = _rms(h, gq_ref[...]).astype(BF16)
    qout_ref[...] = (_dot(hq, wq_ref[...]) * (SWA_HEAD_DIM ** -0.5 * LOG2E)).astype(qout_ref.dtype)


def _tail1_kernel(h_ref, mix_ref, p_ref, wproj_ref, gmlp_ref, wup_ref, wdown_ref, gple_ref, wgate_ref,
                  wple_ref, gfin_ref, y_ref):
    h = _tail_body(h_ref, mix_ref[...], p_ref, wproj_ref, gmlp_ref, wup_ref, wdown_ref, gple_ref,
                   wgate_ref, wple_ref)
    y_ref[...] = _rms(h, gfin_ref[...])


def _resident(shape):
    return pl.BlockSpec(shape, lambda *_: (0,) * len(shape), pipeline_mode=pl.Buffered(1))


def _layer_resident(shape, layer):
    return pl.BlockSpec((None,) + shape, lambda *_: (layer,) + (0,) * len(shape),
                        pipeline_mode=pl.Buffered(1))


def _tail_common_specs(tm, layer):
    row = lambda cols: pl.BlockSpec((tm, cols), lambda i: (i, 0))
    return [row(D_MODEL), row(D_MODEL), pl.BlockSpec((None, tm, PLE_DIM), lambda i: (layer, i, 0)),
            _resident((D_MODEL, D_MODEL)), _layer_resident((1, D_MODEL), layer),
            _layer_resident((D_MODEL, D_FF), layer), _layer_resident((D_FF, D_MODEL), layer),
            _layer_resident((1, D_MODEL), layer), _layer_resident((D_MODEL, D_MODEL), layer),
            _layer_resident((PLE_DIM, D_MODEL), layer)]


def _tail0(h, mix, p, wproj, gmlp, wup, wdown, gple, wgate, wple, gkv, wkv, gq, wq, *, tm, q_dtype):
    t = h.shape[0]
    row = lambda cols: pl.BlockSpec((tm, cols), lambda i: (i, 0))
    return pl.pallas_call(
        _tail0_kernel,
        grid=(t // tm,),
        in_specs=_tail_common_specs(tm, 0) + [
            _resident((1, D_MODEL)), _resident((D_MODEL, 2 * KV_COLS)),
            _resident((1, D_MODEL)), _resident((D_MODEL, D_MODEL))],
        out_specs=[row(D_MODEL), row(KV_COLS), row(KV_COLS), row(D_MODEL)],
        out_shape=[jax.ShapeDtypeStruct((t, D_MODEL), F32), jax.ShapeDtypeStruct((t, KV_COLS), F32),
                   jax.ShapeDtypeStruct((t, KV_COLS), F32), jax.ShapeDtypeStruct((t, D_MODEL), q_dtype)],
        compiler_params=_params(1),
        name="tail0",
    )(h, mix, p, wproj, gmlp, wup, wdown, gple, wgate, wple, gkv, wkv, gq, wq)


def _tail1(h, mix, p, wproj, gmlp, wup, wdown, gple, wgate, wple, gfin, *, tm):
    t = h.shape[0]
    return pl.pallas_call(
        _tail1_kernel,
        grid=(t // tm,),
        in_specs=_tail_common_specs(tm, 1) + [_resident((1, D_MODEL))],
        out_specs=pl.BlockSpec((tm, D_MODEL), lambda i: (i, 0)),
        out_shape=jax.ShapeDtypeStruct((t, D_MODEL), F32),
        compiler_params=_params(1),
        name="tail1",
    )(h, mix, p, wproj, gmlp, wup, wdown, gple, wgate, wple, gfin)


def _bias_table(rb_ref, head, d, valid):
    val = jnp.full(d.shape, rb_ref[BUCKET_STARTS[0][1], head], F32)
    for start, bucket in BUCKET_STARTS[1:]:
        val = jnp.where(d >= start, rb_ref[bucket, head], val)
    return jnp.where(valid, val * LOG2E, NEG)


def _softmax_sink(logits, sink):
    m = jnp.maximum(jnp.max(logits, axis=-1, keepdims=True), sink)
    e = jnp.exp2(logits - m)
    return e / (jnp.sum(e, axis=-1, keepdims=True) + jnp.exp2(sink - m))


def _lane_block(shape):
    return lax.broadcasted_iota(jnp.int32, shape, 1) // SWA_HEAD_DIM


def _swa_query_block(q_ref, r0, kblk, vblk, table, sink_ref, bias_ref):
    w = WINDOW
    rows = SWA_KV_HEADS * w
    from_prev = (lax.broadcasted_iota(jnp.int32, (rows, w), 1)
                 > (lax.broadcasted_iota(jnp.int32, (rows, w), 0) & (w - 1)))
    lane_blk = _lane_block((w, KV_COLS))
    slabs = []
    for g in range(SWA_GROUP):
        slab = q_ref[r0:r0 + w, g * KV_COLS:(g + 1) * KV_COLS].astype(F32)
        q_stack = jnp.concatenate(
            [jnp.where(lane_blk == c, slab, 0.0) for c in range(SWA_KV_HEADS)], axis=0).astype(BF16)
        logits = _dot_nt(q_stack, kblk)
        folded = jnp.where(from_prev, logits[:, :w], logits[:, w:])
        es, scales = [], []
        for c in range(SWA_KV_HEADS):
            head = c * SWA_GROUP + g
            sink = sink_ref[0, head] * LOG2E
            lg = folded[c * w:(c + 1) * w] + bias_ref[table + head]
            m = jnp.maximum(jnp.max(lg, axis=-1, keepdims=True), sink)
            e = jnp.exp2(lg - m)
            scales.append(1.0 / (jnp.sum(e, axis=-1, keepdims=True) + jnp.exp2(sink - m)))
            es.append(e)
        e_all = jnp.concatenate(es, axis=0)
        probs = jnp.concatenate([jnp.where(from_prev, e_all, 0.0).astype(BF16),
                                 jnp.where(from_prev, 0.0, e_all).astype(BF16)], axis=1)
        wide = _dot(probs, vblk)
        out = jnp.zeros((w, KV_COLS), F32)
        for c in range(SWA_KV_HEADS):
            out = jnp.where(lane_blk == c, wide[c * w:(c + 1) * w] * scales[c], out)
        slabs.append(out.astype(BF16))
    return jnp.concatenate(slabs, axis=1)


def _layer1_prompt_kernel(rb_ref, sink_ref, q_ref, kp_ref, kc_ref, vp_ref, vc_ref,
                          h_ref, p_ref, wproj_ref, gmlp_ref, wup_ref, wdown_ref, gple_ref, wgate_ref,
                          wple_ref, gfin_ref, y_ref, bias_ref, kbuf_ref, vbuf_ref, attn_ref,
                          *, qb, tiles_per_seq):
    i = pl.program_id(0)
    last = pl.num_programs(0) - 2
    w = WINDOW

    @pl.when(i == 0)
    def _():
        r = lax.broadcasted_iota(jnp.int32, (w, w), 0)
        c = lax.broadcasted_iota(jnp.int32, (w, w), 1)
        d = jnp.where(c <= r, r - c, w + r - c)
        for head in range(SWA_HEADS):
            bias_ref[head] = _bias_table(rb_ref, head, d, d >= 0)
            bias_ref[SWA_HEADS + head] = _bias_table(rb_ref, head, d, c <= r)
        attn_ref[1] = jnp.zeros(attn_ref.shape[1:], attn_ref.dtype)

    kbuf_ref[0:w] = kp_ref[...].astype(BF16)
    kbuf_ref[w:] = kc_ref[...].astype(BF16)
    vbuf_ref[0:w] = vp_ref[...].astype(BF16)
    vbuf_ref[w:] = vc_ref[...].astype(BF16)
    tile = jnp.minimum(i, last)
    first_table = jnp.where(tile % tiles_per_seq == 0, SWA_HEADS, 0)
    slot = i % 2

    def attend(jb):
        r0 = jb * w
        attn_ref[slot, r0:r0 + w, :] = _swa_query_block(
            q_ref, r0, kbuf_ref[r0:r0 + 2 * w, :], vbuf_ref[r0:r0 + 2 * w, :],
            first_table if jb == 0 else 0, sink_ref, bias_ref)

    h = _tail_body(h_ref, attn_ref[(i + 1) % 2], p_ref, wproj_ref, gmlp_ref, wup_ref, wdown_ref,
                   gple_ref, wgate_ref, wple_ref,
                   side_work=[functools.partial(attend, jb) for jb in range(qb)])
    y_ref[...] = _rms(h, gfin_ref[...])


def _layer1_prompt(rel_bias, sinks, q, k, v, h, p, wproj, gmlp, wup, wdown, gple, wgate, wple, gfin,
                   *, seq, qb):
    t = h.shape[0]
    tm = qb * WINDOW
    n = t // tm
    cur_i = lambda i: jnp.minimum(i, n - 1)
    lag_i = lambda i: jnp.maximum(i - 1, 0)
    cur = lambda cols: pl.BlockSpec((tm, cols), lambda i: (cur_i(i), 0))
    prev = lambda cols: pl.BlockSpec((WINDOW, cols), lambda i: (jnp.maximum(cur_i(i) * qb - 1, 0), 0))
    lag = lambda cols: pl.BlockSpec((tm, cols), lambda i: (lag_i(i), 0))
    smem = pl.BlockSpec(memory_space=pltpu.SMEM)
    tail_specs = _tail_common_specs(tm, 1)
    return pl.pallas_call(
        functools.partial(_layer1_prompt_kernel, qb=qb, tiles_per_seq=seq // tm),
        grid=(n + 1,),
        in_specs=[smem, smem, cur(D_MODEL), prev(KV_COLS), cur(KV_COLS), prev(KV_COLS), cur(KV_COLS),
                  lag(D_MODEL), pl.BlockSpec((None, tm, PLE_DIM), lambda i: (1, lag_i(i), 0))]
                 + tail_specs[3:] + [_resident((1, D_MODEL))],
        out_specs=lag(D_MODEL),
        out_shape=jax.ShapeDtypeStruct((t, D_MODEL), F32),
        scratch_shapes=[pltpu.VMEM((2 * SWA_HEADS, WINDOW, WINDOW), F32),
                        pltpu.VMEM(((qb + 1) * WINDOW, KV_COLS), BF16),
                        pltpu.VMEM(((qb + 1) * WINDOW, KV_COLS), BF16),
                        pltpu.VMEM((2, tm, D_MODEL), BF16)],
        compiler_params=_params(1),
        name="layer1_prompt",
    )(rel_bias, sinks, q, k, k, v, v, h, p, wproj, gmlp, wup, wdown, gple, wgate, wple, gfin)


def _swa_sample_kernel(rb_ref, sink_ref, q_ref, kn_ref, vn_ref, ck_ref, cv_ref,
                       o_ref, ok_ref, ov_ref, bias_ref, sinkcol_ref, *, tb, s, n_past):
    nk = 2 * WINDOW

    @pl.when(pl.program_id(0) == 0)
    def _():
        d = (n_past + lax.broadcasted_iota(jnp.int32, (s, nk), 0)
             - lax.broadcasted_iota(jnp.int32, (s, nk), 1))
        valid = (d >= 0) & (d < WINDOW)
        for g in range(SWA_GROUP):
            for c in range(SWA_KV_HEADS):
                head = c * SWA_GROUP + g
                r0 = (g * SWA_KV_HEADS + c) * s
                bias_ref[r0:r0 + s, :] = _bias_table(rb_ref, head, d, valid)
                sinkcol_ref[r0:r0 + s, :] = jnp.full((s, LANES), sink_ref[0, head] * LOG2E, F32)

    lane_blk = _lane_block((s, KV_COLS))
    zero_rows = jnp.zeros((WINDOW - s, KV_COLS), F32)
    kn_t = jnp.transpose(kn_ref[...])
    vn_t = jnp.transpose(vn_ref[...])
    keep_old = lax.broadcasted_iota(jnp.int32, (KV_COLS, n_past), 1) < n_past - s

    def chain(b):
        r0 = b * s
        new_shift = (n_past - s - r0) % LANES
        place = (lambda x: pltpu.roll(x, new_shift, axis=1)) if new_shift else (lambda x: x)
        ok_ref[b] = jnp.where(keep_old, pltpu.roll(ck_ref[b], n_past - s, axis=1), place(kn_t))
        ov_ref[b] = jnp.where(keep_old, pltpu.roll(cv_ref[b], n_past - s, axis=1), place(vn_t))
        yield
        k_new = jnp.concatenate([kn_ref[r0:r0 + s, :], zero_rows], axis=0).astype(BF16)
        v_new = jnp.concatenate([vn_ref[r0:r0 + s, :], zero_rows], axis=0).astype(BF16)
        pieces = []
        for g in range(SWA_GROUP):
            slab = q_ref[r0:r0 + s, g * KV_COLS:(g + 1) * KV_COLS]
            pieces += [jnp.where(lane_blk == c, slab, 0.0) for c in range(SWA_KV_HEADS)]
        q_stack = jnp.concatenate(pieces, axis=0).astype(BF16)
        yield
        logits = jnp.concatenate([_dot(q_stack, ck_ref[b].astype(BF16)), _dot_nt(q_stack, k_new)],
                                 axis=1) + bias_ref[...]
        yield
        probs = _softmax_sink(logits, sinkcol_ref[:, :1]).astype(BF16)
        yield
        wide = _dot_nt(probs[:, :n_past], cv_ref[b].astype(BF16)) + _dot(probs[:, n_past:], v_new)
        yield
        for g in range(SWA_GROUP):
            out = jnp.zeros((s, KV_COLS), F32)
            for c in range(SWA_KV_HEADS):
                w0 = (g * SWA_KV_HEADS + c) * s
                out = jnp.where(lane_blk == c, wide[w0:w0 + s], out)
            o_ref[r0:r0 + s, g * KV_COLS:(g + 1) * KV_COLS] = out

    _round_robin([chain(b) for b in range(tb)])


def _swa_sample(rel_bias, sinks, q, k_new, v_new, cache_k, cache_v, *, batch, s, tb):
    n_past = cache_k.shape[2]
    assert n_past == WINDOW and tb * s == LANES
    rows = lambda cols: pl.BlockSpec((tb * s, cols), lambda i: (i, 0))
    cache = pl.BlockSpec((tb, KV_COLS, n_past), lambda i: (i, 0, 0))
    smem = pl.BlockSpec(memory_space=pltpu.SMEM)
    return pl.pallas_call(
        functools.partial(_swa_sample_kernel, tb=tb, s=s, n_past=n_past),
        grid=(batch // tb,),
        in_specs=[smem, smem, rows(D_MODEL), rows(KV_COLS), rows(KV_COLS), cache, cache],
        out_specs=[rows(D_MODEL), cache, cache],
        out_shape=[jax.ShapeDtypeStruct((batch * s, D_MODEL), F32),
                   jax.ShapeDtypeStruct(cache_k.shape, F32), jax.ShapeDtypeStruct(cache_v.shape, F32)],
        scratch_shapes=[pltpu.VMEM((SWA_HEADS * s, 2 * WINDOW), F32),
                        pltpu.VMEM((SWA_HEADS * s, LANES), F32)],
        compiler_params=_params(1),
        name="swa_sample",
    )(rel_bias, sinks, q, k_new, v_new, cache_k, cache_v)


def _prep_weights(norm_mix, norm_mlp, norm_ple, norm_kv, norm_final, w_in_a, w_a2, b_a2, gla_o_gain,
                  w_out_a, w_kv, w_q_b, w_o_b, w_up, w_down, w_ple, w_ple_gate):
    wa = jnp.pad(w_in_a[0][:, PROJ_COLS:], ((0, 0), (0, RANK_PAD - GLA_RANK)))
    wa2 = jnp.pad(w_a2[0], ((0, RANK_PAD - GLA_RANK), (0, 0)))
    wq_b = w_q_b[0].reshape(D_MODEL, SWA_KV_HEADS, SWA_GROUP, SWA_HEAD_DIM).transpose(0, 2, 1, 3)
    wo_b = w_o_b[0].reshape(SWA_KV_HEADS, SWA_GROUP, SWA_HEAD_DIM, D_MODEL).transpose(1, 0, 2, 3)
    bf = lambda w: w.astype(BF16)
    row = lambda g: g.reshape(1, -1)
    layers = lambda g: g.reshape(g.shape[0], 1, -1)
    proj_w = (bf(w_out_a[0]), bf(wo_b.reshape(D_MODEL, D_MODEL)))

    def tail_weights(layer, wup_bf, wdown_bf, wgate_bf):
        return (proj_w[layer], layers(norm_mlp), wup_bf, wdown_bf, layers(norm_ple), wgate_bf, bf(w_ple))

    return dict(
        gla_gate=(bf(wa), bf(wa2), row(b_a2[0])),
        gla_gain=row(norm_mix[0]),
        o_gain=row(gla_o_gain[0]),
        tail=tail_weights,
        shared=(row(norm_kv), bf(w_kv), row(norm_mix[1]),
                bf(wq_b.reshape(D_MODEL, D_MODEL))),
        final=row(norm_final),
    )


def kernel(x_prompt, x_sample, state_gla, cache_win_k, cache_win_v, p_prompt, p_sample, norm_mix, norm_mlp, norm_ple, norm_kv, norm_final, w_in_a, w_a2, b_a2, gla_o_gain, w_out_a, w_kv, w_q_b, w_o_b, sinks, rel_bias, w_up, w_down, w_ple, w_ple_gate):
    w = _prep_weights(norm_mix, norm_mlp, norm_ple, norm_kv, norm_final, w_in_a, w_a2, b_a2, gla_o_gain,
                      w_out_a, w_kv, w_q_b, w_o_b, w_up, w_down, w_ple, w_ple_gate)
    sink_row = sinks[0].reshape(1, SWA_HEADS)

    bp, sp, _ = x_prompt.shape
    tp = bp * sp
    xp = x_prompt.reshape(tp, D_MODEL)
    pp = p_prompt.reshape(2, tp, PLE_DIM)
    mix, state_prompt, w_proj_bf, wup_bf, wdown_bf, wgate_bf = _gla_prompt(
        xp, w["gla_gain"], w_in_a[0], *w["gla_gate"], w["o_gain"], w_up.reshape(-1, D_FF),
        w_down.reshape(-1, D_MODEL), w_ple_gate.reshape(-1, D_MODEL),
        batch=bp, seq=sp, chunk=GLA_PROMPT_CHUNK)
    tails = [w["tail"](layer, wup_bf.reshape(w_up.shape), wdown_bf.reshape(w_down.shape),
                       wgate_bf.reshape(w_ple_gate.shape)) for layer in range(2)]
    h, k_sh, v_sh, q_b = _tail0(xp, mix, pp, *tails[0], *w["shared"], tm=TOKEN_TILE, q_dtype=BF16)
    y_prompt = _layer1_prompt(rel_bias, sink_row, q_b, k_sh, v_sh, h, pp, *tails[1], w["final"],
                              seq=sp, qb=SWA_PROMPT_BLOCKS).reshape(bp, sp, D_MODEL)
    keep = min(WINDOW, sp)
    cache_shape = (bp, keep, SWA_KV_HEADS, SWA_HEAD_DIM)
    cache_k_prompt = k_sh.reshape(bp, sp, KV_COLS)[:, sp - keep:].reshape(cache_shape)
    cache_v_prompt = v_sh.reshape(bp, sp, KV_COLS)[:, sp - keep:].reshape(cache_shape)

    bs, ss, _ = x_sample.shape
    ts = bs * ss
    xs = x_sample.reshape(ts, D_MODEL)
    ps = p_sample.reshape(2, ts, PLE_DIM)
    n_past = cache_win_k.shape[1]
    proj, lg = _gla_proj(xs, w["gla_gain"], w_proj_bf, *w["gla_gate"], tm=TOKEN_TILE)
    mix, state_sample = _gla_sample(proj, lg, w["o_gain"], state_gla[0], batch=bs, s=ss,
                                    tb=GLA_SAMPLE_BATCH)
    h, k_sh, v_sh, q_b = _tail0(xs, mix, ps, *tails[0], *w["shared"], tm=TOKEN_TILE, q_dtype=F32)
    to_feature_major = lambda c: jnp.transpose(c, (0, 2, 3, 1)).reshape(bs, KV_COLS, n_past)
    from_feature_major = lambda c: jnp.transpose(
        c.reshape(bs, SWA_KV_HEADS, SWA_HEAD_DIM, n_past), (0, 3, 1, 2))
    attn, ck, cv = _swa_sample(rel_bias, sink_row, q_b, k_sh, v_sh, to_feature_major(cache_win_k),
                               to_feature_major(cache_win_v), batch=bs, s=ss, tb=SWA_SAMPLE_BATCH)
    y_sample = _tail1(h, attn, ps, *tails[1], w["final"], tm=TOKEN_TILE).reshape(bs, ss, D_MODEL)

    return (y_prompt, y_sample, state_prompt[None], state_sample[None],
            cache_k_prompt, cache_v_prompt, from_feature_major(ck), from_feature_major(cv))
```

```python
import functools
import math

import numpy as np
import jax
import jax.numpy as jnp
from jax import lax
from jax.experimental import pallas as pl
from jax.experimental.pallas import tpu as pltpu

F32 = jnp.float32
BF16 = jnp.bfloat16

D_MODEL = 1024
D_FF = 4 * D_MODEL
PLE_DIM = 256
GLA_HEADS = 4
GLA_DK = 128
GLA_DV = 256
GLA_RANK = 16
GLA_TAU = 16.0
SWA_HEADS = 16
SWA_KV_HEADS = 4
SWA_GROUP = 4
SWA_HEAD_DIM = 64
WINDOW = 128
REL_BUCKETS = 32
REL_MAX_DIST = 128
EPS = 1e-6

NQ = GLA_HEADS * GLA_DK
NV = GLA_HEADS * GLA_DV
KV_COLS = SWA_KV_HEADS * SWA_HEAD_DIM
LANES = 128
RANK_PAD = LANES
NEG = -1e30
EXP_CLAMP = 80.0
LOG2E = math.log2(math.e)
VMEM_LIMIT = 62 * 1024 * 1024

TOKEN_TILE = 512
GLA_PROMPT_CHUNK = 256
GLA_SAMPLE_BATCH = 8
SWA_PROMPT_BLOCKS = 4
SWA_SAMPLE_BATCH = 16


def _rel_bucket_starts():
    max_exact = REL_BUCKETS // 2
    starts = {}
    for d in range(WINDOW):
        if d < max_exact:
            b = d
        else:
            b = max_exact + int(np.float32(
                np.log(np.float32(d) / np.float32(max_exact)) / np.float32(math.log(REL_MAX_DIST / max_exact))
                * np.float32(REL_BUCKETS - max_exact)))
            b = min(b, REL_BUCKETS - 1)
        starts.setdefault(b, d)
    return sorted((d, b) for b, d in starts.items())


BUCKET_STARTS = _rel_bucket_starts()


def _dot(a, b):
    return jnp.dot(a, b, preferred_element_type=F32)


def _dot_nt(a, b):
    return lax.dot_general(a, b, (((1,), (1,)), ((), ())), preferred_element_type=F32)


def _dot_tn(a, b):
    return lax.dot_general(a, b, (((0,), (0,)), ((), ())), preferred_element_type=F32)


def _rms(x, gain):
    return x * lax.rsqrt(jnp.mean(x * x, axis=-1, keepdims=True) + EPS) * gain


def _sigmoid(x):
    return 1.0 / (1.0 + jnp.exp(-x))


def _const_spec(shape):
    return pl.BlockSpec(shape, lambda *_: (0,) * len(shape))


def _params(n_axes, flags=None):
    return pltpu.CompilerParams(dimension_semantics=("arbitrary",) * n_axes,
                                vmem_limit_bytes=VMEM_LIMIT, flags=flags)


def _log_forget(hn, wa_ref, wa2_ref, ba2_ref):
    z = _dot(_dot(hn, wa_ref[...]).astype(BF16), wa2_ref[...]) + ba2_ref[...]
    return (jnp.minimum(z, 0.0) - jnp.log(1.0 + jnp.exp(-jnp.abs(z)))) * (1.0 / GLA_TAU)


PROJ_COLS = 2 * NQ + 2 * NV
MXU_COLS = 256


def _head_slices(h):
    q0, k0, v0, g0 = h * GLA_DK, NQ + h * GLA_DK, 2 * NQ + h * GLA_DV, 2 * NQ + NV + h * GLA_DV
    return (slice(q0, q0 + GLA_DK), slice(k0, k0 + GLA_DK), slice(v0, v0 + GLA_DV),
            slice(g0, g0 + GLA_DV))


def _project_stages(hn, w_ref, wa_ref, wa2_ref, ba2_ref, proj_ref, lg_ref):
    lg_ref[...] = _log_forget(hn, wa_ref, wa2_ref, ba2_ref)
    for c0 in range(0, PROJ_COLS, MXU_COLS):
        yield
        out = _dot_nt(hn, w_ref[c0:c0 + MXU_COLS, :])
        if c0 < NQ:
            out = out * GLA_DK ** -0.5
        proj_ref[:, c0:c0 + MXU_COLS] = out.astype(proj_ref.dtype)


def _gla_proj_kernel(x_ref, gain_ref, wh_ref, wa_ref, wa2_ref, ba2_ref, proj_ref, lg_ref):
    hn = _rms(x_ref[...], gain_ref[...]).astype(BF16)
    _round_robin([_project_stages(hn, wh_ref, wa_ref, wa2_ref, ba2_ref, proj_ref, lg_ref)])


def _gla_proj(x, gain, wh, wa, wa2, ba2, *, tm):
    t = x.shape[0]
    row = lambda cols: pl.BlockSpec((tm, cols), lambda i: (i, 0))
    return pl.pallas_call(
        _gla_proj_kernel,
        grid=(t // tm,),
        in_specs=[row(D_MODEL), _const_spec((1, D_MODEL)), _const_spec((PROJ_COLS, D_MODEL)),
                  _const_spec((D_MODEL, RANK_PAD)), _const_spec((RANK_PAD, NQ)), _const_spec((1, NQ))],
        out_specs=[row(PROJ_COLS), row(NQ)],
        out_shape=[jax.ShapeDtypeStruct((t, PROJ_COLS), F32),
                   jax.ShapeDtypeStruct((t, NQ), F32)],
        compiler_params=_params(1),
        name="gla_proj",
    )(x, gain, wh, wa, wa2, ba2)


def _cumsum_rows(x):
    c = x.shape[0]
    row = lax.broadcasted_iota(jnp.int32, x.shape, 0)
    shift = 1
    while shift < c:
        x = x + jnp.where(row >= shift, pltpu.roll(x, shift, axis=0), 0.0)
        shift *= 2
    return x


def _row_to_col(row):
    return jnp.transpose(jnp.broadcast_to(row, (LANES, LANES)))[:, :1]


def _round_robin(chains):
    results = [None] * len(chains)
    live = list(enumerate(chains))
    while live:
        still = []
        for idx, chain in live:
            try:
                next(chain)
                still.append((idx, chain))
            except StopIteration as done:
                results[idx] = done.value
        live = still
    return results


def _coarsen(chain, n):
    count = 0
    while True:
        try:
            next(chain)
        except StopIteration as done:
            return done.value
        count += 1
        if count % n == 0:
            yield


def _gla_chunk_stages(q, k, v, lg, state):
    c = q.shape[0]
    bc = _cumsum_rows(lg)
    last = bc[c - 1:c, :]
    mid = bc[c // 2 - 1:c // 2, :]
    out_of_range = jnp.max(jnp.maximum(bc[0:1, :] - mid, mid - last)) > EXP_CLAMP
    yield
    q_in = (q * jnp.exp(jnp.clip(bc - mid, -EXP_CLAMP, EXP_CLAMP))).astype(BF16)
    k_in = (k * jnp.exp(jnp.clip(mid - bc, -EXP_CLAMP, EXP_CLAMP))).astype(BF16)
    yield
    attn = _dot_nt(q_in, k_in)
    yield
    causal = lax.broadcasted_iota(jnp.int32, (c, c), 0) >= lax.broadcasted_iota(jnp.int32, (c, c), 1)
    attn = jnp.where(causal, attn, 0.0).astype(BF16)
    q_st = (q * jnp.exp(bc)).astype(BF16)
    yield
    o = _dot(attn, v) + _dot(q_st, state.astype(BF16))
    yield
    k_st = (k * jnp.exp(last - bc)).astype(BF16)
    decay_col = _row_to_col(jnp.exp(last))
    yield
    new_state = decay_col * state + _dot_tn(k_st, v)
    return o, new_state, out_of_range


def _gla_exact_output(q, k, v, lg, state, scratch):
    qs_ref, ks_ref, bs_ref, attn_ref = scratch
    c = q.shape[0]
    bc = _cumsum_rows(lg)
    qs_ref[...] = q
    ks_ref[...] = k
    bs_ref[...] = bc
    attn_ref[...] = jnp.zeros_like(attn_ref)
    col_id = lax.broadcasted_iota(jnp.int32, (c, c), 1)

    def key_row(j, carry):
        decay = jnp.exp(jnp.minimum(bs_ref[...] - bs_ref[pl.ds(j, 1), :], 0.0))
        col = jnp.sum(qs_ref[...] * decay * ks_ref[pl.ds(j, 1), :], axis=1, keepdims=True)
        attn_ref[...] = jnp.where(col_id == j, col, attn_ref[...])
        return carry

    lax.fori_loop(0, c, key_row, 0)
    causal = lax.broadcasted_iota(jnp.int32, (c, c), 0) >= col_id
    attn = jnp.where(causal, attn_ref[...], 0.0).astype(BF16)
    return _dot(attn, v) + _dot((q * jnp.exp(bc)).astype(BF16), state.astype(BF16))


def _exact_scratch(c):
    return [pltpu.VMEM((c, GLA_DK), F32), pltpu.VMEM((c, GLA_DK), F32), pltpu.VMEM((c, GLA_DK), F32),
            pltpu.VMEM((c, c), F32)]


def _gla_gate(o, o_gain, g):
    return _rms(o, o_gain) * (g * _sigmoid(g))


def _gla_prompt_kernel(x_ref, gain_ref, win_ref, wa_ref, wa2_ref, ba2_ref, ogain_ref,
                       wup_ref, wdown_ref, wgate_ref,
                       mix_ref, sfin_ref, wbf_ref, wup_bf_ref, wdown_bf_ref, wgate_bf_ref,
                       proj_a, proj_b, lg_a, lg_b, state_ref, sprev_ref, *exact_scratch, tiles_per_seq):
    i = pl.program_id(0)
    lag = jnp.maximum(i - 1, 0)

    @pl.when(i == 0)
    def _():
        proj_b[...] = jnp.zeros_like(proj_b)
        lg_b[...] = jnp.zeros_like(lg_b)
        wbf_ref[...] = win_ref[:PROJ_COLS, :].astype(BF16)

    @pl.when(lag % tiles_per_seq == 0)
    def _():
        state_ref[...] = jnp.zeros_like(state_ref)

    def project(proj_w, lg_w):
        hn = _rms(x_ref[...], gain_ref[...]).astype(BF16)
        yield
        yield from _project_stages(hn, wbf_ref, wa_ref, wa2_ref, ba2_ref, proj_w, lg_w)

    def head_inputs(h, proj_r, lg_r):
        q_sl, k_sl, v_sl, _ = _head_slices(h)
        return (proj_r[:, q_sl].astype(F32), proj_r[:, k_sl].astype(F32), proj_r[:, v_sl],
                lg_r[:, q_sl])

    def write_mix(h, o, proj_r):
        v_sl, g_sl = _head_slices(h)[2:]
        mix_ref[:, h * GLA_DV:(h + 1) * GLA_DV] = _gla_gate(
            o, ogain_ref[...], proj_r[:, g_sl].astype(F32)).astype(mix_ref.dtype)

    def recur(h, proj_r, lg_r):
        state = state_ref[h]
        sprev_ref[h] = state
        o, new_state, out_of_range = yield from _gla_chunk_stages(*head_inputs(h, proj_r, lg_r), state)
        state_ref[h] = new_state
        yield
        write_mix(h, o, proj_r)
        return out_of_range

    def recur_all(proj_r, lg_r):
        flags = []
        for h in range(GLA_HEADS):
            flags.append((yield from recur(h, proj_r, lg_r)))
            yield
        return flags

    def cast_mlp_weights():
        wup_bf_ref[...] = wup_ref[...].astype(BF16)
        yield
        wdown_bf_ref[...] = wdown_ref[...].astype(BF16)
        yield
        wgate_bf_ref[...] = wgate_ref[...].astype(BF16)

    def body(proj_w, lg_w, proj_r, lg_r):
        _, flags, _ = _round_robin([project(proj_w, lg_w), _coarsen(recur_all(proj_r, lg_r), 2),
                                    cast_mlp_weights()])
        for h, out_of_range in enumerate(flags):
            @pl.when(out_of_range)
            def _(h=h):
                o = _gla_exact_output(*head_inputs(h, proj_r, lg_r), sprev_ref[h], exact_scratch)
                write_mix(h, o, proj_r)

    @pl.when(i % 2 == 0)
    def _():
        body(proj_a, lg_a, proj_b, lg_b)

    @pl.when(i % 2 == 1)
    def _():
        body(proj_b, lg_b, proj_a, lg_a)

    @pl.when((lag % tiles_per_seq == tiles_per_seq - 1) & (i > 0))
    def _():
        sfin_ref[0] = state_ref[...]


def _gla_prompt(x, gain, w_in, wa, wa2, ba2, o_gain, w_up, w_down, w_gate, *, batch, seq, chunk):
    tps = seq // chunk
    n = batch * tps
    lag_i = lambda i: jnp.maximum(i - 1, 0)
    cur_i = lambda i: jnp.minimum(i, n - 1)
    streamed = (w_up, w_down, w_gate)
    assert all(w.shape[0] % (16 * n) == 0 for w in streamed)
    w_slice = lambda w: pl.BlockSpec((w.shape[0] // n, w.shape[1]), lambda i: (cur_i(i), 0))
    return pl.pallas_call(
        functools.partial(_gla_prompt_kernel, tiles_per_seq=tps),
        grid=(n + 1,),
        in_specs=[pl.BlockSpec((chunk, D_MODEL), lambda i: (cur_i(i), 0)),
                  _const_spec((1, D_MODEL)), _resident(w_in.shape),
                  _const_spec((D_MODEL, RANK_PAD)), _const_spec((RANK_PAD, NQ)), _const_spec((1, NQ)),
                  _const_spec((1, GLA_DV))] + [w_slice(w) for w in streamed],
        out_specs=[pl.BlockSpec((chunk, NV), lambda i: (lag_i(i), 0)),
                   pl.BlockSpec((1, GLA_HEADS, GLA_DK, GLA_DV), lambda i: (lag_i(i) // tps, 0, 0, 0)),
                   _const_spec((PROJ_COLS, D_MODEL))] + [w_slice(w) for w in streamed],
        out_shape=[jax.ShapeDtypeStruct((batch * seq, NV), BF16),
                   jax.ShapeDtypeStruct((batch, GLA_HEADS, GLA_DK, GLA_DV), F32),
                   jax.ShapeDtypeStruct((PROJ_COLS, D_MODEL), BF16)]
                  + [jax.ShapeDtypeStruct(w.shape, BF16) for w in streamed],
        scratch_shapes=[pltpu.VMEM((chunk, PROJ_COLS), BF16),
                        pltpu.VMEM((chunk, PROJ_COLS), BF16),
                        pltpu.VMEM((chunk, NQ), F32), pltpu.VMEM((chunk, NQ), F32),
                        pltpu.VMEM((GLA_HEADS, GLA_DK, GLA_DV), F32),
                        pltpu.VMEM((GLA_HEADS, GLA_DK, GLA_DV), F32)] + _exact_scratch(chunk),
        compiler_params=_params(1),
        name="gla_prompt",
    )(x, gain, w_in, wa, wa2, ba2, o_gain, w_up, w_down, w_gate)


def _gla_sample_kernel(proj_ref, lg_ref, ogain_ref, s0_ref, mix_ref, sfin_ref, *exact_scratch, tb, s):
    def inputs(b, h):
        rows = slice(b * s, (b + 1) * s)
        q_sl, k_sl, v_sl, _ = _head_slices(h)
        return (proj_ref[rows, q_sl], proj_ref[rows, k_sl], proj_ref[rows, v_sl].astype(BF16),
                lg_ref[rows, q_sl], s0_ref[b, h])

    def write_mix(b, h, o):
        rows = slice(b * s, (b + 1) * s)
        g_sl = _head_slices(h)[3]
        mix_ref[rows, h * GLA_DV:(h + 1) * GLA_DV] = _gla_gate(o, ogain_ref[...], proj_ref[rows, g_sl])

    def chain(b, h):
        o, new_state, out_of_range = yield from _gla_chunk_stages(*inputs(b, h))
        sfin_ref[b, h] = new_state
        yield
        write_mix(b, h, o)
        return out_of_range

    pairs = [(b, h) for b in range(tb) for h in range(GLA_HEADS)]
    flags = _round_robin([chain(b, h) for b, h in pairs])
    for (b, h), out_of_range in zip(pairs, flags):
        @pl.when(out_of_range)
        def _(b=b, h=h):
            write_mix(b, h, _gla_exact_output(*inputs(b, h), exact_scratch))


def _gla_sample(proj, lg, o_gain, s0, *, batch, s, tb):
    rows = lambda cols: pl.BlockSpec((tb * s, cols), lambda i: (i, 0))
    st = pl.BlockSpec((tb, GLA_HEADS, GLA_DK, GLA_DV), lambda i: (i, 0, 0, 0))
    return pl.pallas_call(
        functools.partial(_gla_sample_kernel, tb=tb, s=s),
        grid=(batch // tb,),
        in_specs=[rows(PROJ_COLS), rows(NQ), _const_spec((1, GLA_DV)), st],
        out_specs=[rows(NV), st],
        out_shape=[jax.ShapeDtypeStruct((batch * s, NV), F32),
                   jax.ShapeDtypeStruct((batch, GLA_HEADS, GLA_DK, GLA_DV), F32)],
        scratch_shapes=_exact_scratch(s),
        compiler_params=_params(1),
        name="gla_sample",
    )(proj, lg, o_gain, s0)


def _tail_body(h_ref, mix, p_ref, wproj_ref, gmlp_ref, wup_ref, wdown_ref, gple_ref, wgate_ref,
               wple_ref, side_work=()):
    h = h_ref[...] + _dot(mix.astype(BF16), wproj_ref[...])
    hn = _rms(h, gmlp_ref[...]).astype(BF16)
    ff_chunk = D_MODEL
    for j in range(D_FF // ff_chunk):
        if j < len(side_work):
            side_work[j]()
        cols = slice(j * ff_chunk, (j + 1) * ff_chunk)
        u = jnp.square(jnp.maximum(_dot(hn, wup_ref[:, cols]), 0.0)).astype(BF16)
        h = h + _dot(u, wdown_ref[cols, :])
    gate = _sigmoid(_dot(_rms(h, gple_ref[...]).astype(BF16), wgate_ref[...]))
    return h + gate * _dot(p_ref[...].astype(BF16), wple_ref[...])


def _tail0_kernel(h_ref, mix_ref, p_ref, wproj_ref, gmlp_ref, wup_ref, wdown_ref, gple_ref, wgate_ref,
                  wple_ref, gkv_ref, wkv_ref, gq_ref, wq_ref,
                  hout_ref, kout_ref, vout_ref, qout_ref):
    h = _tail_body(h_ref, mix_ref[...], p_ref, wproj_ref, gmlp_ref, wup_ref, wdown_ref, gple_ref,
                   wgate_ref, wple_ref)
    hout_ref[...] = h
    hkv = _rms(h, gkv_ref[...]).astype(BF16)
    kout_ref[...] = _dot(hkv, wkv_ref[:, :KV_COLS])
    vout_ref[...] = _dot(hkv, wkv_ref[:, KV_COLS:])
    hq = _rms(h, gq_ref[...]).astype(BF16)
    qout_ref[...] = (_dot(hq, wq_ref[...]) * (SWA_HEAD_DIM ** -0.5 * LOG2E)).astype(qout_ref.dtype)


def _tail1_kernel(h_ref, mix_ref, p_ref, wproj_ref, gmlp_ref, wup_ref, wdown_ref, gple_ref, wgate_ref,
                  wple_ref, gfin_ref, y_ref):
    h = _tail_body(h_ref, mix_ref[...], p_ref, wproj_ref, gmlp_ref, wup_ref, wdown_ref, gple_ref,
                   wgate_ref, wple_ref)
    y_ref[...] = _rms(h, gfin_ref[...])


def _resident(shape):
    return pl.BlockSpec(shape, lambda *_: (0,) * len(shape), pipeline_mode=pl.Buffered(1))


def _layer_resident(shape, layer):
    return pl.BlockSpec((None,) + shape, lambda *_: (layer,) + (0,) * len(shape),
                        pipeline_mode=pl.Buffered(1))


def _tail_common_specs(tm, layer):
    row = lambda cols: pl.BlockSpec((tm, cols), lambda i: (i, 0))
    return [row(D_MODEL), row(D_MODEL), pl.BlockSpec((None, tm, PLE_DIM), lambda i: (layer, i, 0)),
            _resident((D_MODEL, D_MODEL)), _layer_resident((1, D_MODEL), layer),
            _layer_resident((D_MODEL, D_FF), layer), _layer_resident((D_FF, D_MODEL), layer),
            _layer_resident((1, D_MODEL), layer), _layer_resident((D_MODEL, D_MODEL), layer),
            _layer_resident((PLE_DIM, D_MODEL), layer)]


def _tail0(h, mix, p, wproj, gmlp, wup, wdown, gple, wgate, wple, gkv, wkv, gq, wq, *, tm, q_dtype):
    t = h.shape[0]
    row = lambda cols: pl.BlockSpec((tm, cols), lambda i: (i, 0))
    return pl.pallas_call(
        _tail0_kernel,
        grid=(t // tm,),
        in_specs=_tail_common_specs(tm, 0) + [
            _resident((1, D_MODEL)), _resident((D_MODEL, 2 * KV_COLS)),
            _resident((1, D_MODEL)), _resident((D_MODEL, D_MODEL))],
        out_specs=[row(D_MODEL), row(KV_COLS), row(KV_COLS), row(D_MODEL)],
        out_shape=[jax.ShapeDtypeStruct((t, D_MODEL), F32), jax.ShapeDtypeStruct((t, KV_COLS), F32),
                   jax.ShapeDtypeStruct((t, KV_COLS), F32), jax.ShapeDtypeStruct((t, D_MODEL), q_dtype)],
        compiler_params=_params(1),
        name="tail0",
    )(h, mix, p, wproj, gmlp, wup, wdown, gple, wgate, wple, gkv, wkv, gq, wq)


def _tail1(h, mix, p, wproj, gmlp, wup, wdown, gple, wgate, wple, gfin, *, tm):
    t = h.shape[0]
    return pl.pallas_call(
        _tail1_kernel,
        grid=(t // tm,),
        in_specs=_tail_common_specs(tm, 1) + [_resident((1, D_MODEL))],
        out_specs=pl.BlockSpec((tm, D_MODEL), lambda i: (i, 0)),
        out_shape=jax.ShapeDtypeStruct((t, D_MODEL), F32),
        compiler_params=_params(1),
        name="tail1",
    )(h, mix, p, wproj, gmlp, wup, wdown, gple, wgate, wple, gfin)


def _bias_table(rb_ref, head, d, valid):
    val = jnp.full(d.shape, rb_ref[BUCKET_STARTS[0][1], head], F32)
    for start, bucket in BUCKET_STARTS[1:]:
        val = jnp.where(d >= start, rb_ref[bucket, head], val)
    return jnp.where(valid, val * LOG2E, NEG)


def _softmax_sink(logits, sink):
    m = jnp.maximum(jnp.max(logits, axis=-1, keepdims=True), sink)
    e = jnp.exp2(logits - m)
    return e / (jnp.sum(e, axis=-1, keepdims=True) + jnp.exp2(sink - m))


def _lane_block(shape):
    return lax.broadcasted_iota(jnp.int32, shape, 1) // SWA_HEAD_DIM


def _swa_query_block(q_ref, r0, kblk, vblk, table, sink_ref, bias_ref):
    w = WINDOW
    rows = SWA_KV_HEADS * w
    from_prev = (lax.broadcasted_iota(jnp.int32, (rows, w), 1)
                 > (lax.broadcasted_iota(jnp.int32, (rows, w), 0) & (w - 1)))
    lane_blk = _lane_block((w, KV_COLS))
    slabs = []
    for g in range(SWA_GROUP):
        slab = q_ref[r0:r0 + w, g * KV_COLS:(g + 1) * KV_COLS].astype(F32)
        q_stack = jnp.concatenate(
            [jnp.where(lane_blk == c, slab, 0.0) for c in range(SWA_KV_HEADS)], axis=0).astype(BF16)
        logits = _dot_nt(q_stack, kblk)
        folded = jnp.where(from_prev, logits[:, :w], logits[:, w:])
        es, scales = [], []
        for c in range(SWA_KV_HEADS):
            head = c * SWA_GROUP + g
            sink = sink_ref[0, head] * LOG2E
            lg = folded[c * w:(c + 1) * w] + bias_ref[table + head]
            m = jnp.maximum(jnp.max(lg, axis=-1, keepdims=True), sink)
            e = jnp.exp2(lg - m)
            scales.append(1.0 / (jnp.sum(e, axis=-1, keepdims=True) + jnp.exp2(sink - m)))
            es.append(e)
        e_all = jnp.concatenate(es, axis=0)
        probs = jnp.concatenate([jnp.where(from_prev, e_all, 0.0).astype(BF16),
                                 jnp.where(from_prev, 0.0, e_all).astype(BF16)], axis=1)
        wide = _dot(probs, vblk)
        out = jnp.zeros((w, KV_COLS), F32)
        for c in range(SWA_KV_HEADS):
            out = jnp.where(lane_blk == c, wide[c * w:(c + 1) * w] * scales[c], out)
        slabs.append(out.astype(BF16))
    return jnp.concatenate(slabs, axis=1)


def _layer1_prompt_kernel(rb_ref, sink_ref, q_ref, kp_ref, kc_ref, vp_ref, vc_ref,
                          h_ref, p_ref, wproj_ref, gmlp_ref, wup_ref, wdown_ref, gple_ref, wgate_ref,
                          wple_ref, gfin_ref, y_ref, bias_ref, kbuf_ref, vbuf_ref, attn_ref,
                          *, qb, tiles_per_seq):
    i = pl.program_id(0)
    last = pl.num_programs(0) - 2
    w = WINDOW

    @pl.when(i == 0)
    def _():
        r = lax.broadcasted_iota(jnp.int32, (w, w), 0)
        c = lax.broadcasted_iota(jnp.int32, (w, w), 1)
        d = jnp.where(c <= r, r - c, w + r - c)
        for head in range(SWA_HEADS):
            bias_ref[head] = _bias_table(rb_ref, head, d, d >= 0)
            bias_ref[SWA_HEADS + head] = _bias_table(rb_ref, head, d, c <= r)
        attn_ref[1] = jnp.zeros(attn_ref.shape[1:], attn_ref.dtype)

    kbuf_ref[0:w] = kp_ref[...].astype(BF16)
    kbuf_ref[w:] = kc_ref[...].astype(BF16)
    vbuf_ref[0:w] = vp_ref[...].astype(BF16)
    vbuf_ref[w:] = vc_ref[...].astype(BF16)
    tile = jnp.minimum(i, last)
    first_table = jnp.where(tile % tiles_per_seq == 0, SWA_HEADS, 0)
    slot = i % 2

    def attend(jb):
        r0 = jb * w
        attn_ref[slot, r0:r0 + w, :] = _swa_query_block(
            q_ref, r0, kbuf_ref[r0:r0 + 2 * w, :], vbuf_ref[r0:r0 + 2 * w, :],
            first_table if jb == 0 else 0, sink_ref, bias_ref)

    h = _tail_body(h_ref, attn_ref[(i + 1) % 2], p_ref, wproj_ref, gmlp_ref, wup_ref, wdown_ref,
                   gple_ref, wgate_ref, wple_ref,
                   side_work=[functools.partial(attend, jb) for jb in range(qb)])
    y_ref[...] = _rms(h, gfin_ref[...])


def _layer1_prompt(rel_bias, sinks, q, k, v, h, p, wproj, gmlp, wup, wdown, gple, wgate, wple, gfin,
                   *, seq, qb):
    t = h.shape[0]
    tm = qb * WINDOW
    n = t // tm
    cur_i = lambda i: jnp.minimum(i, n - 1)
    lag_i = lambda i: jnp.maximum(i - 1, 0)
    cur = lambda cols: pl.BlockSpec((tm, cols), lambda i: (cur_i(i), 0))
    prev = lambda cols: pl.BlockSpec((WINDOW, cols), lambda i: (jnp.maximum(cur_i(i) * qb - 1, 0), 0))
    lag = lambda cols: pl.BlockSpec((tm, cols), lambda i: (lag_i(i), 0))
    smem = pl.BlockSpec(memory_space=pltpu.SMEM)
    tail_specs = _tail_common_specs(tm, 1)
    return pl.pallas_call(
        functools.partial(_layer1_prompt_kernel, qb=qb, tiles_per_seq=seq // tm),
        grid=(n + 1,),
        in_specs=[smem, smem, cur(D_MODEL), prev(KV_COLS), cur(KV_COLS), prev(KV_COLS), cur(KV_COLS),
                  lag(D_MODEL), pl.BlockSpec((None, tm, PLE_DIM), lambda i: (1, lag_i(i), 0))]
                 + tail_specs[3:] + [_resident((1, D_MODEL))],
        out_specs=lag(D_MODEL),
        out_shape=jax.ShapeDtypeStruct((t, D_MODEL), F32),
        scratch_shapes=[pltpu.VMEM((2 * SWA_HEADS, WINDOW, WINDOW), F32),
                        pltpu.VMEM(((qb + 1) * WINDOW, KV_COLS), BF16),
                        pltpu.VMEM(((qb + 1) * WINDOW, KV_COLS), BF16),
                        pltpu.VMEM((2, tm, D_MODEL), BF16)],
        compiler_params=_params(1),
        name="layer1_prompt",
    )(rel_bias, sinks, q, k, k, v, v, h, p, wproj, gmlp, wup, wdown, gple, wgate, wple, gfin)


def _swa_sample_kernel(rb_ref, sink_ref, q_ref, kn_ref, vn_ref, ck_ref, cv_ref,
                       o_ref, ok_ref, ov_ref, bias_ref, sinkcol_ref, *, tb, s, n_past):
    nk = 2 * WINDOW

    @pl.when(pl.program_id(0) == 0)
    def _():
        d = (n_past + lax.broadcasted_iota(jnp.int32, (s, nk), 0)
             - lax.broadcasted_iota(jnp.int32, (s, nk), 1))
        valid = (d >= 0) & (d < WINDOW)
        for g in range(SWA_GROUP):
            for c in range(SWA_KV_HEADS):
                head = c * SWA_GROUP + g
                r0 = (g * SWA_KV_HEADS + c) * s
                bias_ref[r0:r0 + s, :] = _bias_table(rb_ref, head, d, valid)
                sinkcol_ref[r0:r0 + s, :] = jnp.full((s, LANES), sink_ref[0, head] * LOG2E, F32)

    lane_blk = _lane_block((s, KV_COLS))
    zero_rows = jnp.zeros((WINDOW - s, KV_COLS), F32)
    kn_t = jnp.transpose(kn_ref[...])
    vn_t = jnp.transpose(vn_ref[...])
    keep_old = lax.broadcasted_iota(jnp.int32, (KV_COLS, n_past), 1) < n_past - s

    def chain(b):
        r0 = b * s
        new_shift = (n_past - s - r0) % LANES
        place = (lambda x: pltpu.roll(x, new_shift, axis=1)) if new_shift else (lambda x: x)
        ok_ref[b] = jnp.where(keep_old, pltpu.roll(ck_ref[b], n_past - s, axis=1), place(kn_t))
        ov_ref[b] = jnp.where(keep_old, pltpu.roll(cv_ref[b], n_past - s, axis=1), place(vn_t))
        yield
        k_new = jnp.concatenate([kn_ref[r0:r0 + s, :], zero_rows], axis=0).astype(BF16)
        v_new = jnp.concatenate([vn_ref[r0:r0 + s, :], zero_rows], axis=0).astype(BF16)
        pieces = []
        for g in range(SWA_GROUP):
            slab = q_ref[r0:r0 + s, g * KV_COLS:(g + 1) * KV_COLS]
            pieces += [jnp.where(lane_blk == c, slab, 0.0) for c in range(SWA_KV_HEADS)]
        q_stack = jnp.concatenate(pieces, axis=0).astype(BF16)
        yield
        logits = jnp.concatenate([_dot(q_stack, ck_ref[b].astype(BF16)), _dot_nt(q_stack, k_new)],
                                 axis=1) + bias_ref[...]
        yield
        probs = _softmax_sink(logits, sinkcol_ref[:, :1]).astype(BF16)
        yield
        wide = _dot_nt(probs[:, :n_past], cv_ref[b].astype(BF16)) + _dot(probs[:, n_past:], v_new)
        yield
        for g in range(SWA_GROUP):
            out = jnp.zeros((s, KV_COLS), F32)
            for c in range(SWA_KV_HEADS):
                w0 = (g * SWA_KV_HEADS + c) * s
                out = jnp.where(lane_blk == c, wide[w0:w0 + s], out)
            o_ref[r0:r0 + s, g * KV_COLS:(g + 1) * KV_COLS] = out

    _round_robin([chain(b) for b in range(tb)])


def _swa_sample(rel_bias, sinks, q, k_new, v_new, cache_k, cache_v, *, batch, s, tb):
    n_past = cache_k.shape[2]
    assert n_past == WINDOW and tb * s == LANES
    rows = lambda cols: pl.BlockSpec((tb * s, cols), lambda i: (i, 0))
    cache = pl.BlockSpec((tb, KV_COLS, n_past), lambda i: (i, 0, 0))
    smem = pl.BlockSpec(memory_space=pltpu.SMEM)
    return pl.pallas_call(
        functools.partial(_swa_sample_kernel, tb=tb, s=s, n_past=n_past),
        grid=(batch // tb,),
        in_specs=[smem, smem, rows(D_MODEL), rows(KV_COLS), rows(KV_COLS), cache, cache],
        out_specs=[rows(D_MODEL), cache, cache],
        out_shape=[jax.ShapeDtypeStruct((batch * s, D_MODEL), F32),
                   jax.ShapeDtypeStruct(cache_k.shape, F32), jax.ShapeDtypeStruct(cache_v.shape, F32)],
        scratch_shapes=[pltpu.VMEM((SWA_HEADS * s, 2 * WINDOW), F32),
                        pltpu.VMEM((SWA_HEADS * s, LANES), F32)],
        compiler_params=_params(1),
        name="swa_sample",
    )(rel_bias, sinks, q, k_new, v_new, cache_k, cache_v)


def _prep_weights(norm_mix, norm_mlp, norm_ple, norm_kv, norm_final, w_in_a, w_a2, b_a2, gla_o_gain,
                  w_out_a, w_kv, w_q_b, w_o_b, w_up, w_down, w_ple, w_ple_gate):
    wa = jnp.pad(w_in_a[0][:, PROJ_COLS:], ((0, 0), (0, RANK_PAD - GLA_RANK)))
    wa2 = jnp.pad(w_a2[0], ((0, RANK_PAD - GLA_RANK), (0, 0)))
    wq_b = w_q_b[0].reshape(D_MODEL, SWA_KV_HEADS, SWA_GROUP, SWA_HEAD_DIM).transpose(0, 2, 1, 3)
    wo_b = w_o_b[0].reshape(SWA_KV_HEADS, SWA_GROUP, SWA_HEAD_DIM, D_MODEL).transpose(1, 0, 2, 3)
    bf = lambda w: w.astype(BF16)
    row = lambda g: g.reshape(1, -1)
    layers = lambda g: g.reshape(g.shape[0], 1, -1)
    proj_w = (bf(w_out_a[0]), bf(wo_b.reshape(D_MODEL, D_MODEL)))

    def tail_weights(layer, wup_bf, wdown_bf, wgate_bf):
        return (proj_w[layer], layers(norm_mlp), wup_bf, wdown_bf, layers(norm_ple), wgate_bf, bf(w_ple))

    return dict(
        gla_gate=(bf(wa), bf(wa2), row(b_a2[0])),
        gla_gain=row(norm_mix[0]),
        o_gain=row(gla_o_gain[0]),
        tail=tail_weights,
        shared=(row(norm_kv), bf(w_kv), row(norm_mix[1]),
                bf(wq_b.reshape(D_MODEL, D_MODEL))),
        final=row(norm_final),
    )


def kernel(x_prompt, x_sample, state_gla, cache_win_k, cache_win_v, p_prompt, p_sample, norm_mix, norm_mlp, norm_ple, norm_kv, norm_final, w_in_a, w_a2, b_a2, gla_o_gain, w_out_a, w_kv, w_q_b, w_o_b, sinks, rel_bias, w_up, w_down, w_ple, w_ple_gate):
    w = _prep_weights(norm_mix, norm_mlp, norm_ple, norm_kv, norm_final, w_in_a, w_a2, b_a2, gla_o_gain,
                      w_out_a, w_kv, w_q_b, w_o_b, w_up, w_down, w_ple, w_ple_gate)
    sink_row = sinks[0].reshape(1, SWA_HEADS)

    bp, sp, _ = x_prompt.shape
    tp = bp * sp
    xp = x_prompt.reshape(tp, D_MODEL)
    pp = p_prompt.reshape(2, tp, PLE_DIM)
    mix, state_prompt, w_proj_bf, wup_bf, wdown_bf, wgate_bf = _gla_prompt(
        xp, w["gla_gain"], jnp.transpose(w_in_a[0]), *w["gla_gate"], w["o_gain"], w_up.reshape(-1, D_FF),
        w_down.reshape(-1, D_MODEL), w_ple_gate.reshape(-1, D_MODEL),
        batch=bp, seq=sp, chunk=GLA_PROMPT_CHUNK)
    tails = [w["tail"](layer, wup_bf.reshape(w_up.shape), wdown_bf.reshape(w_down.shape),
                       wgate_bf.reshape(w_ple_gate.shape)) for layer in range(2)]
    h, k_sh, v_sh, q_b = _tail0(xp, mix, pp, *tails[0], *w["shared"], tm=TOKEN_TILE, q_dtype=BF16)
    y_prompt = _layer1_prompt(rel_bias, sink_row, q_b, k_sh, v_sh, h, pp, *tails[1], w["final"],
                              seq=sp, qb=SWA_PROMPT_BLOCKS).reshape(bp, sp, D_MODEL)
    keep = min(WINDOW, sp)
    cache_shape = (bp, keep, SWA_KV_HEADS, SWA_HEAD_DIM)
    cache_k_prompt = k_sh.reshape(bp, sp, KV_COLS)[:, sp - keep:].reshape(cache_shape)
    cache_v_prompt = v_sh.reshape(bp, sp, KV_COLS)[:, sp - keep:].reshape(cache_shape)

    bs, ss, _ = x_sample.shape
    ts = bs * ss
    xs = x_sample.reshape(ts, D_MODEL)
    ps = p_sample.reshape(2, ts, PLE_DIM)
    n_past = cache_win_k.shape[1]
    proj, lg = _gla_proj(xs, w["gla_gain"], w_proj_bf, *w["gla_gate"], tm=TOKEN_TILE)
    mix, state_sample = _gla_sample(proj, lg, w["o_gain"], state_gla[0], batch=bs, s=ss,
                                    tb=GLA_SAMPLE_BATCH)
    h, k_sh, v_sh, q_b = _tail0(xs, mix, ps, *tails[0], *w["shared"], tm=TOKEN_TILE, q_dtype=F32)
    to_feature_major = lambda c: jnp.transpose(c, (0, 2, 3, 1)).reshape(bs, KV_COLS, n_past)
    from_feature_major = lambda c: jnp.transpose(
        c.reshape(bs, SWA_KV_HEADS, SWA_HEAD_DIM, n_past), (0, 3, 1, 2))
    attn, ck, cv = _swa_sample(rel_bias, sink_row, q_b, k_sh, v_sh, to_feature_major(cache_win_k),
                               to_feature_major(cache_win_v), batch=bs, s=ss, tb=SWA_SAMPLE_BATCH)
    y_sample = _tail1(h, attn, ps, *tails[1], w["final"], tm=TOKEN_TILE).reshape(bs, ss, D_MODEL)

    return (y_prompt, y_sample, state_prompt[None], state_sample[None],
            cache_k_prompt, cache_v_prompt, from_feature_major(ck), from_feature_major(cv))
```

```python
import functools
import math

import numpy as np
import jax
import jax.numpy as jnp
from jax import lax
from jax.experimental import pallas as pl
from jax.experimental.pallas import tpu as pltpu

F32 = jnp.float32
BF16 = jnp.bfloat16

D_MODEL = 1024
D_FF = 4 * D_MODEL
PLE_DIM = 256
GLA_HEADS = 4
GLA_DK = 128
GLA_DV = 256
GLA_RANK = 16
GLA_TAU = 16.0
SWA_HEADS = 16
SWA_KV_HEADS = 4
SWA_GROUP = 4
SWA_HEAD_DIM = 64
WINDOW = 128
REL_BUCKETS = 32
REL_MAX_DIST = 128
EPS = 1e-6

NQ = GLA_HEADS * GLA_DK
NV = GLA_HEADS * GLA_DV
KV_COLS = SWA_KV_HEADS * SWA_HEAD_DIM
LANES = 128
RANK_PAD = LANES
NEG = -1e30
EXP_CLAMP = 80.0
LOG2E = math.log2(math.e)
VMEM_LIMIT = 62 * 1024 * 1024

TOKEN_TILE = 512
GLA_PROMPT_CHUNK = 256
GLA_SAMPLE_BATCH = 8
SWA_PROMPT_BLOCKS = 4
SWA_SAMPLE_BATCH = 16


def _rel_bucket_starts():
    max_exact = REL_BUCKETS // 2
    starts = {}
    for d in range(WINDOW):
        if d < max_exact:
            b = d
        else:
            b = max_exact + int(np.float32(
                np.log(np.float32(d) / np.float32(max_exact)) / np.float32(math.log(REL_MAX_DIST / max_exact))
                * np.float32(REL_BUCKETS - max_exact)))
            b = min(b, REL_BUCKETS - 1)
        starts.setdefault(b, d)
    return sorted((d, b) for b, d in starts.items())


BUCKET_STARTS = _rel_bucket_starts()


def _dot(a, b):
    return jnp.dot(a, b, preferred_element_type=F32)


def _dot_nt(a, b):
    return lax.dot_general(a, b, (((1,), (1,)), ((), ())), preferred_element_type=F32)


def _dot_tn(a, b):
    return lax.dot_general(a, b, (((0,), (0,)), ((), ())), preferred_element_type=F32)


def _rms(x, gain):
    return x * lax.rsqrt(jnp.mean(x * x, axis=-1, keepdims=True) + EPS) * gain


def _sigmoid(x):
    return 1.0 / (1.0 + jnp.exp(-x))


def _const_spec(shape):
    return pl.BlockSpec(shape, lambda *_: (0,) * len(shape))


def _params(n_axes, flags=None):
    return pltpu.CompilerParams(dimension_semantics=("arbitrary",) * n_axes,
                                vmem_limit_bytes=VMEM_LIMIT, flags=flags)


def _log_forget(hn, wa_ref, wa2_ref, ba2_ref):
    z = _dot(_dot_nt(hn, wa_ref[...]).astype(BF16), wa2_ref[...]) + ba2_ref[...]
    return (jnp.minimum(z, 0.0) - jnp.log(1.0 + jnp.exp(-jnp.abs(z)))) * (1.0 / GLA_TAU)


PROJ_COLS = 2 * NQ + 2 * NV
MXU_COLS = 256


def _head_slices(h):
    q0, k0, v0, g0 = h * GLA_DK, NQ + h * GLA_DK, 2 * NQ + h * GLA_DV, 2 * NQ + NV + h * GLA_DV
    return (slice(q0, q0 + GLA_DK), slice(k0, k0 + GLA_DK), slice(v0, v0 + GLA_DV),
            slice(g0, g0 + GLA_DV))


def _project_stages(hn, w_ref, wa_ref, wa2_ref, ba2_ref, proj_ref, lg_ref):
    lg_ref[...] = _log_forget(hn, wa_ref, wa2_ref, ba2_ref)
    for c0 in range(0, PROJ_COLS, MXU_COLS):
        yield
        out = _dot_nt(hn, w_ref[c0:c0 + MXU_COLS, :])
        if c0 < NQ:
            out = out * GLA_DK ** -0.5
        proj_ref[:, c0:c0 + MXU_COLS] = out.astype(proj_ref.dtype)


def _gla_proj_kernel(x_ref, gain_ref, wh_ref, wa_ref, wa2_ref, ba2_ref, proj_ref, lg_ref):
    hn = _rms(x_ref[...], gain_ref[...]).astype(BF16)
    _round_robin([_project_stages(hn, wh_ref, wa_ref, wa2_ref, ba2_ref, proj_ref, lg_ref)])


def _gla_proj(x, gain, wh, wa, wa2, ba2, *, tm):
    t = x.shape[0]
    row = lambda cols: pl.BlockSpec((tm, cols), lambda i: (i, 0))
    return pl.pallas_call(
        _gla_proj_kernel,
        grid=(t // tm,),
        in_specs=[row(D_MODEL), _const_spec((1, D_MODEL)), _const_spec((PROJ_COLS, D_MODEL)),
                  _const_spec((RANK_PAD, D_MODEL)), _const_spec((RANK_PAD, NQ)), _const_spec((1, NQ))],
        out_specs=[row(PROJ_COLS), row(NQ)],
        out_shape=[jax.ShapeDtypeStruct((t, PROJ_COLS), F32),
                   jax.ShapeDtypeStruct((t, NQ), F32)],
        compiler_params=_params(1),
        name="gla_proj",
    )(x, gain, wh, wa, wa2, ba2)


def _cumsum_rows(x):
    c = x.shape[0]
    row = lax.broadcasted_iota(jnp.int32, x.shape, 0)
    shift = 1
    while shift < c:
        x = x + jnp.where(row >= shift, pltpu.roll(x, shift, axis=0), 0.0)
        shift *= 2
    return x


def _row_to_col(row):
    return jnp.transpose(jnp.broadcast_to(row, (LANES, LANES)))[:, :1]


def _round_robin(chains):
    results = [None] * len(chains)
    live = list(enumerate(chains))
    while live:
        still = []
        for idx, chain in live:
            try:
                next(chain)
                still.append((idx, chain))
            except StopIteration as done:
                results[idx] = done.value
        live = still
    return results


def _coarsen(chain, n):
    count = 0
    while True:
        try:
            next(chain)
        except StopIteration as done:
            return done.value
        count += 1
        if count % n == 0:
            yield


def _gla_chunk_stages(q, k, v, lg, state):
    c = q.shape[0]
    bc = _cumsum_rows(lg)
    last = bc[c - 1:c, :]
    mid = bc[c // 2 - 1:c // 2, :]
    out_of_range = jnp.max(jnp.maximum(bc[0:1, :] - mid, mid - last)) > EXP_CLAMP
    yield
    q_in = (q * jnp.exp(jnp.clip(bc - mid, -EXP_CLAMP, EXP_CLAMP))).astype(BF16)
    k_in = (k * jnp.exp(jnp.clip(mid - bc, -EXP_CLAMP, EXP_CLAMP))).astype(BF16)
    yield
    attn = _dot_nt(q_in, k_in)
    yield
    causal = lax.broadcasted_iota(jnp.int32, (c, c), 0) >= lax.broadcasted_iota(jnp.int32, (c, c), 1)
    attn = jnp.where(causal, attn, 0.0).astype(BF16)
    q_st = (q * jnp.exp(bc)).astype(BF16)
    yield
    o = _dot(attn, v) + _dot(q_st, state.astype(BF16))
    yield
    k_st = (k * jnp.exp(last - bc)).astype(BF16)
    decay_col = _row_to_col(jnp.exp(last))
    yield
    new_state = decay_col * state + _dot_tn(k_st, v)
    return o, new_state, out_of_range


def _gla_exact_output(q, k, v, lg, state, scratch):
    qs_ref, ks_ref, bs_ref, attn_ref = scratch
    c = q.shape[0]
    bc = _cumsum_rows(lg)
    qs_ref[...] = q
    ks_ref[...] = k
    bs_ref[...] = bc
    attn_ref[...] = jnp.zeros_like(attn_ref)
    col_id = lax.broadcasted_iota(jnp.int32, (c, c), 1)

    def key_row(j, carry):
        decay = jnp.exp(jnp.minimum(bs_ref[...] - bs_ref[pl.ds(j, 1), :], 0.0))
        col = jnp.sum(qs_ref[...] * decay * ks_ref[pl.ds(j, 1), :], axis=1, keepdims=True)
        attn_ref[...] = jnp.where(col_id == j, col, attn_ref[...])
        return carry

    lax.fori_loop(0, c, key_row, 0)
    causal = lax.broadcasted_iota(jnp.int32, (c, c), 0) >= col_id
    attn = jnp.where(causal, attn_ref[...], 0.0).astype(BF16)
    return _dot(attn, v) + _dot((q * jnp.exp(bc)).astype(BF16), state.astype(BF16))


def _exact_scratch(c):
    return [pltpu.VMEM((c, GLA_DK), F32), pltpu.VMEM((c, GLA_DK), F32), pltpu.VMEM((c, GLA_DK), F32),
            pltpu.VMEM((c, c), F32)]


def _gla_gate(o, o_gain, g):
    return _rms(o, o_gain) * (g * _sigmoid(g))


def _gla_prompt_kernel(x_ref, gain_ref, win_ref, wa_ref, wa2_ref, ba2_ref, ogain_ref,
                       wup_ref, wdown_ref, wgate_ref,
                       mix_ref, sfin_ref, wbf_ref, wup_bf_ref, wdown_bf_ref, wgate_bf_ref,
                       proj_a, proj_b, lg_a, lg_b, state_ref, sprev_ref, *exact_scratch, tiles_per_seq):
    i = pl.program_id(0)
    lag = jnp.maximum(i - 1, 0)

    @pl.when(i == 0)
    def _():
        proj_b[...] = jnp.zeros_like(proj_b)
        lg_b[...] = jnp.zeros_like(lg_b)
        wbf_ref[...] = win_ref[:PROJ_COLS, :].astype(BF16)

    @pl.when(lag % tiles_per_seq == 0)
    def _():
        state_ref[...] = jnp.zeros_like(state_ref)

    def project(proj_w, lg_w):
        hn = _rms(x_ref[...], gain_ref[...]).astype(BF16)
        yield
        yield from _project_stages(hn, wbf_ref, wa_ref, wa2_ref, ba2_ref, proj_w, lg_w)

    def head_inputs(h, proj_r, lg_r):
        q_sl, k_sl, v_sl, _ = _head_slices(h)
        return (proj_r[:, q_sl].astype(F32), proj_r[:, k_sl].astype(F32), proj_r[:, v_sl],
                lg_r[:, q_sl])

    def write_mix(h, o, proj_r):
        v_sl, g_sl = _head_slices(h)[2:]
        mix_ref[:, h * GLA_DV:(h + 1) * GLA_DV] = _gla_gate(
            o, ogain_ref[...], proj_r[:, g_sl].astype(F32)).astype(mix_ref.dtype)

    def recur(h, proj_r, lg_r):
        state = state_ref[h]
        sprev_ref[h] = state
        o, new_state, out_of_range = yield from _gla_chunk_stages(*head_inputs(h, proj_r, lg_r), state)
        state_ref[h] = new_state
        yield
        write_mix(h, o, proj_r)
        return out_of_range

    def recur_all(proj_r, lg_r):
        flags = []
        for h in range(GLA_HEADS):
            flags.append((yield from recur(h, proj_r, lg_r)))
            yield
        return flags

    def cast_mlp_weights():
        wup_bf_ref[...] = wup_ref[...].astype(BF16)
        yield
        wdown_bf_ref[...] = wdown_ref[...].astype(BF16)
        yield
        wgate_bf_ref[...] = wgate_ref[...].astype(BF16)

    def body(proj_w, lg_w, proj_r, lg_r):
        _, flags, _ = _round_robin([project(proj_w, lg_w), _coarsen(recur_all(proj_r, lg_r), 2),
                                    cast_mlp_weights()])
        for h, out_of_range in enumerate(flags):
            @pl.when(out_of_range)
            def _(h=h):
                o = _gla_exact_output(*head_inputs(h, proj_r, lg_r), sprev_ref[h], exact_scratch)
                write_mix(h, o, proj_r)

    @pl.when(i % 2 == 0)
    def _():
        body(proj_a, lg_a, proj_b, lg_b)

    @pl.when(i % 2 == 1)
    def _():
        body(proj_b, lg_b, proj_a, lg_a)

    @pl.when((lag % tiles_per_seq == tiles_per_seq - 1) & (i > 0))
    def _():
        sfin_ref[0] = state_ref[...]


def _gla_prompt(x, gain, w_in, wa, wa2, ba2, o_gain, w_up, w_down, w_gate, *, batch, seq, chunk):
    tps = seq // chunk
    n = batch * tps
    lag_i = lambda i: jnp.maximum(i - 1, 0)
    cur_i = lambda i: jnp.minimum(i, n - 1)
    streamed = (w_up, w_down, w_gate)
    assert all(w.shape[0] % (16 * n) == 0 for w in streamed)
    w_slice = lambda w: pl.BlockSpec((w.shape[0] // n, w.shape[1]), lambda i: (cur_i(i), 0))
    return pl.pallas_call(
        functools.partial(_gla_prompt_kernel, tiles_per_seq=tps),
        grid=(n + 1,),
        in_specs=[pl.BlockSpec((chunk, D_MODEL), lambda i: (cur_i(i), 0)),
                  _const_spec((1, D_MODEL)), _resident(w_in.shape),
                  _const_spec((RANK_PAD, D_MODEL)), _const_spec((RANK_PAD, NQ)), _const_spec((1, NQ)),
                  _const_spec((1, GLA_DV))] + [w_slice(w) for w in streamed],
        out_specs=[pl.BlockSpec((chunk, NV), lambda i: (lag_i(i), 0)),
                   pl.BlockSpec((1, GLA_HEADS, GLA_DK, GLA_DV), lambda i: (lag_i(i) // tps, 0, 0, 0)),
                   _const_spec((PROJ_COLS, D_MODEL))] + [w_slice(w) for w in streamed],
        out_shape=[jax.ShapeDtypeStruct((batch * seq, NV), BF16),
                   jax.ShapeDtypeStruct((batch, GLA_HEADS, GLA_DK, GLA_DV), F32),
                   jax.ShapeDtypeStruct((PROJ_COLS, D_MODEL), BF16)]
                  + [jax.ShapeDtypeStruct(w.shape, BF16) for w in streamed],
        scratch_shapes=[pltpu.VMEM((chunk, PROJ_COLS), BF16),
                        pltpu.VMEM((chunk, PROJ_COLS), BF16),
                        pltpu.VMEM((chunk, NQ), F32), pltpu.VMEM((chunk, NQ), F32),
                        pltpu.VMEM((GLA_HEADS, GLA_DK, GLA_DV), F32),
                        pltpu.VMEM((GLA_HEADS, GLA_DK, GLA_DV), F32)] + _exact_scratch(chunk),
        compiler_params=_params(1),
        name="gla_prompt",
    )(x, gain, w_in, wa, wa2, ba2, o_gain, w_up, w_down, w_gate)


def _gla_sample_kernel(proj_ref, lg_ref, ogain_ref, s0_ref, mix_ref, sfin_ref, *exact_scratch, tb, s):
    def inputs(b, h):
        rows = slice(b * s, (b + 1) * s)
        q_sl, k_sl, v_sl, _ = _head_slices(h)
        return (proj_ref[rows, q_sl], proj_ref[rows, k_sl], proj_ref[rows, v_sl].astype(BF16),
                lg_ref[rows, q_sl], s0_ref[b, h])

    def write_mix(b, h, o):
        rows = slice(b * s, (b + 1) * s)
        g_sl = _head_slices(h)[3]
        mix_ref[rows, h * GLA_DV:(h + 1) * GLA_DV] = _gla_gate(o, ogain_ref[...], proj_ref[rows, g_sl])

    def chain(b, h):
        o, new_state, out_of_range = yield from _gla_chunk_stages(*inputs(b, h))
        sfin_ref[b, h] = new_state
        yield
        write_mix(b, h, o)
        return out_of_range

    pairs = [(b, h) for b in range(tb) for h in range(GLA_HEADS)]
    flags = _round_robin([chain(b, h) for b, h in pairs])
    for (b, h), out_of_range in zip(pairs, flags):
        @pl.when(out_of_range)
        def _(b=b, h=h):
            write_mix(b, h, _gla_exact_output(*inputs(b, h), exact_scratch))


def _gla_sample(proj, lg, o_gain, s0, *, batch, s, tb):
    rows = lambda cols: pl.BlockSpec((tb * s, cols), lambda i: (i, 0))
    st = pl.BlockSpec((tb, GLA_HEADS, GLA_DK, GLA_DV), lambda i: (i, 0, 0, 0))
    return pl.pallas_call(
        functools.partial(_gla_sample_kernel, tb=tb, s=s),
        grid=(batch // tb,),
        in_specs=[rows(PROJ_COLS), rows(NQ), _const_spec((1, GLA_DV)), st],
        out_specs=[rows(NV), st],
        out_shape=[jax.ShapeDtypeStruct((batch * s, NV), F32),
                   jax.ShapeDtypeStruct((batch, GLA_HEADS, GLA_DK, GLA_DV), F32)],
        scratch_shapes=_exact_scratch(s),
        compiler_params=_params(1),
        name="gla_sample",
    )(proj, lg, o_gain, s0)


def _tail_body(h_ref, mix, p_ref, wproj_ref, gmlp_ref, wup_ref, wdown_ref, gple_ref, wgate_ref,
               wple_ref, side_work=()):
    h = h_ref[...] + _dot(mix.astype(BF16), wproj_ref[...])
    hn = _rms(h, gmlp_ref[...]).astype(BF16)
    ff_chunk = D_MODEL
    for j in range(D_FF // ff_chunk):
        if j < len(side_work):
            side_work[j]()
        cols = slice(j * ff_chunk, (j + 1) * ff_chunk)
        u = jnp.square(jnp.maximum(_dot(hn, wup_ref[:, cols]), 0.0)).astype(BF16)
        h = h + _dot(u, wdown_ref[cols, :])
    gate = _sigmoid(_dot(_rms(h, gple_ref[...]).astype(BF16), wgate_ref[...]))
    return h + gate * _dot(p_ref[...].astype(BF16), wple_ref[...])


def _tail0_kernel(h_ref, mix_ref, p_ref, wproj_ref, gmlp_ref, wup_ref, wdown_ref, gple_ref, wgate_ref,
                  wple_ref, gkv_ref, wkv_ref, gq_ref, wq_ref,
                  hout_ref, kout_ref, vout_ref, qout_ref):
    h = _tail_body(h_ref, mix_ref[...], p_ref, wproj_ref, gmlp_ref, wup_ref, wdown_ref, gple_ref,
                   wgate_ref, wple_ref)
    hout_ref[...] = h
    hkv = _rms(h, gkv_ref[...]).astype(BF16)
    kout_ref[...] = _dot(hkv, wkv_ref[:, :KV_COLS])
    vout_ref[...] = _dot(hkv, wkv_ref[:, KV_COLS:])
    hq = _rms(h, gq_ref[...]).astype(BF16)
    qout_ref[...] = (_dot(hq, wq_ref[...]) * (SWA_HEAD_DIM ** -0.5 * LOG2E)).astype(qout_ref.dtype)


def _tail1_kernel(h_ref, mix_ref, p_ref, wproj_ref, gmlp_ref, wup_ref, wdown_ref, gple_ref, wgate_ref,
                  wple_ref, gfin_ref, y_ref):
    h = _tail_body(h_ref, mix_ref[...], p_ref, wproj_ref, gmlp_ref, wup_ref, wdown_ref, gple_ref,
                   wgate_ref, wple_ref)
    y_ref[...] = _rms(h, gfin_ref[...])


def _resident(shape):
    return pl.BlockSpec(shape, lambda *_: (0,) * len(shape), pipeline_mode=pl.Buffered(1))


def _layer_resident(shape, layer):
    return pl.BlockSpec((None,) + shape, lambda *_: (layer,) + (0,) * len(shape),
                        pipeline_mode=pl.Buffered(1))


def _tail_common_specs(tm, layer):
    row = lambda cols: pl.BlockSpec((tm, cols), lambda i: (i, 0))
    return [row(D_MODEL), row(D_MODEL), pl.BlockSpec((None, tm, PLE_DIM), lambda i: (layer, i, 0)),
            _resident((D_MODEL, D_MODEL)), _layer_resident((1, D_MODEL), layer),
            _layer_resident((D_MODEL, D_FF), layer), _layer_resident((D_FF, D_MODEL), layer),
            _layer_resident((1, D_MODEL), layer), _layer_resident((D_MODEL, D_MODEL), layer),
            _layer_resident((PLE_DIM, D_MODEL), layer)]


def _tail0(h, mix, p, wproj, gmlp, wup, wdown, gple, wgate, wple, gkv, wkv, gq, wq, *, tm, q_dtype):
    t = h.shape[0]
    row = lambda cols: pl.BlockSpec((tm, cols), lambda i: (i, 0))
    return pl.pallas_call(
        _tail0_kernel,
        grid=(t // tm,),
        in_specs=_tail_common_specs(tm, 0) + [
            _resident((1, D_MODEL)), _resident((D_MODEL, 2 * KV_COLS)),
            _resident((1, D_MODEL)), _resident((D_MODEL, D_MODEL))],
        out_specs=[row(D_MODEL), row(KV_COLS), row(KV_COLS), row(D_MODEL)],
        out_shape=[jax.ShapeDtypeStruct((t, D_MODEL), F32), jax.ShapeDtypeStruct((t, KV_COLS), F32),
                   jax.ShapeDtypeStruct((t, KV_COLS), F32), jax.ShapeDtypeStruct((t, D_MODEL), q_dtype)],
        compiler_params=_params(1),
        name="tail0",
    )(h, mix, p, wproj, gmlp, wup, wdown, gple, wgate, wple, gkv, wkv, gq, wq)


def _tail1(h, mix, p, wproj, gmlp, wup, wdown, gple, wgate, wple, gfin, *, tm):
    t = h.shape[0]
    return pl.pallas_call(
        _tail1_kernel,
        grid=(t // tm,),
        in_specs=_tail_common_specs(tm, 1) + [_resident((1, D_MODEL))],
        out_specs=pl.BlockSpec((tm, D_MODEL), lambda i: (i, 0)),
        out_shape=jax.ShapeDtypeStruct((t, D_MODEL), F32),
        compiler_params=_params(1),
        name="tail1",
    )(h, mix, p, wproj, gmlp, wup, wdown, gple, wgate, wple, gfin)


def _bias_table(rb_ref, head, d, valid):
    val = jnp.full(d.shape, rb_ref[BUCKET_STARTS[0][1], head], F32)
    for start, bucket in BUCKET_STARTS[1:]:
        val = jnp.where(d >= start, rb_ref[bucket, head], val)
    return jnp.where(valid, val * LOG2E, NEG)


def _softmax_sink(logits, sink):
    m = jnp.maximum(jnp.max(logits, axis=-1, keepdims=True), sink)
    e = jnp.exp2(logits - m)
    return e / (jnp.sum(e, axis=-1, keepdims=True) + jnp.exp2(sink - m))


def _lane_block(shape):
    return lax.broadcasted_iota(jnp.int32, shape, 1) // SWA_HEAD_DIM


def _swa_query_block(q_ref, r0, kblk, vblk, table, sink_ref, bias_ref):
    w = WINDOW
    rows = SWA_KV_HEADS * w
    from_prev = (lax.broadcasted_iota(jnp.int32, (rows, w), 1)
                 > (lax.broadcasted_iota(jnp.int32, (rows, w), 0) & (w - 1)))
    lane_blk = _lane_block((w, KV_COLS))
    slabs = []
    for g in range(SWA_GROUP):
        slab = q_ref[r0:r0 + w, g * KV_COLS:(g + 1) * KV_COLS].astype(F32)
        q_stack = jnp.concatenate(
            [jnp.where(lane_blk == c, slab, 0.0) for c in range(SWA_KV_HEADS)], axis=0).astype(BF16)
        logits = _dot_nt(q_stack, kblk)
        folded = jnp.where(from_prev, logits[:, :w], logits[:, w:])
        es, scales = [], []
        for c in range(SWA_KV_HEADS):
            head = c * SWA_GROUP + g
            sink = sink_ref[0, head] * LOG2E
            lg = folded[c * w:(c + 1) * w] + bias_ref[table + head]
            m = jnp.maximum(jnp.max(lg, axis=-1, keepdims=True), sink)
            e = jnp.exp2(lg - m)
            scales.append(1.0 / (jnp.sum(e, axis=-1, keepdims=True) + jnp.exp2(sink - m)))
            es.append(e)
        e_all = jnp.concatenate(es, axis=0)
        probs = jnp.concatenate([jnp.where(from_prev, e_all, 0.0).astype(BF16),
                                 jnp.where(from_prev, 0.0, e_all).astype(BF16)], axis=1)
        wide = _dot(probs, vblk)
        out = jnp.zeros((w, KV_COLS), F32)
        for c in range(SWA_KV_HEADS):
            out = jnp.where(lane_blk == c, wide[c * w:(c + 1) * w] * scales[c], out)
        slabs.append(out.astype(BF16))
    return jnp.concatenate(slabs, axis=1)


def _layer1_prompt_kernel(rb_ref, sink_ref, q_ref, kp_ref, kc_ref, vp_ref, vc_ref,
                          h_ref, p_ref, wproj_ref, gmlp_ref, wup_ref, wdown_ref, gple_ref, wgate_ref,
                          wple_ref, gfin_ref, y_ref, bias_ref, kbuf_ref, vbuf_ref, attn_ref,
                          *, qb, tiles_per_seq):
    i = pl.program_id(0)
    last = pl.num_programs(0) - 2
    w = WINDOW

    @pl.when(i == 0)
    def _():
        r = lax.broadcasted_iota(jnp.int32, (w, w), 0)
        c = lax.broadcasted_iota(jnp.int32, (w, w), 1)
        d = jnp.where(c <= r, r - c, w + r - c)
        for head in range(SWA_HEADS):
            bias_ref[head] = _bias_table(rb_ref, head, d, d >= 0)
            bias_ref[SWA_HEADS + head] = _bias_table(rb_ref, head, d, c <= r)
        attn_ref[1] = jnp.zeros(attn_ref.shape[1:], attn_ref.dtype)

    kbuf_ref[0:w] = kp_ref[...].astype(BF16)
    kbuf_ref[w:] = kc_ref[...].astype(BF16)
    vbuf_ref[0:w] = vp_ref[...].astype(BF16)
    vbuf_ref[w:] = vc_ref[...].astype(BF16)
    tile = jnp.minimum(i, last)
    first_table = jnp.where(tile % tiles_per_seq == 0, SWA_HEADS, 0)
    slot = i % 2

    def attend(jb):
        r0 = jb * w
        attn_ref[slot, r0:r0 + w, :] = _swa_query_block(
            q_ref, r0, kbuf_ref[r0:r0 + 2 * w, :], vbuf_ref[r0:r0 + 2 * w, :],
            first_table if jb == 0 else 0, sink_ref, bias_ref)

    h = _tail_body(h_ref, attn_ref[(i + 1) % 2], p_ref, wproj_ref, gmlp_ref, wup_ref, wdown_ref,
                   gple_ref, wgate_ref, wple_ref,
                   side_work=[functools.partial(attend, jb) for jb in range(qb)])
    y_ref[...] = _rms(h, gfin_ref[...])


def _layer1_prompt(rel_bias, sinks, q, k, v, h, p, wproj, gmlp, wup, wdown, gple, wgate, wple, gfin,
                   *, seq, qb):
    t = h.shape[0]
    tm = qb * WINDOW
    n = t // tm
    cur_i = lambda i: jnp.minimum(i, n - 1)
    lag_i = lambda i: jnp.maximum(i - 1, 0)
    cur = lambda cols: pl.BlockSpec((tm, cols), lambda i: (cur_i(i), 0))
    prev = lambda cols: pl.BlockSpec((WINDOW, cols), lambda i: (jnp.maximum(cur_i(i) * qb - 1, 0), 0))
    lag = lambda cols: pl.BlockSpec((tm, cols), lambda i: (lag_i(i), 0))
    smem = pl.BlockSpec(memory_space=pltpu.SMEM)
    tail_specs = _tail_common_specs(tm, 1)
    return pl.pallas_call(
        functools.partial(_layer1_prompt_kernel, qb=qb, tiles_per_seq=seq // tm),
        grid=(n + 1,),
        in_specs=[smem, smem, cur(D_MODEL), prev(KV_COLS), cur(KV_COLS), prev(KV_COLS), cur(KV_COLS),
                  lag(D_MODEL), pl.BlockSpec((None, tm, PLE_DIM), lambda i: (1, lag_i(i), 0))]
                 + tail_specs[3:] + [_resident((1, D_MODEL))],
        out_specs=lag(D_MODEL),
        out_shape=jax.ShapeDtypeStruct((t, D_MODEL), F32),
        scratch_shapes=[pltpu.VMEM((2 * SWA_HEADS, WINDOW, WINDOW), F32),
                        pltpu.VMEM(((qb + 1) * WINDOW, KV_COLS), BF16),
                        pltpu.VMEM(((qb + 1) * WINDOW, KV_COLS), BF16),
                        pltpu.VMEM((2, tm, D_MODEL), BF16)],
        compiler_params=_params(1),
        name="layer1_prompt",
    )(rel_bias, sinks, q, k, k, v, v, h, p, wproj, gmlp, wup, wdown, gple, wgate, wple, gfin)


def _swa_sample_kernel(rb_ref, sink_ref, q_ref, kn_ref, vn_ref, ck_ref, cv_ref,
                       o_ref, ok_ref, ov_ref, bias_ref, sinkcol_ref, *, tb, s, n_past):
    nk = 2 * WINDOW

    @pl.when(pl.program_id(0) == 0)
    def _():
        d = (n_past + lax.broadcasted_iota(jnp.int32, (s, nk), 0)
             - lax.broadcasted_iota(jnp.int32, (s, nk), 1))
        valid = (d >= 0) & (d < WINDOW)
        for g in range(SWA_GROUP):
            for c in range(SWA_KV_HEADS):
                head = c * SWA_GROUP + g
                r0 = (g * SWA_KV_HEADS + c) * s
                bias_ref[r0:r0 + s, :] = _bias_table(rb_ref, head, d, valid)
                sinkcol_ref[r0:r0 + s, :] = jnp.full((s, LANES), sink_ref[0, head] * LOG2E, F32)

    lane_blk = _lane_block((s, KV_COLS))
    zero_rows = jnp.zeros((WINDOW - s, KV_COLS), F32)
    kn_t = jnp.transpose(kn_ref[...])
    vn_t = jnp.transpose(vn_ref[...])
    keep_old = lax.broadcasted_iota(jnp.int32, (KV_COLS, n_past), 1) < n_past - s

    def chain(b):
        r0 = b * s
        new_shift = (n_past - s - r0) % LANES
        place = (lambda x: pltpu.roll(x, new_shift, axis=1)) if new_shift else (lambda x: x)
        ok_ref[b] = jnp.where(keep_old, pltpu.roll(ck_ref[b], n_past - s, axis=1), place(kn_t))
        ov_ref[b] = jnp.where(keep_old, pltpu.roll(cv_ref[b], n_past - s, axis=1), place(vn_t))
        yield
        k_new = jnp.concatenate([kn_ref[r0:r0 + s, :], zero_rows], axis=0).astype(BF16)
        v_new = jnp.concatenate([vn_ref[r0:r0 + s, :], zero_rows], axis=0).astype(BF16)
        pieces = []
        for g in range(SWA_GROUP):
            slab = q_ref[r0:r0 + s, g * KV_COLS:(g + 1) * KV_COLS]
            pieces += [jnp.where(lane_blk == c, slab, 0.0) for c in range(SWA_KV_HEADS)]
        q_stack = jnp.concatenate(pieces, axis=0).astype(BF16)
        yield
        logits = jnp.concatenate([_dot(q_stack, ck_ref[b].astype(BF16)), _dot_nt(q_stack, k_new)],
                                 axis=1) + bias_ref[...]
        yield
        probs = _softmax_sink(logits, sinkcol_ref[:, :1]).astype(BF16)
        yield
        wide = _dot_nt(probs[:, :n_past], cv_ref[b].astype(BF16)) + _dot(probs[:, n_past:], v_new)
        yield
        for g in range(SWA_GROUP):
            out = jnp.zeros((s, KV_COLS), F32)
            for c in range(SWA_KV_HEADS):
                w0 = (g * SWA_KV_HEADS + c) * s
                out = jnp.where(lane_blk == c, wide[w0:w0 + s], out)
            o_ref[r0:r0 + s, g * KV_COLS:(g + 1) * KV_COLS] = out

    _round_robin([chain(b) for b in range(tb)])


def _swa_sample(rel_bias, sinks, q, k_new, v_new, cache_k, cache_v, *, batch, s, tb):
    n_past = cache_k.shape[2]
    assert n_past == WINDOW and tb * s == LANES
    rows = lambda cols: pl.BlockSpec((tb * s, cols), lambda i: (i, 0))
    cache = pl.BlockSpec((tb, KV_COLS, n_past), lambda i: (i, 0, 0))
    smem = pl.BlockSpec(memory_space=pltpu.SMEM)
    return pl.pallas_call(
        functools.partial(_swa_sample_kernel, tb=tb, s=s, n_past=n_past),
        grid=(batch // tb,),
        in_specs=[smem, smem, rows(D_MODEL), rows(KV_COLS), rows(KV_COLS), cache, cache],
        out_specs=[rows(D_MODEL), cache, cache],
        out_shape=[jax.ShapeDtypeStruct((batch * s, D_MODEL), F32),
                   jax.ShapeDtypeStruct(cache_k.shape, F32), jax.ShapeDtypeStruct(cache_v.shape, F32)],
        scratch_shapes=[pltpu.VMEM((SWA_HEADS * s, 2 * WINDOW), F32),
                        pltpu.VMEM((SWA_HEADS * s, LANES), F32)],
        compiler_params=_params(1),
        name="swa_sample",
    )(rel_bias, sinks, q, k_new, v_new, cache_k, cache_v)


def _prep_weights(norm_mix, norm_mlp, norm_ple, norm_kv, norm_final, w_in_a, w_a2, b_a2, gla_o_gain,
                  w_out_a, w_kv, w_q_b, w_o_b, w_up, w_down, w_ple, w_ple_gate):
    wa = jnp.pad(jnp.transpose(w_in_a[0])[PROJ_COLS:], ((0, RANK_PAD - GLA_RANK), (0, 0)))
    wa2 = jnp.pad(w_a2[0], ((0, RANK_PAD - GLA_RANK), (0, 0)))
    wq_b = w_q_b[0].reshape(D_MODEL, SWA_KV_HEADS, SWA_GROUP, SWA_HEAD_DIM).transpose(0, 2, 1, 3)
    wo_b = w_o_b[0].reshape(SWA_KV_HEADS, SWA_GROUP, SWA_HEAD_DIM, D_MODEL).transpose(1, 0, 2, 3)
    bf = lambda w: w.astype(BF16)
    row = lambda g: g.reshape(1, -1)
    layers = lambda g: g.reshape(g.shape[0], 1, -1)
    proj_w = (bf(w_out_a[0]), bf(wo_b.reshape(D_MODEL, D_MODEL)))

    def tail_weights(layer, wup_bf, wdown_bf, wgate_bf):
        return (proj_w[layer], layers(norm_mlp), wup_bf, wdown_bf, layers(norm_ple), wgate_bf, bf(w_ple))

    return dict(
        gla_gate=(bf(wa), bf(wa2), row(b_a2[0])),
        gla_gain=row(norm_mix[0]),
        o_gain=row(gla_o_gain[0]),
        tail=tail_weights,
        shared=(row(norm_kv), bf(w_kv), row(norm_mix[1]),
                bf(wq_b.reshape(D_MODEL, D_MODEL))),
        final=row(norm_final),
    )


def kernel(x_prompt, x_sample, state_gla, cache_win_k, cache_win_v, p_prompt, p_sample, norm_mix, norm_mlp, norm_ple, norm_kv, norm_final, w_in_a, w_a2, b_a2, gla_o_gain, w_out_a, w_kv, w_q_b, w_o_b, sinks, rel_bias, w_up, w_down, w_ple, w_ple_gate):
    w = _prep_weights(norm_mix, norm_mlp, norm_ple, norm_kv, norm_final, w_in_a, w_a2, b_a2, gla_o_gain,
                      w_out_a, w_kv, w_q_b, w_o_b, w_up, w_down, w_ple, w_ple_gate)
    sink_row = sinks[0].reshape(1, SWA_HEADS)

    bp, sp, _ = x_prompt.shape
    tp = bp * sp
    xp = x_prompt.reshape(tp, D_MODEL)
    pp = p_prompt.reshape(2, tp, PLE_DIM)
    mix, state_prompt, w_proj_bf, wup_bf, wdown_bf, wgate_bf = _gla_prompt(
        xp, w["gla_gain"], jnp.transpose(w_in_a[0]), *w["gla_gate"], w["o_gain"], w_up.reshape(-1, D_FF),
        w_down.reshape(-1, D_MODEL), w_ple_gate.reshape(-1, D_MODEL),
        batch=bp, seq=sp, chunk=GLA_PROMPT_CHUNK)
    tails = [w["tail"](layer, wup_bf.reshape(w_up.shape), wdown_bf.reshape(w_down.shape),
                       wgate_bf.reshape(w_ple_gate.shape)) for layer in range(2)]
    h, k_sh, v_sh, q_b = _tail0(xp, mix, pp, *tails[0], *w["shared"], tm=TOKEN_TILE, q_dtype=BF16)
    y_prompt = _layer1_prompt(rel_bias, sink_row, q_b, k_sh, v_sh, h, pp, *tails[1], w["final"],
                              seq=sp, qb=SWA_PROMPT_BLOCKS).reshape(bp, sp, D_MODEL)
    keep = min(WINDOW, sp)
    cache_shape = (bp, keep, SWA_KV_HEADS, SWA_HEAD_DIM)
    cache_k_prompt = k_sh.reshape(bp, sp, KV_COLS)[:, sp - keep:].reshape(cache_shape)
    cache_v_prompt = v_sh.reshape(bp, sp, KV_COLS)[:, sp - keep:].reshape(cache_shape)

    bs, ss, _ = x_sample.shape
    ts = bs * ss
    xs = x_sample.reshape(ts, D_MODEL)
    ps = p_sample.reshape(2, ts, PLE_DIM)
    n_past = cache_win_k.shape[1]
    proj, lg = _gla_proj(xs, w["gla_gain"], w_proj_bf, *w["gla_gate"], tm=TOKEN_TILE)
    mix, state_sample = _gla_sample(proj, lg, w["o_gain"], state_gla[0], batch=bs, s=ss,
                                    tb=GLA_SAMPLE_BATCH)
    h, k_sh, v_sh, q_b = _tail0(xs, mix, ps, *tails[0], *w["shared"], tm=TOKEN_TILE, q_dtype=F32)
    to_feature_major = lambda c: jnp.transpose(c, (0, 2, 3, 1)).reshape(bs, KV_COLS, n_past)
    from_feature_major = lambda c: jnp.transpose(
        c.reshape(bs, SWA_KV_HEADS, SWA_HEAD_DIM, n_past), (0, 3, 1, 2))
    attn, ck, cv = _swa_sample(rel_bias, sink_row, q_b, k_sh, v_sh, to_feature_major(cache_win_k),
                               to_feature_major(cache_win_v), batch=bs, s=ss, tb=SWA_SAMPLE_BATCH)
    y_sample = _tail1(h, attn, ps, *tails[1], w["final"], tm=TOKEN_TILE).reshape(bs, ss, D_MODEL)

    return (y_prompt, y_sample, state_prompt[None], state_sample[None],
            cache_k_prompt, cache_v_prompt, from_feature_major(ck), from_feature_major(cv))
```

```python
import functools
import math

import numpy as np
import jax
import jax.numpy as jnp
from jax import lax
from jax.experimental import pallas as pl
from jax.experimental.pallas import tpu as pltpu

F32 = jnp.float32
BF16 = jnp.bfloat16

D_MODEL = 1024
D_FF = 4 * D_MODEL
PLE_DIM = 256
GLA_HEADS = 4
GLA_DK = 128
GLA_DV = 256
GLA_RANK = 16
GLA_TAU = 16.0
SWA_HEADS = 16
SWA_KV_HEADS = 4
SWA_GROUP = 4
SWA_HEAD_DIM = 64
WINDOW = 128
REL_BUCKETS = 32
REL_MAX_DIST = 128
EPS = 1e-6

NQ = GLA_HEADS * GLA_DK
NV = GLA_HEADS * GLA_DV
KV_COLS = SWA_KV_HEADS * SWA_HEAD_DIM
LANES = 128
RANK_PAD = LANES
NEG = -1e30
EXP_CLAMP = 80.0
LOG2E = math.log2(math.e)
VMEM_LIMIT = 62 * 1024 * 1024

TOKEN_TILE = 512
GLA_PROMPT_CHUNK = 256
GLA_SAMPLE_BATCH = 8
SWA_PROMPT_BLOCKS = 4
SWA_SAMPLE_BATCH = 16


def _rel_bucket_starts():
    max_exact = REL_BUCKETS // 2
    starts = {}
    for d in range(WINDOW):
        if d < max_exact:
            b = d
        else:
            b = max_exact + int(np.float32(
                np.log(np.float32(d) / np.float32(max_exact)) / np.float32(math.log(REL_MAX_DIST / max_exact))
                * np.float32(REL_BUCKETS - max_exact)))
            b = min(b, REL_BUCKETS - 1)
        starts.setdefault(b, d)
    return sorted((d, b) for b, d in starts.items())


BUCKET_STARTS = _rel_bucket_starts()


def _dot(a, b):
    return jnp.dot(a, b, preferred_element_type=F32)


def _dot_nt(a, b):
    return lax.dot_general(a, b, (((1,), (1,)), ((), ())), preferred_element_type=F32)


def _dot_tn(a, b):
    return lax.dot_general(a, b, (((0,), (0,)), ((), ())), preferred_element_type=F32)


def _rms(x, gain):
    return x * lax.rsqrt(jnp.mean(x * x, axis=-1, keepdims=True) + EPS) * gain


def _sigmoid(x):
    return 1.0 / (1.0 + jnp.exp(-x))


def _const_spec(shape):
    return pl.BlockSpec(shape, lambda *_: (0,) * len(shape))


def _params():
    return pltpu.CompilerParams(dimension_semantics=("arbitrary",), vmem_limit_bytes=VMEM_LIMIT)


def _log_forget(hn, wa_ref, wa2_ref, ba2_ref):
    z = _dot(_dot_nt(hn, wa_ref[...]).astype(BF16), wa2_ref[...]) + ba2_ref[...]
    return (jnp.minimum(z, 0.0) - jnp.log(1.0 + jnp.exp(-jnp.abs(z)))) * (1.0 / GLA_TAU)


PROJ_COLS = 2 * NQ + 2 * NV
MXU_COLS = 256


def _head_slices(h):
    q0, k0, v0, g0 = h * GLA_DK, NQ + h * GLA_DK, 2 * NQ + h * GLA_DV, 2 * NQ + NV + h * GLA_DV
    return (slice(q0, q0 + GLA_DK), slice(k0, k0 + GLA_DK), slice(v0, v0 + GLA_DV),
            slice(g0, g0 + GLA_DV))


def _project_stages(hn, w_ref, wa_ref, wa2_ref, ba2_ref, proj_ref, lg_ref):
    lg_ref[...] = _log_forget(hn, wa_ref, wa2_ref, ba2_ref)
    for c0 in range(0, PROJ_COLS, MXU_COLS):
        yield
        out = _dot_nt(hn, w_ref[c0:c0 + MXU_COLS, :])
        if c0 < NQ:
            out = out * GLA_DK ** -0.5
        proj_ref[:, c0:c0 + MXU_COLS] = out.astype(proj_ref.dtype)


def _gla_proj_kernel(x_ref, gain_ref, wt_ref, wa_ref, wa2_ref, ba2_ref, proj_ref, lg_ref):
    hn = _rms(x_ref[...], gain_ref[...]).astype(BF16)
    _round_robin([_project_stages(hn, wt_ref, wa_ref, wa2_ref, ba2_ref, proj_ref, lg_ref)])


def _gla_proj(x, gain, wt, wa, wa2, ba2, *, tm):
    t = x.shape[0]
    row = lambda cols: pl.BlockSpec((tm, cols), lambda i: (i, 0))
    return pl.pallas_call(
        _gla_proj_kernel,
        grid=(t // tm,),
        in_specs=[row(D_MODEL), _const_spec((1, D_MODEL)), _const_spec((PROJ_COLS, D_MODEL)),
                  _const_spec((RANK_PAD, D_MODEL)), _const_spec((RANK_PAD, NQ)), _const_spec((1, NQ))],
        out_specs=[row(PROJ_COLS), row(NQ)],
        out_shape=[jax.ShapeDtypeStruct((t, PROJ_COLS), F32),
                   jax.ShapeDtypeStruct((t, NQ), F32)],
        compiler_params=_params(),
        name="gla_proj",
    )(x, gain, wt, wa, wa2, ba2)


def _cumsum_rows(x):
    c = x.shape[0]
    row = lax.broadcasted_iota(jnp.int32, x.shape, 0)
    shift = 1
    while shift < c:
        x = x + jnp.where(row >= shift, pltpu.roll(x, shift, axis=0), 0.0)
        shift *= 2
    return x


def _row_to_col(row):
    return jnp.transpose(jnp.broadcast_to(row, (LANES, LANES)))[:, :1]


def _round_robin(chains):
    results = [None] * len(chains)
    live = list(enumerate(chains))
    while live:
        still = []
        for idx, chain in live:
            try:
                next(chain)
                still.append((idx, chain))
            except StopIteration as done:
                results[idx] = done.value
        live = still
    return results


def _coarsen(chain, n):
    count = 0
    while True:
        try:
            next(chain)
        except StopIteration as done:
            return done.value
        count += 1
        if count % n == 0:
            yield


def _gla_chunk_stages(q, k, v, lg, state):
    c = q.shape[0]
    bc = _cumsum_rows(lg)
    last = bc[c - 1:c, :]
    mid = bc[c // 2 - 1:c // 2, :]
    out_of_range = jnp.max(jnp.maximum(bc[0:1, :] - mid, mid - last)) > EXP_CLAMP
    yield
    q_in = (q * jnp.exp(jnp.clip(bc - mid, -EXP_CLAMP, EXP_CLAMP))).astype(BF16)
    k_in = (k * jnp.exp(jnp.clip(mid - bc, -EXP_CLAMP, EXP_CLAMP))).astype(BF16)
    yield
    attn = _dot_nt(q_in, k_in)
    yield
    causal = lax.broadcasted_iota(jnp.int32, (c, c), 0) >= lax.broadcasted_iota(jnp.int32, (c, c), 1)
    attn = jnp.where(causal, attn, 0.0).astype(BF16)
    q_st = (q * jnp.exp(bc)).astype(BF16)
    yield
    o = _dot(attn, v) + _dot(q_st, state.astype(BF16))
    yield
    k_st = (k * jnp.exp(last - bc)).astype(BF16)
    decay_col = _row_to_col(jnp.exp(last))
    yield
    new_state = decay_col * state + _dot_tn(k_st, v)
    return o, new_state, out_of_range


def _gla_exact_output(q, k, v, lg, state, scratch):
    qs_ref, ks_ref, bs_ref, attn_ref = scratch
    c = q.shape[0]
    bc = _cumsum_rows(lg)
    qs_ref[...] = q
    ks_ref[...] = k
    bs_ref[...] = bc
    attn_ref[...] = jnp.zeros_like(attn_ref)
    col_id = lax.broadcasted_iota(jnp.int32, (c, c), 1)

    def key_row(j, carry):
        decay = jnp.exp(jnp.minimum(bs_ref[...] - bs_ref[pl.ds(j, 1), :], 0.0))
        col = jnp.sum(qs_ref[...] * decay * ks_ref[pl.ds(j, 1), :], axis=1, keepdims=True)
        attn_ref[...] = jnp.where(col_id == j, col, attn_ref[...])
        return carry

    lax.fori_loop(0, c, key_row, 0)
    causal = lax.broadcasted_iota(jnp.int32, (c, c), 0) >= col_id
    attn = jnp.where(causal, attn_ref[...], 0.0).astype(BF16)
    return _dot(attn, v) + _dot((q * jnp.exp(bc)).astype(BF16), state.astype(BF16))


def _exact_scratch(c):
    return [pltpu.VMEM((c, GLA_DK), F32), pltpu.VMEM((c, GLA_DK), F32), pltpu.VMEM((c, GLA_DK), F32),
            pltpu.VMEM((c, c), F32)]


def _gla_gate(o, o_gain, g):
    return _rms(o, o_gain) * (g * _sigmoid(g))


def _gla_prompt_kernel(x_ref, gain_ref, win_ref, wa_ref, wa2_ref, ba2_ref, ogain_ref,
                       wup_ref, wdown_ref, wgate_ref,
                       mix_ref, sfin_ref, wbf_ref, wup_bf_ref, wdown_bf_ref, wgate_bf_ref,
                       proj_a, proj_b, lg_a, lg_b, state_ref, sprev_ref, *exact_scratch, tiles_per_seq):
    i = pl.program_id(0)
    lag = jnp.maximum(i - 1, 0)

    @pl.when(i == 0)
    def _():
        proj_b[...] = jnp.zeros_like(proj_b)
        lg_b[...] = jnp.zeros_like(lg_b)
        wbf_ref[...] = win_ref[:PROJ_COLS, :].astype(BF16)

    @pl.when(lag % tiles_per_seq == 0)
    def _():
        state_ref[...] = jnp.zeros_like(state_ref)

    def project(proj_w, lg_w):
        hn = _rms(x_ref[...], gain_ref[...]).astype(BF16)
        yield
        yield from _project_stages(hn, wbf_ref, wa_ref, wa2_ref, ba2_ref, proj_w, lg_w)

    def head_inputs(h, proj_r, lg_r):
        q_sl, k_sl, v_sl, _ = _head_slices(h)
        return (proj_r[:, q_sl].astype(F32), proj_r[:, k_sl].astype(F32), proj_r[:, v_sl],
                lg_r[:, q_sl])

    def write_mix(h, o, proj_r):
        v_sl, g_sl = _head_slices(h)[2:]
        mix_ref[:, h * GLA_DV:(h + 1) * GLA_DV] = _gla_gate(
            o, ogain_ref[...], proj_r[:, g_sl].astype(F32)).astype(mix_ref.dtype)

    def recur(h, proj_r, lg_r):
        state = state_ref[h]
        sprev_ref[h] = state
        o, new_state, out_of_range = yield from _gla_chunk_stages(*head_inputs(h, proj_r, lg_r), state)
        state_ref[h] = new_state
        yield
        write_mix(h, o, proj_r)
        return out_of_range

    def recur_all(proj_r, lg_r):
        flags = []
        for h in range(GLA_HEADS):
            flags.append((yield from recur(h, proj_r, lg_r)))
            yield
        return flags

    def cast_mlp_weights():
        wup_bf_ref[...] = wup_ref[...].astype(BF16)
        yield
        wdown_bf_ref[...] = wdown_ref[...].astype(BF16)
        yield
        wgate_bf_ref[...] = wgate_ref[...].astype(BF16)

    def body(proj_w, lg_w, proj_r, lg_r):
        _, flags, _ = _round_robin([project(proj_w, lg_w), _coarsen(recur_all(proj_r, lg_r), 2),
                                    cast_mlp_weights()])
        for h, out_of_range in enumerate(flags):
            @pl.when(out_of_range)
            def _(h=h):
                o = _gla_exact_output(*head_inputs(h, proj_r, lg_r), sprev_ref[h], exact_scratch)
                write_mix(h, o, proj_r)

    @pl.when(i % 2 == 0)
    def _():
        body(proj_a, lg_a, proj_b, lg_b)

    @pl.when(i % 2 == 1)
    def _():
        body(proj_b, lg_b, proj_a, lg_a)

    @pl.when((lag % tiles_per_seq == tiles_per_seq - 1) & (i > 0))
    def _():
        sfin_ref[0] = state_ref[...]


def _gla_prompt(x, gain, w_in, wa, wa2, ba2, o_gain, w_up, w_down, w_gate, *, batch, seq, chunk):
    tps = seq // chunk
    n = batch * tps
    lag_i = lambda i: jnp.maximum(i - 1, 0)
    cur_i = lambda i: jnp.minimum(i, n - 1)
    streamed = (w_up, w_down, w_gate)
    assert all(w.shape[0] % (16 * n) == 0 for w in streamed)
    w_slice = lambda w: pl.BlockSpec((w.shape[0] // n, w.shape[1]), lambda i: (cur_i(i), 0))
    return pl.pallas_call(
        functools.partial(_gla_prompt_kernel, tiles_per_seq=tps),
        grid=(n + 1,),
        in_specs=[pl.BlockSpec((chunk, D_MODEL), lambda i: (cur_i(i), 0)),
                  _const_spec((1, D_MODEL)), _resident(w_in.shape),
                  _const_spec((RANK_PAD, D_MODEL)), _const_spec((RANK_PAD, NQ)), _const_spec((1, NQ)),
                  _const_spec((1, GLA_DV))] + [w_slice(w) for w in streamed],
        out_specs=[pl.BlockSpec((chunk, NV), lambda i: (lag_i(i), 0)),
                   pl.BlockSpec((1, GLA_HEADS, GLA_DK, GLA_DV), lambda i: (lag_i(i) // tps, 0, 0, 0)),
                   _const_spec((PROJ_COLS, D_MODEL))] + [w_slice(w) for w in streamed],
        out_shape=[jax.ShapeDtypeStruct((batch * seq, NV), BF16),
                   jax.ShapeDtypeStruct((batch, GLA_HEADS, GLA_DK, GLA_DV), F32),
                   jax.ShapeDtypeStruct((PROJ_COLS, D_MODEL), BF16)]
                  + [jax.ShapeDtypeStruct(w.shape, BF16) for w in streamed],
        scratch_shapes=[pltpu.VMEM((chunk, PROJ_COLS), BF16),
                        pltpu.VMEM((chunk, PROJ_COLS), BF16),
                        pltpu.VMEM((chunk, NQ), F32), pltpu.VMEM((chunk, NQ), F32),
                        pltpu.VMEM((GLA_HEADS, GLA_DK, GLA_DV), F32),
                        pltpu.VMEM((GLA_HEADS, GLA_DK, GLA_DV), F32)] + _exact_scratch(chunk),
        compiler_params=_params(),
        name="gla_prompt",
    )(x, gain, w_in, wa, wa2, ba2, o_gain, w_up, w_down, w_gate)


def _gla_sample_kernel(proj_ref, lg_ref, ogain_ref, s0_ref, mix_ref, sfin_ref, *exact_scratch, tb, s):
    def inputs(b, h):
        rows = slice(b * s, (b + 1) * s)
        q_sl, k_sl, v_sl, _ = _head_slices(h)
        return (proj_ref[rows, q_sl], proj_ref[rows, k_sl], proj_ref[rows, v_sl].astype(BF16),
                lg_ref[rows, q_sl], s0_ref[b, h])

    def write_mix(b, h, o):
        rows = slice(b * s, (b + 1) * s)
        g_sl = _head_slices(h)[3]
        mix_ref[rows, h * GLA_DV:(h + 1) * GLA_DV] = _gla_gate(o, ogain_ref[...], proj_ref[rows, g_sl])

    def chain(b, h):
        o, new_state, out_of_range = yield from _gla_chunk_stages(*inputs(b, h))
        sfin_ref[b, h] = new_state
        yield
        write_mix(b, h, o)
        return out_of_range

    pairs = [(b, h) for b in range(tb) for h in range(GLA_HEADS)]
    flags = _round_robin([chain(b, h) for b, h in pairs])
    for (b, h), out_of_range in zip(pairs, flags):
        @pl.when(out_of_range)
        def _(b=b, h=h):
            write_mix(b, h, _gla_exact_output(*inputs(b, h), exact_scratch))


def _gla_sample(proj, lg, o_gain, s0, *, batch, s, tb):
    rows = lambda cols: pl.BlockSpec((tb * s, cols), lambda i: (i, 0))
    st = pl.BlockSpec((tb, GLA_HEADS, GLA_DK, GLA_DV), lambda i: (i, 0, 0, 0))
    return pl.pallas_call(
        functools.partial(_gla_sample_kernel, tb=tb, s=s),
        grid=(batch // tb,),
        in_specs=[rows(PROJ_COLS), rows(NQ), _const_spec((1, GLA_DV)), st],
        out_specs=[rows(NV), st],
        out_shape=[jax.ShapeDtypeStruct((batch * s, NV), F32),
                   jax.ShapeDtypeStruct((batch, GLA_HEADS, GLA_DK, GLA_DV), F32)],
        scratch_shapes=_exact_scratch(s),
        compiler_params=_params(),
        name="gla_sample",
    )(proj, lg, o_gain, s0)


def _tail_body(h_ref, mix, p_ref, wproj_ref, gmlp_ref, wup_ref, wdown_ref, gple_ref, wgate_ref,
               wple_ref, side_work=()):
    h = h_ref[...] + _dot(mix.astype(BF16), wproj_ref[...])
    hn = _rms(h, gmlp_ref[...]).astype(BF16)
    ff_chunk = D_MODEL
    for j in range(D_FF // ff_chunk):
        if j < len(side_work):
            side_work[j]()
        cols = slice(j * ff_chunk, (j + 1) * ff_chunk)
        u = jnp.square(jnp.maximum(_dot(hn, wup_ref[:, cols]), 0.0)).astype(BF16)
        h = h + _dot(u, wdown_ref[cols, :])
    gate = _sigmoid(_dot(_rms(h, gple_ref[...]).astype(BF16), wgate_ref[...]))
    return h + gate * _dot(p_ref[...].astype(BF16), wple_ref[...])


def _tail0_kernel(h_ref, mix_ref, p_ref, wproj_ref, gmlp_ref, wup_ref, wdown_ref, gple_ref, wgate_ref,
                  wple_ref, gkv_ref, wkv_ref, gq_ref, wq_ref,
                  hout_ref, kout_ref, vout_ref, qout_ref):
    h = _tail_body(h_ref, mix_ref[...], p_ref, wproj_ref, gmlp_ref, wup_ref, wdown_ref, gple_ref,
                   wgate_ref, wple_ref)
    hout_ref[...] = h
    hkv = _rms(h, gkv_ref[...]).astype(BF16)
    kout_ref[...] = _dot(hkv, wkv_ref[:, :KV_COLS])
    vout_ref[...] = _dot(hkv, wkv_ref[:, KV_COLS:])
    hq = _rms(h, gq_ref[...]).astype(BF16)
    qout_ref[...] = (_dot(hq, wq_ref[...]) * (SWA_HEAD_DIM ** -0.5 * LOG2E)).astype(qout_ref.dtype)


def _tail1_kernel(h_ref, mix_ref, p_ref, wproj_ref, gmlp_ref, wup_ref, wdown_ref, gple_ref, wgate_ref,
                  wple_ref, gfin_ref, y_ref):
    h = _tail_body(h_ref, mix_ref[...], p_ref, wproj_ref, gmlp_ref, wup_ref, wdown_ref, gple_ref,
                   wgate_ref, wple_ref)
    y_ref[...] = _rms(h, gfin_ref[...])


def _resident(shape):
    return pl.BlockSpec(shape, lambda *_: (0,) * len(shape), pipeline_mode=pl.Buffered(1))


def _layer_resident(shape, layer):
    return pl.BlockSpec((None,) + shape, lambda *_: (layer,) + (0,) * len(shape),
                        pipeline_mode=pl.Buffered(1))


def _tail_common_specs(tm, layer):
    row = lambda cols: pl.BlockSpec((tm, cols), lambda i: (i, 0))
    return [row(D_MODEL), row(D_MODEL), pl.BlockSpec((None, tm, PLE_DIM), lambda i: (layer, i, 0)),
            _resident((D_MODEL, D_MODEL)), _layer_resident((1, D_MODEL), layer),
            _layer_resident((D_MODEL, D_FF), layer), _layer_resident((D_FF, D_MODEL), layer),
            _layer_resident((1, D_MODEL), layer), _layer_resident((D_MODEL, D_MODEL), layer),
            _layer_resident((PLE_DIM, D_MODEL), layer)]


def _tail0(h, mix, p, wproj, gmlp, wup, wdown, gple, wgate, wple, gkv, wkv, gq, wq, *, tm, q_dtype):
    t = h.shape[0]
    row = lambda cols: pl.BlockSpec((tm, cols), lambda i: (i, 0))
    return pl.pallas_call(
        _tail0_kernel,
        grid=(t // tm,),
        in_specs=_tail_common_specs(tm, 0) + [
            _resident((1, D_MODEL)), _resident((D_MODEL, 2 * KV_COLS)),
            _resident((1, D_MODEL)), _resident((D_MODEL, D_MODEL))],
        out_specs=[row(D_MODEL), row(KV_COLS), row(KV_COLS), row(D_MODEL)],
        out_shape=[jax.ShapeDtypeStruct((t, D_MODEL), F32), jax.ShapeDtypeStruct((t, KV_COLS), F32),
                   jax.ShapeDtypeStruct((t, KV_COLS), F32), jax.ShapeDtypeStruct((t, D_MODEL), q_dtype)],
        compiler_params=_params(),
        name="tail0",
    )(h, mix, p, wproj, gmlp, wup, wdown, gple, wgate, wple, gkv, wkv, gq, wq)


def _tail1(h, mix, p, wproj, gmlp, wup, wdown, gple, wgate, wple, gfin, *, tm):
    t = h.shape[0]
    return pl.pallas_call(
        _tail1_kernel,
        grid=(t // tm,),
        in_specs=_tail_common_specs(tm, 1) + [_resident((1, D_MODEL))],
        out_specs=pl.BlockSpec((tm, D_MODEL), lambda i: (i, 0)),
        out_shape=jax.ShapeDtypeStruct((t, D_MODEL), F32),
        compiler_params=_params(),
        name="tail1",
    )(h, mix, p, wproj, gmlp, wup, wdown, gple, wgate, wple, gfin)


def _bias_table(rb_ref, head, d, valid):
    val = jnp.full(d.shape, rb_ref[BUCKET_STARTS[0][1], head], F32)
    for start, bucket in BUCKET_STARTS[1:]:
        val = jnp.where(d >= start, rb_ref[bucket, head], val)
    return jnp.where(valid, val * LOG2E, NEG)


def _softmax_sink(logits, sink):
    m = jnp.maximum(jnp.max(logits, axis=-1, keepdims=True), sink)
    e = jnp.exp2(logits - m)
    return e / (jnp.sum(e, axis=-1, keepdims=True) + jnp.exp2(sink - m))


def _lane_block(shape):
    return lax.broadcasted_iota(jnp.int32, shape, 1) // SWA_HEAD_DIM


def _swa_query_block(q_ref, r0, kblk, vblk, table, sink_ref, bias_ref):
    w = WINDOW
    rows = SWA_KV_HEADS * w
    from_prev = (lax.broadcasted_iota(jnp.int32, (rows, w), 1)
                 > (lax.broadcasted_iota(jnp.int32, (rows, w), 0) & (w - 1)))
    lane_blk = _lane_block((w, KV_COLS))
    slabs = []
    for g in range(SWA_GROUP):
        slab = q_ref[r0:r0 + w, g * KV_COLS:(g + 1) * KV_COLS].astype(F32)
        q_stack = jnp.concatenate(
            [jnp.where(lane_blk == c, slab, 0.0) for c in range(SWA_KV_HEADS)], axis=0).astype(BF16)
        logits = _dot_nt(q_stack, kblk)
        folded = jnp.where(from_prev, logits[:, :w], logits[:, w:])
        es, scales = [], []
        for c in range(SWA_KV_HEADS):
            head = c * SWA_GROUP + g
            sink = sink_ref[0, head] * LOG2E
            lg = folded[c * w:(c + 1) * w] + bias_ref[table + head]
            m = jnp.maximum(jnp.max(lg, axis=-1, keepdims=True), sink)
            e = jnp.exp2(lg - m)
            scales.append(1.0 / (jnp.sum(e, axis=-1, keepdims=True) + jnp.exp2(sink - m)))
            es.append(e)
        e_all = jnp.concatenate(es, axis=0)
        probs = jnp.concatenate([jnp.where(from_prev, e_all, 0.0).astype(BF16),
                                 jnp.where(from_prev, 0.0, e_all).astype(BF16)], axis=1)
        wide = _dot(probs, vblk)
        out = jnp.zeros((w, KV_COLS), F32)
        for c in range(SWA_KV_HEADS):
            out = jnp.where(lane_blk == c, wide[c * w:(c + 1) * w] * scales[c], out)
        slabs.append(out.astype(BF16))
    return jnp.concatenate(slabs, axis=1)


def _layer1_prompt_kernel(rb_ref, sink_ref, q_ref, kp_ref, kc_ref, vp_ref, vc_ref,
                          h_ref, p_ref, wproj_ref, gmlp_ref, wup_ref, wdown_ref, gple_ref, wgate_ref,
                          wple_ref, gfin_ref, y_ref, bias_ref, kbuf_ref, vbuf_ref, attn_ref,
                          *, qb, tiles_per_seq):
    i = pl.program_id(0)
    last = pl.num_programs(0) - 2
    w = WINDOW
    first_table = jnp.where(i % tiles_per_seq == 0, SWA_HEADS, 0)
    slot = i % 2

    def fill_kv():
        kbuf_ref[0:w] = kp_ref[...].astype(BF16)
        kbuf_ref[w:] = kc_ref[...].astype(BF16)
        vbuf_ref[0:w] = vp_ref[...].astype(BF16)
        vbuf_ref[w:] = vc_ref[...].astype(BF16)

    def attend(jb):
        r0 = jb * w
        attn_ref[slot, r0:r0 + w, :] = _swa_query_block(
            q_ref, r0, kbuf_ref[r0:r0 + 2 * w, :], vbuf_ref[r0:r0 + 2 * w, :],
            first_table if jb == 0 else 0, sink_ref, bias_ref)

    attention = [functools.partial(attend, jb) for jb in range(qb)]

    def tail(side_work):
        h = _tail_body(h_ref, attn_ref[(i + 1) % 2], p_ref, wproj_ref, gmlp_ref, wup_ref, wdown_ref,
                       gple_ref, wgate_ref, wple_ref, side_work=side_work)
        y_ref[...] = _rms(h, gfin_ref[...])

    @pl.when(i == 0)
    def _():
        r = lax.broadcasted_iota(jnp.int32, (w, w), 0)
        c = lax.broadcasted_iota(jnp.int32, (w, w), 1)
        d = jnp.where(c <= r, r - c, w + r - c)
        for head in range(SWA_HEADS):
            bias_ref[head] = _bias_table(rb_ref, head, d, d >= 0)
            bias_ref[SWA_HEADS + head] = _bias_table(rb_ref, head, d, c <= r)
        fill_kv()
        for work in attention:
            work()

    @pl.when((i > 0) & (i <= last))
    def _():
        fill_kv()
        tail(attention)

    @pl.when(i > last)
    def _():
        tail(())


def _layer1_prompt(rel_bias, sinks, q, k, v, h, p, wproj, gmlp, wup, wdown, gple, wgate, wple, gfin,
                   *, seq, qb):
    t = h.shape[0]
    tm = qb * WINDOW
    n = t // tm
    cur_i = lambda i: jnp.minimum(i, n - 1)
    lag_i = lambda i: jnp.maximum(i - 1, 0)
    cur = lambda cols: pl.BlockSpec((tm, cols), lambda i: (cur_i(i), 0))
    prev = lambda cols: pl.BlockSpec((WINDOW, cols), lambda i: (jnp.maximum(cur_i(i) * qb - 1, 0), 0))
    lag = lambda cols: pl.BlockSpec((tm, cols), lambda i: (lag_i(i), 0))
    smem = pl.BlockSpec(memory_space=pltpu.SMEM)
    tail_specs = _tail_common_specs(tm, 1)
    return pl.pallas_call(
        functools.partial(_layer1_prompt_kernel, qb=qb, tiles_per_seq=seq // tm),
        grid=(n + 1,),
        in_specs=[smem, smem, cur(D_MODEL), prev(KV_COLS), cur(KV_COLS), prev(KV_COLS), cur(KV_COLS),
                  lag(D_MODEL), pl.BlockSpec((None, tm, PLE_DIM), lambda i: (1, lag_i(i), 0))]
                 + tail_specs[3:] + [_resident((1, D_MODEL))],
        out_specs=lag(D_MODEL),
        out_shape=jax.ShapeDtypeStruct((t, D_MODEL), F32),
        scratch_shapes=[pltpu.VMEM((2 * SWA_HEADS, WINDOW, WINDOW), F32),
                        pltpu.VMEM(((qb + 1) * WINDOW, KV_COLS), BF16),
                        pltpu.VMEM(((qb + 1) * WINDOW, KV_COLS), BF16),
                        pltpu.VMEM((2, tm, D_MODEL), BF16)],
        compiler_params=_params(),
        name="layer1_prompt",
    )(rel_bias, sinks, q, k, k, v, v, h, p, wproj, gmlp, wup, wdown, gple, wgate, wple, gfin)


def _swa_sample_kernel(rb_ref, sink_ref, q_ref, kn_ref, vn_ref, ck_ref, cv_ref,
                       o_ref, ok_ref, ov_ref, bias_ref, sinkcol_ref, *, tb, s, n_past):
    nk = 2 * WINDOW

    @pl.when(pl.program_id(0) == 0)
    def _():
        d = (n_past + lax.broadcasted_iota(jnp.int32, (s, nk), 0)
             - lax.broadcasted_iota(jnp.int32, (s, nk), 1))
        valid = (d >= 0) & (d < WINDOW)
        for g in range(SWA_GROUP):
            for c in range(SWA_KV_HEADS):
                head = c * SWA_GROUP + g
                r0 = (g * SWA_KV_HEADS + c) * s
                bias_ref[r0:r0 + s, :] = _bias_table(rb_ref, head, d, valid)
                sinkcol_ref[r0:r0 + s, :] = jnp.full((s, LANES), sink_ref[0, head] * LOG2E, F32)

    lane_blk = _lane_block((s, KV_COLS))
    zero_rows = jnp.zeros((WINDOW - s, KV_COLS), F32)
    kn_t = jnp.transpose(kn_ref[...])
    vn_t = jnp.transpose(vn_ref[...])
    keep_old = lax.broadcasted_iota(jnp.int32, (KV_COLS, n_past), 1) < n_past - s

    def chain(b):
        r0 = b * s
        new_shift = (n_past - s - r0) % LANES
        place = (lambda x: pltpu.roll(x, new_shift, axis=1)) if new_shift else (lambda x: x)
        ok_ref[b] = jnp.where(keep_old, pltpu.roll(ck_ref[b], n_past - s, axis=1), place(kn_t))
        ov_ref[b] = jnp.where(keep_old, pltpu.roll(cv_ref[b], n_past - s, axis=1), place(vn_t))
        yield
        k_new = jnp.concatenate([kn_ref[r0:r0 + s, :], zero_rows], axis=0).astype(BF16)
        v_new = jnp.concatenate([vn_ref[r0:r0 + s, :], zero_rows], axis=0).astype(BF16)
        pieces = []
        for g in range(SWA_GROUP):
            slab = q_ref[r0:r0 + s, g * KV_COLS:(g + 1) * KV_COLS]
            pieces += [jnp.where(lane_blk == c, slab, 0.0) for c in range(SWA_KV_HEADS)]
        q_stack = jnp.concatenate(pieces, axis=0).astype(BF16)
        yield
        logits = jnp.concatenate([_dot(q_stack, ck_ref[b].astype(BF16)), _dot_nt(q_stack, k_new)],
                                 axis=1) + bias_ref[...]
        yield
        probs = _softmax_sink(logits, sinkcol_ref[:, :1]).astype(BF16)
        yield
        wide = _dot_nt(probs[:, :n_past], cv_ref[b].astype(BF16)) + _dot(probs[:, n_past:], v_new)
        yield
        for g in range(SWA_GROUP):
            out = jnp.zeros((s, KV_COLS), F32)
            for c in range(SWA_KV_HEADS):
                w0 = (g * SWA_KV_HEADS + c) * s
                out = jnp.where(lane_blk == c, wide[w0:w0 + s], out)
            o_ref[r0:r0 + s, g * KV_COLS:(g + 1) * KV_COLS] = out

    _round_robin([chain(b) for b in range(tb)])


def _swa_sample(rel_bias, sinks, q, k_new, v_new, cache_k, cache_v, *, batch, s, tb):
    n_past = cache_k.shape[2]
    assert n_past == WINDOW and tb * s == LANES
    rows = lambda cols: pl.BlockSpec((tb * s, cols), lambda i: (i, 0))
    cache = pl.BlockSpec((tb, KV_COLS, n_past), lambda i: (i, 0, 0))
    smem = pl.BlockSpec(memory_space=pltpu.SMEM)
    return pl.pallas_call(
        functools.partial(_swa_sample_kernel, tb=tb, s=s, n_past=n_past),
        grid=(batch // tb,),
        in_specs=[smem, smem, rows(D_MODEL), rows(KV_COLS), rows(KV_COLS), cache, cache],
        out_specs=[rows(D_MODEL), cache, cache],
        out_shape=[jax.ShapeDtypeStruct((batch * s, D_MODEL), F32),
                   jax.ShapeDtypeStruct(cache_k.shape, F32), jax.ShapeDtypeStruct(cache_v.shape, F32)],
        scratch_shapes=[pltpu.VMEM((SWA_HEADS * s, 2 * WINDOW), F32),
                        pltpu.VMEM((SWA_HEADS * s, LANES), F32)],
        compiler_params=_params(),
        name="swa_sample",
    )(rel_bias, sinks, q, k_new, v_new, cache_k, cache_v)


def _prep_weights(norm_mix, norm_mlp, norm_ple, norm_kv, norm_final, w_in_a, w_a2, b_a2, gla_o_gain,
                  w_out_a, w_kv, w_q_b, w_o_b, w_up, w_down, w_ple, w_ple_gate):
    wa = jnp.pad(jnp.transpose(w_in_a[0])[PROJ_COLS:], ((0, RANK_PAD - GLA_RANK), (0, 0)))
    wa2 = jnp.pad(w_a2[0], ((0, RANK_PAD - GLA_RANK), (0, 0)))
    wq_b = w_q_b[0].reshape(D_MODEL, SWA_KV_HEADS, SWA_GROUP, SWA_HEAD_DIM).transpose(0, 2, 1, 3)
    wo_b = w_o_b[0].reshape(SWA_KV_HEADS, SWA_GROUP, SWA_HEAD_DIM, D_MODEL).transpose(1, 0, 2, 3)
    bf = lambda w: w.astype(BF16)
    row = lambda g: g.reshape(1, -1)
    layers = lambda g: g.reshape(g.shape[0], 1, -1)
    proj_w = (bf(w_out_a[0]), bf(wo_b.reshape(D_MODEL, D_MODEL)))

    def tail_weights(layer, wup_bf, wdown_bf, wgate_bf):
        return (proj_w[layer], layers(norm_mlp), wup_bf, wdown_bf, layers(norm_ple), wgate_bf, bf(w_ple))

    return dict(
        gla_gate=(bf(wa), bf(wa2), row(b_a2[0])),
        gla_gain=row(norm_mix[0]),
        o_gain=row(gla_o_gain[0]),
        tail=tail_weights,
        shared=(row(norm_kv), bf(w_kv), row(norm_mix[1]),
                bf(wq_b.reshape(D_MODEL, D_MODEL))),
        final=row(norm_final),
    )


def kernel(x_prompt, x_sample, state_gla, cache_win_k, cache_win_v, p_prompt, p_sample, norm_mix, norm_mlp, norm_ple, norm_kv, norm_final, w_in_a, w_a2, b_a2, gla_o_gain, w_out_a, w_kv, w_q_b, w_o_b, sinks, rel_bias, w_up, w_down, w_ple, w_ple_gate):
    w = _prep_weights(norm_mix, norm_mlp, norm_ple, norm_kv, norm_final, w_in_a, w_a2, b_a2, gla_o_gain,
                      w_out_a, w_kv, w_q_b, w_o_b, w_up, w_down, w_ple, w_ple_gate)
    sink_row = sinks[0].reshape(1, SWA_HEADS)

    bp, sp, _ = x_prompt.shape
    tp = bp * sp
    xp = x_prompt.reshape(tp, D_MODEL)
    pp = p_prompt.reshape(2, tp, PLE_DIM)
    mix, state_prompt, w_proj_bf, wup_bf, wdown_bf, wgate_bf = _gla_prompt(
        xp, w["gla_gain"], jnp.transpose(w_in_a[0]), *w["gla_gate"], w["o_gain"], w_up.reshape(-1, D_FF),
        w_down.reshape(-1, D_MODEL), w_ple_gate.reshape(-1, D_MODEL),
        batch=bp, seq=sp, chunk=GLA_PROMPT_CHUNK)
    tails = [w["tail"](layer, wup_bf.reshape(w_up.shape), wdown_bf.reshape(w_down.shape),
                       wgate_bf.reshape(w_ple_gate.shape)) for layer in range(2)]
    h, k_sh, v_sh, q_b = _tail0(xp, mix, pp, *tails[0], *w["shared"], tm=TOKEN_TILE, q_dtype=BF16)
    y_prompt = _layer1_prompt(rel_bias, sink_row, q_b, k_sh, v_sh, h, pp, *tails[1], w["final"],
                              seq=sp, qb=SWA_PROMPT_BLOCKS).reshape(bp, sp, D_MODEL)
    keep = min(WINDOW, sp)
    cache_shape = (bp, keep, SWA_KV_HEADS, SWA_HEAD_DIM)
    cache_k_prompt = k_sh.reshape(bp, sp, KV_COLS)[:, sp - keep:].reshape(cache_shape)
    cache_v_prompt = v_sh.reshape(bp, sp, KV_COLS)[:, sp - keep:].reshape(cache_shape)

    bs, ss, _ = x_sample.shape
    ts = bs * ss
    xs = x_sample.reshape(ts, D_MODEL)
    ps = p_sample.reshape(2, ts, PLE_DIM)
    n_past = cache_win_k.shape[1]
    proj, lg = _gla_proj(xs, w["gla_gain"], w_proj_bf, *w["gla_gate"], tm=TOKEN_TILE)
    mix, state_sample = _gla_sample(proj, lg, w["o_gain"], state_gla[0], batch=bs, s=ss,
                                    tb=GLA_SAMPLE_BATCH)
    h, k_sh, v_sh, q_b = _tail0(xs, mix, ps, *tails[0], *w["shared"], tm=TOKEN_TILE, q_dtype=F32)
    to_feature_major = lambda c: jnp.transpose(c, (0, 2, 3, 1)).reshape(bs, KV_COLS, n_past)
    from_feature_major = lambda c: jnp.transpose(
        c.reshape(bs, SWA_KV_HEADS, SWA_HEAD_DIM, n_past), (0, 3, 1, 2))
    attn, ck, cv = _swa_sample(rel_bias, sink_row, q_b, k_sh, v_sh, to_feature_major(cache_win_k),
                               to_feature_major(cache_win_v), batch=bs, s=ss, tb=SWA_SAMPLE_BATCH)
    y_sample = _tail1(h, attn, ps, *tails[1], w["final"], tm=TOKEN_TILE).reshape(bs, ss, D_MODEL)

    return (y_prompt, y_sample, state_prompt[None], state_sample[None],
            cache_k_prompt, cache_v_prompt, from_feature_major(ck), from_feature_major(cv))
```
